```python
import math
import jax
import jax.numpy as jnp
from jax import lax
import numpy as np

D_MODEL = 1024
BATCH = 8
SEQ = 4096
DEPTH = 1

RET_HEADS = 4
RET_HEAD_DIM = 256
RET_WIDTH = RET_HEADS * RET_HEAD_DIM
RET_CHUNK = 128
ROPE_BASE = 10000.0
ATT_HEADS = 16
ATT_HEAD_DIM = 64
ATT_WIDTH = ATT_HEADS * ATT_HEAD_DIM
DILATION_PATTERNS = ((128, 1), (512, 4), (2048, 16))
REL_BUCKETS = 32
REL_MAX_DISTANCE = 1024
N_EXPERTS = 256
TOP_K = 8
N_GROUPS = 8
TOPK_GROUPS = 4
EXPERT_HIDDEN = 256
SHARED_HIDDEN = 256
ROUTED_SCALE = 2.5
MOE_BLOCK = 128
NORM_EPS = 1e-6
PROJ_WIDTH = 4 * RET_WIDTH + 3 * ATT_WIDTH + 2 * D_MODEL

kernel_name = 'hybrid_retention_dilated_attn_moe_block'


def rms_norm(x, gain):
    xf = x.astype(jnp.float32)
    y = xf * lax.rsqrt(jnp.mean(xf * xf, axis=-1, keepdims=True) + NORM_EPS)
    return (y * gain.astype(jnp.float32)).astype(x.dtype)


def modulate(h, shift, scale):
    return h * (1.0 + scale[:, None, :]) + shift[:, None, :]


def swiglu(h, w_gate, w_up, w_down):
    return (jax.nn.silu(h @ w_gate) * (h @ w_up)) @ w_down


def rotary(t):
    seq, dh = t.shape[1], t.shape[-1]
    half = dh // 2
    inv_freq = ROPE_BASE ** (-jnp.arange(half, dtype=jnp.float32) / half)
    ang = jnp.arange(seq, dtype=jnp.float32)[:, None] * inv_freq[None, :]
    cos = jnp.cos(ang)[None, :, None, :].astype(t.dtype)
    sin = jnp.sin(ang)[None, :, None, :].astype(t.dtype)
    t1, t2 = t[..., :half], t[..., half:]
    return jnp.concatenate([t1 * cos - t2 * sin, t1 * sin + t2 * cos], axis=-1)


def retention_scan(q, k, v, log_gamma, strict):
    bsz, heads, seq, dk = q.shape
    dv = v.shape[-1]
    n_chunks = seq // RET_CHUNK
    dt = q.dtype
    pos = jnp.arange(RET_CHUNK, dtype=jnp.float32)
    diff = pos[:, None] - pos[None, :]
    allowed = (diff > 0) if strict else (diff >= 0)
    lg = log_gamma[:, None, None]
    intra = jnp.where(allowed[None], jnp.exp(lg * jnp.where(allowed, diff, 0.0)[None]), 0.0).astype(dt)
    q_scale = jnp.exp(log_gamma[:, None] * (pos[None, :] + 1.0)).astype(dt)[None, :, :, None]
    k_scale = jnp.exp(log_gamma[:, None] * (RET_CHUNK - 1.0 - pos[None, :])).astype(dt)[None, :, :, None]
    chunk_decay = jnp.exp(log_gamma * RET_CHUNK).astype(dt)[None, :, None, None]

    def chunks(t):
        return t.reshape(bsz, heads, n_chunks, RET_CHUNK, t.shape[-1]).transpose(2, 0, 1, 3, 4)

    def step(state, inp):
        qn, kn, vn = inp
        scores = jnp.einsum('bhid,bhjd->bhij', qn, kn) * intra[None]
        out = (jnp.einsum('bhij,bhje->bhie', scores, vn)
               + jnp.einsum('bhid,bhde->bhie', qn * q_scale, state))
        state = state * chunk_decay + jnp.einsum('bhjd,bhje->bhde', kn * k_scale, vn)
        return state, out

    state0 = jnp.zeros((bsz, heads, dk, dv), dt)
    _, out = lax.scan(step, state0, (chunks(q), chunks(k), chunks(v)))
    return out.transpose(1, 2, 0, 3, 4).reshape(bsz, heads, seq, dv)


def retention_mixer(q, k, v, g, ret_decay):
    bsz, seq, _ = q.shape

    def heads(t):
        return t.reshape(bsz, seq, RET_HEADS, RET_HEAD_DIM)

    q = rotary(heads(q)).transpose(0, 2, 1, 3)
    k = (rotary(heads(k)) * (RET_HEAD_DIM ** -0.5)).transpose(0, 2, 1, 3)
    v = heads(v).transpose(0, 2, 1, 3)
    log_gamma = jnp.log1p(-jnp.exp(ret_decay.astype(jnp.float32)))
    fwd = retention_scan(q, k, v, log_gamma[0], strict=False)
    bwd = jnp.flip(retention_scan(jnp.flip(q, 2), jnp.flip(k, 2), jnp.flip(v, 2), log_gamma[1], strict=True), 2)
    o = (fwd + bwd).astype(jnp.float32)
    mu = jnp.mean(o, axis=-1, keepdims=True)
    var = jnp.mean(jnp.square(o - mu), axis=-1, keepdims=True)
    o = ((o - mu) * lax.rsqrt(var + NORM_EPS)).astype(g.dtype)
    o = o.transpose(0, 2, 1, 3).reshape(bsz, seq, RET_WIDTH)
    return o * jax.nn.silu(g)


def t5_bucket(rel):
    half = REL_BUCKETS // 2
    max_exact = half // 2
    n = jnp.abs(rel)
    large = max_exact + (jnp.log(jnp.maximum(n, 1).astype(jnp.float32) / max_exact)
                         / math.log(REL_MAX_DISTANCE / max_exact) * (half - max_exact)).astype(jnp.int32)
    large = jnp.minimum(large, half - 1)
    return jnp.where(rel > 0, half, 0) + jnp.where(n < max_exact, n, large)


def dilated_band_attention(q, k, v, t5_bias, window, dilation):
    bsz, seq, heads, dh = q.shape
    radius = window // (2 * dilation)
    blk = radius
    length = seq // dilation
    n_blk = -(-length // blk)
    padded = n_blk * blk

    def to_strided(t):
        t = t.reshape(bsz, length, dilation, heads, dh).transpose(0, 2, 3, 1, 4)
        return jnp.pad(t, ((0, 0), (0, 0), (0, 0), (0, padded - length), (0, 0)))

    def band(t):
        t = jnp.pad(to_strided(t), ((0, 0), (0, 0), (0, 0), (blk, blk), (0, 0)))
        t = t.reshape(bsz, dilation, heads, n_blk + 2, blk, dh)
        return jnp.concatenate([t[:, :, :, :-2], t[:, :, :, 1:-1], t[:, :, :, 2:]], axis=4)

    qb = to_strided(q).reshape(bsz, dilation, heads, n_blk, blk, dh)
    kb, vb = band(k), band(v)
    qi = jnp.arange(blk, dtype=jnp.int32)
    kj = jnp.arange(3 * blk, dtype=jnp.int32)
    rel = kj[None, :] - blk - qi[:, None]
    key_idx = jnp.arange(n_blk, dtype=jnp.int32)[:, None] * blk - blk + kj[None, :]
    valid = (jnp.abs(rel) <= radius)[None] & ((key_idx >= 0) & (key_idx < length))[:, None, :]
    bias = t5_bias[t5_bucket(rel * dilation)].astype(jnp.float32).transpose(2, 0, 1)
    s = jnp.einsum('brhnqe,brhnke->brhnqk', qb, kb).astype(jnp.float32) * (dh ** -0.5)
    s = s + bias[None, None, :, None]
    s = jnp.where(valid[None, None, None], s, -jnp.inf)
    lse = jax.nn.logsumexp(s, axis=-1)
    p = jnp.exp(s - lse[..., None]).astype(v.dtype)
    o = jnp.einsum('brhnqk,brhnke->brhnqe', p, vb)
    o = o.reshape(bsz, dilation, heads, padded, dh)[:, :, :, :length]
    o = o.transpose(0, 3, 1, 2, 4).reshape(bsz, seq, heads, dh)
    lse = lse.reshape(bsz, dilation, heads, padded)[..., :length].transpose(0, 3, 1, 2).reshape(bsz, seq, heads)
    return o, lse


def dilated_attention(q, k, v, t5_bias):
    bsz, seq, _ = q.shape

    def heads(t):
        return t.reshape(bsz, seq, ATT_HEADS, ATT_HEAD_DIM)

    q, k, v = heads(q), heads(k), heads(v)
    outs, lses = [], []
    for window, dilation in DILATION_PATTERNS:
        o, lse = dilated_band_attention(q, k, v, t5_bias, window, dilation)
        outs.append(o)
        lses.append(lse)
    weights = jax.nn.softmax(jnp.stack(lses, axis=-1), axis=-1).astype(q.dtype)
    o = jnp.sum(jnp.stack(outs, axis=-2) * weights[..., None], axis=-2)
    return o.reshape(bsz, seq, ATT_WIDTH)


def token_mixer(h, w_in, ret_decay, t5_bias, w_ret_up, w_att_up, w_o):
    widths = [RET_WIDTH] * 4 + [ATT_WIDTH] * 3 + [D_MODEL] * 2
    splits = np.cumsum(widths)[:-1].tolist()
    proj = h @ w_in
    rq, rk, rv, rg, aq, ak, av, gate_r, gate_a = jnp.split(proj, splits, axis=-1)
    y_ret = retention_mixer(rq, rk, rv, rg, ret_decay) @ w_ret_up
    y_att = dilated_attention(aq, ak, av, t5_bias) @ w_att_up
    merged = jax.nn.sigmoid(gate_r) * y_ret + jax.nn.sigmoid(gate_a) * y_att
    return merged @ w_o


def route(h, w_router, router_bias):
    n_tok = h.shape[0]
    per_group = N_EXPERTS // N_GROUPS
    scores = jax.nn.sigmoid((h @ w_router).astype(jnp.float32))
    choice = scores + router_bias.astype(jnp.float32)
    group_score = jnp.sum(lax.top_k(choice.reshape(n_tok, N_GROUPS, per_group), 2)[0], axis=-1)
    _, top_groups = lax.top_k(group_score, TOPK_GROUPS)
    group_mask = jnp.sum(jax.nn.one_hot(top_groups, N_GROUPS, dtype=jnp.float32), axis=-2) > 0
    expert_mask = jnp.repeat(group_mask, per_group, axis=-1)
    _, expert_idx = lax.top_k(jnp.where(expert_mask, choice, -jnp.inf), TOP_K)
    gate = jnp.take_along_axis(scores, expert_idx, axis=-1)
    gate = gate / jnp.sum(gate, axis=-1, keepdims=True) * ROUTED_SCALE
    return expert_idx.astype(jnp.int32), gate


def routed_experts(h, expert_idx, gate, w_gate, w_up, w_down):
    n_tok, d = h.shape
    n_assign = n_tok * TOP_K
    n_blocks = -(-n_assign // MOE_BLOCK) + N_EXPERTS
    buf = n_blocks * MOE_BLOCK
    flat_e = expert_idx.reshape(-1)
    flat_t = jnp.repeat(jnp.arange(n_tok, dtype=jnp.int32), TOP_K)
    flat_w = gate.reshape(-1).astype(h.dtype)
    order = jnp.argsort(flat_e)
    e_s, t_s, w_s = flat_e[order], flat_t[order], flat_w[order]
    counts = jnp.bincount(flat_e, length=N_EXPERTS).astype(jnp.int32)
    padded = (counts + MOE_BLOCK - 1) // MOE_BLOCK * MOE_BLOCK
    pad_end = jnp.cumsum(padded)
    pad_start = pad_end - padded
    start = jnp.cumsum(counts) - counts
    dest = pad_start[e_s] + jnp.arange(n_assign, dtype=jnp.int32) - start[e_s]
    tok_buf = jnp.full((buf,), n_tok, jnp.int32).at[dest].set(t_s)
    w_buf = jnp.zeros((buf,), h.dtype).at[dest].set(w_s)
    block_expert = jnp.clip(jnp.searchsorted(pad_end, jnp.arange(n_blocks, dtype=jnp.int32) * MOE_BLOCK, side='right'),
                            0, N_EXPERTS - 1)
    h_pad = jnp.concatenate([h, jnp.zeros((1, d), h.dtype)], axis=0)

    def block(args):
        tok, wt, e = args
        xb = h_pad[tok]
        hid = jax.nn.silu(xb @ w_gate[e]) * (xb @ w_up[e])
        return (hid @ w_down[e]) * wt[:, None]

    y = lax.map(block, (tok_buf.reshape(n_blocks, MOE_BLOCK), w_buf.reshape(n_blocks, MOE_BLOCK), block_expert))
    return jax.ops.segment_sum(y.reshape(buf, d), tok_buf, num_segments=n_tok + 1)[:n_tok]


def setup_inputs(seed: int = 0) -> dict:
    key = jax.random.key(seed)
    ks = jax.random.split(key, 24)
    f32 = jnp.float32

    def nrm(k, shape, scale):
        return jax.random.normal(k, shape, f32) * scale

    decay_init = jnp.asarray(np.log(2.0 ** (-5.0 - np.arange(RET_HEADS))), f32)
    return {
        'x': nrm(ks[0], (BATCH, SEQ, D_MODEL), 1.0),
        'c': nrm(ks[1], (BATCH, D_MODEL), 1.0),
        'w_ada': nrm(ks[2], (DEPTH, D_MODEL, 6 * D_MODEL), 0.5 * D_MODEL ** -0.5),
        'b_ada': nrm(ks[3], (DEPTH, 6 * D_MODEL), 0.02),
        'norm_mix': 1.0 + nrm(ks[4], (DEPTH, D_MODEL), 0.02),
        'w_in': nrm(ks[5], (DEPTH, D_MODEL, PROJ_WIDTH), D_MODEL ** -0.5),
        'ret_decay': decay_init[None, None, :] + nrm(ks[6], (DEPTH, 2, RET_HEADS), 0.1),
        't5_bias': nrm(ks[7], (REL_BUCKETS, ATT_HEADS), 0.5),
        'w_ret_up': nrm(ks[8], (DEPTH, RET_WIDTH, D_MODEL), RET_WIDTH ** -0.5),
        'w_att_up': nrm(ks[9], (DEPTH, ATT_WIDTH, D_MODEL), ATT_WIDTH ** -0.5),
        'w_o': nrm(ks[10], (DEPTH, D_MODEL, D_MODEL), D_MODEL ** -0.5),
        'norm_ffn': 1.0 + nrm(ks[11], (DEPTH, D_MODEL), 0.02),
        'w_router': nrm(ks[12], (DEPTH, D_MODEL, N_EXPERTS), D_MODEL ** -0.5),
        'router_bias': nrm(ks[13], (DEPTH, N_EXPERTS), 0.01),
        'w_gate': nrm(ks[14], (DEPTH, N_EXPERTS, D_MODEL, EXPERT_HIDDEN), D_MODEL ** -0.5),
        'w_up': nrm(ks[15], (DEPTH, N_EXPERTS, D_MODEL, EXPERT_HIDDEN), D_MODEL ** -0.5),
        'w_down': nrm(ks[16], (DEPTH, N_EXPERTS, EXPERT_HIDDEN, D_MODEL), EXPERT_HIDDEN ** -0.5),
        'ws_gate': nrm(ks[17], (DEPTH, D_MODEL, SHARED_HIDDEN), D_MODEL ** -0.5),
        'ws_up': nrm(ks[18], (DEPTH, D_MODEL, SHARED_HIDDEN), D_MODEL ** -0.5),
        'ws_down': nrm(ks[19], (DEPTH, SHARED_HIDDEN, D_MODEL), SHARED_HIDDEN ** -0.5),
        'norm_final': 1.0 + nrm(ks[20], (D_MODEL,), 0.02),
    }


def reference(x, c, w_ada, b_ada, norm_mix, w_in, ret_decay, t5_bias, w_ret_up, w_att_up, w_o,
              norm_ffn, w_router, router_bias, w_gate, w_up, w_down, ws_gate, ws_up, ws_down, norm_final):
    bsz, seq, d = x.shape
    cond = jax.nn.silu(c)
    for layer in range(DEPTH):
        mod = cond @ w_ada[layer] + b_ada[layer]
        shift_m, scale_m, gate_m, shift_f, scale_f, gate_f = jnp.split(mod, 6, axis=-1)
        h = modulate(rms_norm(x, norm_mix[layer]), shift_m, scale_m)
        x = x + gate_m[:, None, :] * token_mixer(h, w_in[layer], ret_decay[layer], t5_bias,
                                                 w_ret_up[layer], w_att_up[layer], w_o[layer])
        h = modulate(rms_norm(x, norm_ffn[layer]), shift_f, scale_f).reshape(bsz * seq, d)
        expert_idx, gate = route(h, w_router[layer], router_bias[layer])
        y = (routed_experts(h, expert_idx, gate, w_gate[layer], w_up[layer], w_down[layer])
             + swiglu(h, ws_gate[layer], ws_up[layer], ws_down[layer]))
        x = x + gate_f[:, None, :] * y.reshape(bsz, seq, d)
    return rms_norm(x, norm_final)
```

```python
import functools
import math

import jax
import jax.numpy as jnp
import numpy as np
from jax import lax
from jax.experimental import pallas as pl
from jax.experimental.pallas import tpu as pltpu

F32 = jnp.float32
BF16 = jnp.bfloat16
I32 = jnp.int32

D_MODEL = 1024
RET_HEADS = 4
RET_HEAD_DIM = 256
RET_WIDTH = RET_HEADS * RET_HEAD_DIM
RET_CHUNK = 128
ROPE_BASE = 10000.0
ATT_HEADS = 16
ATT_HEAD_DIM = 64
ATT_WIDTH = ATT_HEADS * ATT_HEAD_DIM
DILATION_PATTERNS = ((128, 1), (512, 4), (2048, 16))
REL_BUCKETS = 32
REL_MAX_DISTANCE = 1024
N_EXPERTS = 256
TOP_K = 8
N_GROUPS = 8
TOPK_GROUPS = 4
EXPERT_HIDDEN = 256
ROUTED_SCALE = 2.5
MOE_BLOCK = 128
NORM_EPS = 1e-6
PROJ_WIDTH = 4 * RET_WIDTH + 3 * ATT_WIDTH + 2 * D_MODEL

LANES = 128
SUBLANES = 8
SLABS = D_MODEL // LANES
ATT_RADIUS = 64
ATT_QBLK = 128
ATT_KWIN = 256
NEG_BIG = -1e30
MIB = 1024 * 1024


def _cparams(sem, vmem_mib):
    return pltpu.CompilerParams(dimension_semantics=sem, vmem_limit_bytes=vmem_mib * MIB)


def _sigmoid(x):
    return 1.0 / (1.0 + jnp.exp(-x))


def _silu(x):
    return x * _sigmoid(x)


def _rms(x, gain):
    return x * lax.rsqrt(jnp.mean(x * x, axis=-1, keepdims=True) + NORM_EPS) * gain


def _adaln_body(c_ref, w_ref, b_ref, o_ref):
    cond = _silu(c_ref[...])
    o_ref[...] = jnp.dot(cond, w_ref[...], preferred_element_type=F32,
                         precision=lax.Precision.HIGHEST) + b_ref[...]


def _adaln(c, w, b):
    bsz, d = c.shape
    n = w.shape[1]
    return pl.pallas_call(
        _adaln_body,
        grid=(n // d,),
        in_specs=[pl.BlockSpec((bsz, d), lambda j: (0, 0)),
                  pl.BlockSpec((d, d), lambda j: (0, j)),
                  pl.BlockSpec((1, d), lambda j: (0, j))],
        out_specs=pl.BlockSpec((bsz, d), lambda j: (0, j)),
        out_shape=jax.ShapeDtypeStruct((bsz, n), F32),
        compiler_params=_cparams(("arbitrary",), 32),
        name="adaln",
    )(c, w, b.reshape(1, n))


def _inproj_body(x_ref, gain_ref, mod_ref, w_ref, o_ref, h_scr):
    @pl.when(pl.program_id(1) == 0)
    def _():
        y = _rms(x_ref[...], gain_ref[...])
        h = y * (1.0 + mod_ref[0, 1:2, :]) + mod_ref[0, 0:1, :]
        h_scr[...] = h.astype(BF16)

    o_ref[...] = jnp.dot(h_scr[...], w_ref[...], preferred_element_type=F32).astype(BF16)


def _inproj(x2, gain, mod3, w_bf, seq, tm=1024, tn=1536):
    t, d = x2.shape
    n = w_bf.shape[1]
    return pl.pallas_call(
        _inproj_body,
        grid=(t // tm, n // tn),
        in_specs=[pl.BlockSpec((tm, d), lambda i, j: (i, 0)),
                  pl.BlockSpec((1, d), lambda i, j: (0, 0)),
                  pl.BlockSpec((1, 6, d), lambda i, j: ((i * tm) // seq, 0, 0)),
                  pl.BlockSpec((d, tn), lambda i, j: (0, j))],
        out_specs=pl.BlockSpec((tm, tn), lambda i, j: (i, j)),
        out_shape=jax.ShapeDtypeStruct((t, n), BF16),
        scratch_shapes=[pltpu.VMEM((tm, d), BF16)],
        compiler_params=_cparams(("arbitrary", "arbitrary"), 48),
        name="inproj",
    )(x2, gain, mod3, w_bf)


def _ret_body(lg_ref, q_ref, k_ref, v_ref, g_ref, cos_ref, sin_ref, o_ref,
              qr_scr, kr_scr, acc_scr, st_scr):
    head = pl.program_id(1)
    seq = q_ref.shape[0]
    n_chunks = seq // RET_CHUNK
    half = RET_HEAD_DIM // 2
    rot_rows = 256

    def rot_step(c, carry):
        r = pl.ds(pl.multiple_of(c * rot_rows, rot_rows), rot_rows)
        cs = cos_ref[r, :]
        sn = sin_ref[r, :]
        for src, dst, scale in ((q_ref, qr_scr, 1.0), (k_ref, kr_scr, RET_HEAD_DIM ** -0.5)):
            t = src[r, :].astype(F32)
            t1, t2 = t[:, :half], t[:, half:]
            dst[r, :half] = ((t1 * cs - t2 * sn) * scale).astype(BF16)
            dst[r, half:] = ((t1 * sn + t2 * cs) * scale).astype(BF16)
        return carry

    lax.fori_loop(0, seq // rot_rows, rot_step, 0)

    ri = lax.broadcasted_iota(I32, (RET_CHUNK, RET_CHUNK), 0)
    ci = lax.broadcasted_iota(I32, (RET_CHUNK, RET_CHUNK), 1)
    rowpos = lax.broadcasted_iota(I32, (RET_CHUNK, RET_HEAD_DIM), 0).astype(F32)

    def direction(lg, forward):
        if forward:
            diff = (ri - ci).astype(F32)
            allowed = ri >= ci
            q_scale = jnp.exp(lg * (rowpos + 1.0))
            k_scale = jnp.exp(lg * (RET_CHUNK - 1.0 - rowpos))
        else:
            diff = (ci - ri).astype(F32)
            allowed = ci > ri
            q_scale = jnp.exp(lg * (RET_CHUNK - rowpos))
            k_scale = jnp.exp(lg * rowpos)
        intra = jnp.where(allowed, jnp.exp(lg * jnp.where(allowed, diff, 0.0)), 0.0)
        chunk_decay = jnp.exp(lg * jnp.full((1, RET_HEAD_DIM), float(RET_CHUNK), F32))
        st_scr[...] = jnp.zeros_like(st_scr)

        def step(i, carry):
            c = i if forward else n_chunks - 1 - i
            r = pl.ds(pl.multiple_of(c * RET_CHUNK, RET_CHUNK), RET_CHUNK)
            q = qr_scr[r, :]
            k = kr_scr[r, :]
            v = v_ref[r, :]
            scores = lax.dot_general(q, k, (((1,), (1,)), ((), ())),
                                     preferred_element_type=F32) * intra
            state = st_scr[...]
            out = (jnp.dot(scores.astype(BF16), v, preferred_element_type=F32)
                   + jnp.dot((q.astype(F32) * q_scale).astype(BF16), state.astype(BF16),
                             preferred_element_type=F32))
            k_t = (k.astype(F32) * k_scale).T.astype(BF16)
            st_scr[...] = state * chunk_decay + jnp.dot(k_t, v, preferred_element_type=F32)
            if forward:
                acc_scr[r, :] = out
            else:
                o = acc_scr[r, :] + out
                mu = jnp.mean(o, axis=-1, keepdims=True)
                oc = o - mu
                var = jnp.mean(oc * oc, axis=-1, keepdims=True)
                o = oc * lax.rsqrt(var + NORM_EPS)
                o_ref[r, :] = (o * _silu(g_ref[r, :].astype(F32))).astype(BF16)
            return carry

        lax.fori_loop(0, n_chunks, step, 0)

    direction(lg_ref[0, head], True)
    direction(lg_ref[1, head], False)


def _retention(proj3, log_gamma, cos, sin):
    bsz, seq, _ = proj3.shape
    hd = RET_HEAD_DIM

    def col(section):
        return pl.BlockSpec((None, seq, hd), lambda b, h, lg: (b, 0, section * RET_HEADS + h))

    return pl.pallas_call(
        _ret_body,
        grid_spec=pltpu.PrefetchScalarGridSpec(
            num_scalar_prefetch=1,
            grid=(bsz, RET_HEADS),
            in_specs=[col(0), col(1), col(2), col(3),
                      pl.BlockSpec((seq, hd // 2), lambda b, h, lg: (0, 0)),
                      pl.BlockSpec((seq, hd // 2), lambda b, h, lg: (0, 0))],
            out_specs=pl.BlockSpec((None, seq, hd), lambda b, h, lg: (b, 0, h)),
            scratch_shapes=[pltpu.VMEM((seq, hd), BF16), pltpu.VMEM((seq, hd), BF16),
                            pltpu.VMEM((seq, hd), F32), pltpu.VMEM((hd, hd), F32)]),
        out_shape=jax.ShapeDtypeStruct((bsz, seq, RET_WIDTH), BF16),
        compiler_params=_cparams(("arbitrary", "arbitrary"), 56),
        name="retention",
    )(log_gamma, proj3, proj3, proj3, proj3, cos, sin)


def _t5_bucket(rel):
    half = REL_BUCKETS // 2
    max_exact = half // 2
    n = jnp.abs(rel)
    large = max_exact + (jnp.log(jnp.maximum(n, 1).astype(F32) / max_exact)
                         / math.log(REL_MAX_DISTANCE / max_exact) * (half - max_exact)).astype(I32)
    large = jnp.minimum(large, half - 1)
    return jnp.where(rel > 0, half, 0) + jnp.where(n < max_exact, n, large)


def _attention_bias(t5_bias):
    qi = jnp.arange(ATT_QBLK, dtype=I32)[:, None]
    kj = jnp.arange(ATT_KWIN, dtype=I32)[None, :]
    tables = []
    for _, dilation in DILATION_PATTERNS:
        cases = []
        for offset in (0, -ATT_RADIUS, ATT_QBLK - ATT_KWIN):
            rel = kj + offset - qi
            bias = t5_bias[_t5_bucket(rel * dilation)].astype(F32).transpose(2, 0, 1)
            cases.append(jnp.where((jnp.abs(rel) <= ATT_RADIUS)[None], bias, NEG_BIG))
        tables.append(jnp.stack(cases, axis=1))
    return jnp.stack(tables, axis=0)


def _attn_body(q_ref, k_ref, v_ref, bias_ref, o_ref,
               qf, kf, vf, qd, kd, vd, od, ld, og, lgs):
    seq = q_ref.shape[0]
    rows = 256
    lane = lax.broadcasted_iota(I32, (ATT_QBLK, LANES), 1)
    head0 = lane < ATT_HEAD_DIM

    def to_f32(c, carry):
        r = pl.ds(pl.multiple_of(c * rows, rows), rows)
        qf[r, :] = q_ref[r, :].astype(F32)
        kf[r, :] = k_ref[r, :].astype(F32)
        vf[r, :] = v_ref[r, :].astype(F32)
        return carry

    lax.fori_loop(0, seq // rows, to_f32, 0)

    def band_blocks(g, length, q_src, k_src, v_src, o_dst, l_dst):
        n_qb = length // ATT_QBLK

        def qblock(qi, carry):
            qs = pl.multiple_of(qi * ATT_QBLK, ATT_QBLK)
            ws = pl.multiple_of(jnp.clip(qs - ATT_RADIUS, 0, length - ATT_KWIN), ATT_RADIUS)
            case = jnp.where(qi == 0, 0, jnp.where(qi == n_qb - 1, 2, 1))
            q = q_src[pl.ds(qs, ATT_QBLK), :]
            k = k_src[pl.ds(ws, ATT_KWIN), :]
            v = v_src[pl.ds(ws, ATT_KWIN), :]
            outs, lses = [], []
            for hh in range(2):
                mask = head0 if hh == 0 else jnp.logical_not(head0)
                qm = jnp.where(mask, q, jnp.zeros_like(q))
                s = lax.dot_general(qm, k, (((1,), (1,)), ((), ())), preferred_element_type=F32)
                s = s * (ATT_HEAD_DIM ** -0.5) + bias_ref[g, hh, case]
                m = jnp.max(s, axis=-1, keepdims=True)
                p = jnp.exp(s - m)
                l = jnp.sum(p, axis=-1, keepdims=True)
                outs.append(jnp.dot(p.astype(BF16), v, preferred_element_type=F32) / l)
                lses.append(m + jnp.log(l))
            o_dst[pl.ds(qs, ATT_QBLK), :] = jnp.where(head0, outs[0], outs[1])
            l_dst[pl.ds(qs, ATT_QBLK), :] = jnp.where(head0, lses[0], lses[1])
            return carry

        lax.fori_loop(0, n_qb, qblock, 0)

    for g, (_, dilation) in enumerate(DILATION_PATTERNS):
        length = seq // dilation
        if dilation == 1:
            band_blocks(g, length, q_ref, k_ref, v_ref, og.at[g], lgs.at[g])
            continue

        def residue(r, carry, g=g, dilation=dilation, length=length):
            strided = pl.ds(r, length, stride=dilation)
            dense = pl.ds(0, length)
            qd[dense, :] = qf[strided, :].astype(BF16)
            kd[dense, :] = kf[strided, :].astype(BF16)
            vd[dense, :] = vf[strided, :].astype(BF16)
            band_blocks(g, length, qd, kd, vd, od, ld)
            og[g, strided, :] = od[dense, :]
            lgs[g, strided, :] = ld[dense, :]
            return carry

        lax.fori_loop(0, dilation, residue, 0)

    def merge(c, carry):
        r = pl.ds(pl.multiple_of(c * rows, rows), rows)
        l0, l1, l2 = lgs[0, r, :], lgs[1, r, :], lgs[2, r, :]
        m = jnp.maximum(jnp.maximum(l0, l1), l2)
        w0, w1, w2 = jnp.exp(l0 - m), jnp.exp(l1 - m), jnp.exp(l2 - m)
        num = w0 * og[0, r, :] + w1 * og[1, r, :] + w2 * og[2, r, :]
        o_ref[r, :] = (num / (w0 + w1 + w2)).astype(BF16)
        return carry

    lax.fori_loop(0, seq // rows, merge, 0)


def _attention(proj3, bias_tab):
    bsz, seq, _ = proj3.shape
    n_pat = len(DILATION_PATTERNS)
    pairs = ATT_HEADS // 2
    base = 4 * RET_WIDTH // LANES

    def col(section):
        return pl.BlockSpec((None, seq, LANES),
                            lambda b, hp: (b, 0, base + section * (ATT_WIDTH // LANES) + hp))

    return pl.pallas_call(
        _attn_body,
        grid=(bsz, pairs),
        in_specs=[col(0), col(1), col(2),
                  pl.BlockSpec((n_pat, 2, 3, ATT_QBLK, ATT_KWIN), lambda b, hp: (0, hp, 0, 0, 0))],
        out_specs=pl.BlockSpec((None, seq, LANES), lambda b, hp: (b, 0, hp)),
        out_shape=jax.ShapeDtypeStruct((bsz, seq, ATT_WIDTH), BF16),
        scratch_shapes=[pltpu.VMEM((seq, LANES), F32)] * 3
                       + [pltpu.VMEM((seq // 4, LANES), BF16)] * 3
                       + [pltpu.VMEM((seq // 4, LANES), F32)] * 2
                       + [pltpu.VMEM((n_pat, seq, LANES), F32)] * 2,
        compiler_params=_cparams(("arbitrary", "arbitrary"), 56),
        name="attention",
    )(proj3, proj3, proj3, bias_tab)


def _mix_body(ret_ref, att_ref, gr_ref, ga_ref, x_ref, mod_ref, gain_ref,
              wr_ref, wa_ref, wo_ref, wrt_ref, x1_ref, hslab_ref, hrow_ref, logit_ref):
    tm = x_ref.shape[0]
    y_ret = jnp.dot(ret_ref[...], wr_ref[...], preferred_element_type=F32)
    y_att = jnp.dot(att_ref[...], wa_ref[...], preferred_element_type=F32)
    merged = (_sigmoid(gr_ref[...].astype(F32)) * y_ret
              + _sigmoid(ga_ref[...].astype(F32)) * y_att)
    mixed = jnp.dot(merged.astype(BF16), wo_ref[...], preferred_element_type=F32)
    x1 = x_ref[...] + mod_ref[0, 2:3, :] * mixed
    x1_ref[...] = x1
    h = _rms(x1, gain_ref[...]) * (1.0 + mod_ref[0, 4:5, :]) + mod_ref[0, 3:4, :]
    hrow_ref[...] = h.astype(BF16)
    for s in range(SLABS):
        hslab_ref[pl.ds(s, tm, stride=SLABS), :] = h[:, s * LANES:(s + 1) * LANES]
    logit_ref[...] = jnp.dot(h, wrt_ref[...], preferred_element_type=F32,
                             precision=lax.Precision.HIGHEST)


def _mix(ret2, att2, proj2, x2, mod3, gain, wr, wa, wo, w_router, seq, tm=512):
    t, d = x2.shape
    gate_base = (4 * RET_WIDTH + 3 * ATT_WIDTH) // d
    row = lambda i: (i, 0)
    const = lambda i: (0, 0)
    return pl.pallas_call(
        _mix_body,
        grid=(t // tm,),
        in_specs=[pl.BlockSpec((tm, d), row), pl.BlockSpec((tm, d), row),
                  pl.BlockSpec((tm, d), lambda i: (i, gate_base)),
                  pl.BlockSpec((tm, d), lambda i: (i, gate_base + 1)),
                  pl.BlockSpec((tm, d), row),
                  pl.BlockSpec((1, 6, d), lambda i: ((i * tm) // seq, 0, 0)),
                  pl.BlockSpec((1, d), const),
                  pl.BlockSpec((d, d), const), pl.BlockSpec((d, d), const),
                  pl.BlockSpec((d, d), const), pl.BlockSpec((d, N_EXPERTS), const)],
        out_specs=[pl.BlockSpec((tm, d), row),
                   pl.BlockSpec((tm * SLABS, LANES), row),
                   pl.BlockSpec((tm, d), row),
                   pl.BlockSpec((tm, N_EXPERTS), row)],
        out_shape=[jax.ShapeDtypeStruct((t, d), F32),
                   jax.ShapeDtypeStruct((t * SLABS, LANES), F32),
                   jax.ShapeDtypeStruct((t, d), BF16),
                   jax.ShapeDtypeStruct((t, N_EXPERTS), F32)],
        compiler_params=_cparams(("arbitrary",), 56),
        name="mix",
    )(ret2, att2, proj2, proj2, x2, mod3, gain, wr, wa, wo, w_router)


def _route_body(logit_ref, bias_ref, eidx_ref, gate_ref, rank_ref, cnt_ref, carry_scr, tri_scr):
    tn = logit_ref.shape[0]
    per_group = N_EXPERTS // N_GROUPS

    @pl.when(pl.program_id(0) == 0)
    def _():
        carry_scr[...] = jnp.zeros_like(carry_scr)
        r = lax.broadcasted_iota(I32, (tn, tn), 0)
        c = lax.broadcasted_iota(I32, (tn, tn), 1)
        tri_scr[...] = jnp.where(r < c, 1.0, 0.0).astype(BF16)

    scores = _sigmoid(logit_ref[...]).T
    choice = scores + jnp.concatenate([bias_ref[...]] * (tn // LANES), axis=1)
    neg_inf = -jnp.inf

    sub = lax.broadcasted_iota(I32, (per_group, tn), 0).astype(F32)
    group_score = []
    for g in range(N_GROUPS):
        cg = choice[g * per_group:(g + 1) * per_group, :]
        m1 = jnp.max(cg, axis=0, keepdims=True)
        first = jnp.min(jnp.where(cg == m1, sub, float(per_group)), axis=0, keepdims=True)
        m2 = jnp.max(jnp.where(sub == first, neg_inf, cg), axis=0, keepdims=True)
        group_score.append(m1 + m2)

    masked = []
    for a in range(N_GROUPS):
        beaten = jnp.zeros((1, tn), F32)
        for b in range(N_GROUPS):
            if b == a:
                continue
            wins = (group_score[b] >= group_score[a]) if b < a else (group_score[b] > group_score[a])
            beaten = beaten + jnp.where(wins, 1.0, 0.0)
        keep = beaten < float(TOPK_GROUPS)
        masked.append(jnp.where(keep, choice[a * per_group:(a + 1) * per_group, :], neg_inf))
    work = jnp.concatenate(masked, axis=0)

    eid = lax.broadcasted_iota(I32, (N_EXPERTS, tn), 0).astype(F32)
    picked = jnp.zeros((N_EXPERTS, tn), F32)
    idx_rows, gate_rows = [], []
    for _ in range(TOP_K):
        m = jnp.max(work, axis=0, keepdims=True)
        idx = jnp.min(jnp.where(work == m, eid, float(N_EXPERTS)), axis=0, keepdims=True)
        sel = eid == idx
        gate_rows.append(jnp.sum(jnp.where(sel, scores, 0.0), axis=0, keepdims=True))
        picked = picked + jnp.where(sel, 1.0, 0.0)
        work = jnp.where(sel, neg_inf, work)
        idx_rows.append(idx)

    before = (jnp.dot(picked.astype(BF16), tri_scr[...], preferred_element_type=F32)
              + carry_scr[:, 0:1])
    rank_rows = [jnp.sum(jnp.where(eid == idx, before, 0.0), axis=0, keepdims=True)
                 for idx in idx_rows]
    carry = carry_scr[...] + jnp.sum(picked, axis=1, keepdims=True)
    carry_scr[...] = carry
    cnt_ref[...] = carry

    gates = jnp.concatenate(gate_rows, axis=0)
    gates = gates / jnp.sum(gates, axis=0, keepdims=True) * ROUTED_SCALE
    eidx_ref[...] = jnp.concatenate(idx_rows, axis=0).astype(I32)
    gate_ref[...] = gates
    rank_ref[...] = jnp.concatenate(rank_rows, axis=0).astype(I32)


def _route(logits, bias_b, tn=512):
    t = logits.shape[0]
    tok = lambda i: (0, i)
    return pl.pallas_call(
        _route_body,
        grid=(t // tn,),
        in_specs=[pl.BlockSpec((tn, N_EXPERTS), lambda i: (i, 0)),
                  pl.BlockSpec((N_EXPERTS, LANES), lambda i: (0, 0))],
        out_specs=[pl.BlockSpec((TOP_K, tn), tok), pl.BlockSpec((TOP_K, tn), tok),
                   pl.BlockSpec((TOP_K, tn), tok),
                   pl.BlockSpec((N_EXPERTS, LANES), lambda i: (0, 0))],
        out_shape=[jax.ShapeDtypeStruct((TOP_K, t), I32), jax.ShapeDtypeStruct((TOP_K, t), F32),
                   jax.ShapeDtypeStruct((TOP_K, t), I32),
                   jax.ShapeDtypeStruct((N_EXPERTS, LANES), F32)],
        scratch_shapes=[pltpu.VMEM((N_EXPERTS, LANES), F32), pltpu.VMEM((tn, tn), BF16)],
        compiler_params=_cparams(("arbitrary",), 48),
        name="route",
    )(logits, bias_b)


def _dest_body(start_ref, eidx_ref, rank_ref, dest_ref):
    e = eidx_ref[...]

    def body(j, acc):
        return jnp.where(e == j, start_ref[j], acc)

    dest_ref[...] = rank_ref[...] + lax.fori_loop(0, N_EXPERTS, body, jnp.zeros_like(e))


def _dest(pad_start, eidx, rank, tn=2048):
    t = eidx.shape[1]
    tok = lambda i, s: (0, i)
    return pl.pallas_call(
        _dest_body,
        grid_spec=pltpu.PrefetchScalarGridSpec(
            num_scalar_prefetch=1,
            grid=(t // tn,),
            in_specs=[pl.BlockSpec((TOP_K, tn), tok), pl.BlockSpec((TOP_K, tn), tok)],
            out_specs=pl.BlockSpec((TOP_K, tn), tok)),
        out_shape=jax.ShapeDtypeStruct((TOP_K, t), I32),
        compiler_params=_cparams(("arbitrary",), 32),
        name="dest",
    )(pad_start, eidx, rank)


def _row_copy(src_hbm, src_row, dst, dst_row, sem):
    return pltpu.make_async_copy(src_hbm.at[src_row], dst.at[dst_row], sem)


def _dispatch_body(dest_hbm, h_hbm, xs_hbm, dest_smem, idx_sem, row_sem):
    tt = dest_smem.shape[0] // TOP_K
    i = pl.program_id(0)
    idx_copy = pltpu.make_async_copy(dest_hbm.at[i], dest_smem, idx_sem)
    idx_copy.start()
    idx_copy.wait()
    base = i * tt

    def issue(t, carry):
        for k in range(TOP_K):
            _row_copy(h_hbm, base + t, xs_hbm, dest_smem[k * tt + t], row_sem).start()
        return carry

    lax.fori_loop(0, tt, issue, 0)

    def drain(t, carry):
        for k in range(TOP_K):
            _row_copy(h_hbm, 0, xs_hbm, 0, row_sem).wait()
        return carry

    lax.fori_loop(0, tt, drain, 0)


def _dispatch(dest_tiles, hslab3, n_rows):
    n_tiles, width = dest_tiles.shape
    return pl.pallas_call(
        _dispatch_body,
        grid=(n_tiles,),
        in_specs=[pl.BlockSpec(memory_space=pl.ANY), pl.BlockSpec(memory_space=pl.ANY)],
        out_specs=pl.BlockSpec(memory_space=pl.ANY),
        out_shape=jax.ShapeDtypeStruct((n_rows, SLABS, LANES), F32),
        scratch_shapes=[pltpu.SMEM((width,), I32), pltpu.SemaphoreType.DMA, pltpu.SemaphoreType.DMA],
        compiler_params=_cparams(("arbitrary",), 32),
        name="dispatch",
    )(dest_tiles, hslab3)


def _experts_body(bexp_ref, bnew_ref, nused_ref, xs_ref, wg_ref, wu_ref, wd_ref, ys_ref,
                  wg_s, wu_s, wd_s):
    i = pl.program_id(0)

    @pl.when(i < nused_ref[0])
    def _():
        @pl.when(bnew_ref[i] == 1)
        def _():
            wg_s[...] = wg_ref[...].astype(BF16)
            wu_s[...] = wu_ref[...].astype(BF16)
            wd_s[...] = wd_ref[...].astype(BF16)

        x = jnp.concatenate([xs_ref[pl.ds(s, MOE_BLOCK, stride=SLABS), :] for s in range(SLABS)],
                            axis=1).astype(BF16)
        hg = jnp.dot(x, wg_s[...], preferred_element_type=F32)
        hu = jnp.dot(x, wu_s[...], preferred_element_type=F32)
        y = jnp.dot((_silu(hg) * hu).astype(BF16), wd_s[...], preferred_element_type=F32)
        for s in range(SLABS):
            ys_ref[pl.ds(s, MOE_BLOCK, stride=SLABS), :] = y[:, s * LANES:(s + 1) * LANES]


def _experts(block_expert, block_new, n_used, xs2, w_gate, w_up, w_down):
    n_blocks = block_expert.shape[0]
    d, hid = w_gate.shape[1], w_gate.shape[2]
    rows = MOE_BLOCK * SLABS

    def blk(i, be, bn, nu):
        return (jnp.minimum(i, nu[0] - 1), 0)

    def wsel(i, be, bn, nu):
        return (be[jnp.minimum(i, nu[0] - 1)], 0, 0)

    return pl.pallas_call(
        _experts_body,
        grid_spec=pltpu.PrefetchScalarGridSpec(
            num_scalar_prefetch=3,
            grid=(n_blocks,),
            in_specs=[pl.BlockSpec((rows, LANES), blk),
                      pl.BlockSpec((None, d, hid), wsel),
                      pl.BlockSpec((None, d, hid), wsel),
                      pl.BlockSpec((None, hid, d), wsel)],
            out_specs=pl.BlockSpec((rows, LANES), blk),
            scratch_shapes=[pltpu.VMEM((d, hid), BF16), pltpu.VMEM((d, hid), BF16),
                            pltpu.VMEM((hid, d), BF16)]),
        out_shape=jax.ShapeDtypeStruct(xs2.shape, F32),
        compiler_params=_cparams(("arbitrary",), 32),
        name="experts",
    )(block_expert, block_new, n_used, xs2, w_gate, w_up, w_down)


def _combine_body(dest_hbm, gate_ref, ys_hbm, x1_ref, h_ref, wsg_ref, wsu_ref, wsd_ref,
                  mod_ref, gain_ref, o_ref, dest_smem, buf, idx_sem, row_sem, *, final_norm):
    tc = x1_ref.shape[0]
    i = pl.program_id(0)
    idx_copy = pltpu.make_async_copy(dest_hbm.at[i], dest_smem, idx_sem)
    idx_copy.start()
    idx_copy.wait()

    def slot(k, t):
        return buf.at[k, pl.ds(pl.multiple_of(t * SLABS, SLABS), SLABS)]

    def issue(t, carry):
        for k in range(TOP_K):
            pltpu.make_async_copy(ys_hbm.at[dest_smem[k * tc + t]], slot(k, t), row_sem).start()
        return carry

    lax.fori_loop(0, tc, issue, 0)

    h = h_ref[...]
    hid = (_silu(jnp.dot(h, wsg_ref[...], preferred_element_type=F32))
           * jnp.dot(h, wsu_ref[...], preferred_element_type=F32))
    y = jnp.dot(hid.astype(BF16), wsd_ref[...], preferred_element_type=F32)

    def drain(t, carry):
        for k in range(TOP_K):
            pltpu.make_async_copy(ys_hbm.at[0], slot(k, 0), row_sem).wait()
        return carry

    lax.fori_loop(0, tc, drain, 0)

    gates = gate_ref[...]
    for k in range(TOP_K):
        rows = jnp.concatenate([buf[k, pl.ds(s, tc, stride=SLABS), :] for s in range(SLABS)], axis=1)
        y = y + gates[:, k:k + 1] * rows
    x2 = x1_ref[...] + mod_ref[0, 5:6, :] * y
    o_ref[...] = _rms(x2, gain_ref[...]) if final_norm else x2


def _combine(dest_tiles, gate_t, ys3, x1, hrow, wsg, wsu, wsd, mod3, gain, seq, tc, final_norm):
    t, d = x1.shape
    hid = wsg.shape[1]
    row = lambda i: (i, 0)
    const = lambda i: (0, 0)
    return pl.pallas_call(
        functools.partial(_combine_body, final_norm=final_norm),
        grid=(t // tc,),
        in_specs=[pl.BlockSpec(memory_space=pl.ANY),
                  pl.BlockSpec((tc, TOP_K), row),
                  pl.BlockSpec(memory_space=pl.ANY),
                  pl.BlockSpec((tc, d), row), pl.BlockSpec((tc, d), row),
                  pl.BlockSpec((d, hid), const), pl.BlockSpec((d, hid), const),
                  pl.BlockSpec((hid, d), const),
                  pl.BlockSpec((1, 6, d), lambda i: ((i * tc) // seq, 0, 0)),
                  pl.BlockSpec((1, d), const)],
        out_specs=pl.BlockSpec((tc, d), row),
        out_shape=jax.ShapeDtypeStruct((t, d), F32),
        scratch_shapes=[pltpu.SMEM((TOP_K * tc,), I32),
                        pltpu.VMEM((TOP_K, tc * SLABS, LANES), F32),
                        pltpu.SemaphoreType.DMA, pltpu.SemaphoreType.DMA],
        compiler_params=_cparams(("arbitrary",), 48),
        name="combine",
    )(dest_tiles, gate_t, ys3, x1, hrow, wsg, wsu, wsd, mod3, gain)


def _tile_major(a, tile):
    k, t = a.shape
    return a.reshape(k, t // tile, tile).transpose(1, 0, 2).reshape(t // tile, k * tile)


def kernel(x, c, w_ada, b_ada, norm_mix, w_in, ret_decay, t5_bias, w_ret_up, w_att_up, w_o,
           norm_ffn, w_router, router_bias, w_gate, w_up, w_down, ws_gate, ws_up, ws_down, norm_final):
    bsz, seq, d = x.shape
    depth = w_ada.shape[0]
    t = bsz * seq
    assert d == D_MODEL and seq % (ATT_KWIN * DILATION_PATTERNS[-1][1]) == 0

    half = RET_HEAD_DIM // 2
    inv_freq = ROPE_BASE ** (-jnp.arange(half, dtype=F32) / half)
    ang = jnp.arange(seq, dtype=F32)[:, None] * inv_freq[None, :]
    cos, sin = jnp.cos(ang), jnp.sin(ang)
    bias_tab = _attention_bias(t5_bias)

    n_assign = t * TOP_K
    n_blocks = -(-n_assign // MOE_BLOCK) + N_EXPERTS
    n_rows = n_blocks * MOE_BLOCK
    disp_tile = min(1024, t)
    comb_tile = min(256, t)

    x2 = x.reshape(t, d)
    for layer in range(depth):
        mod3 = _adaln(c, w_ada[layer], b_ada[layer]).reshape(bsz, 6, d)
        proj = _inproj(x2, norm_mix[layer].reshape(1, d), mod3, w_in[layer].astype(BF16), seq)
        proj3 = proj.reshape(bsz, seq, PROJ_WIDTH)
        log_gamma = jnp.log1p(-jnp.exp(ret_decay[layer].astype(F32)))
        ret = _retention(proj3, log_gamma, cos, sin)
        att = _attention(proj3, bias_tab)
        x1, hslab, hrow, logits = _mix(
            ret.reshape(t, RET_WIDTH), att.reshape(t, ATT_WIDTH), proj, x2, mod3,
            norm_ffn[layer].reshape(1, d), w_ret_up[layer].astype(BF16),
            w_att_up[layer].astype(BF16), w_o[layer].astype(BF16), w_router[layer], seq)

        bias_b = jnp.broadcast_to(router_bias[layer].astype(F32)[:, None], (N_EXPERTS, LANES))
        eidx, gate, rank, counts = _route(logits, bias_b)

        counts = counts[:, 0].astype(I32)
        padded = (counts + MOE_BLOCK - 1) // MOE_BLOCK * MOE_BLOCK
        pad_end = jnp.cumsum(padded)
        pad_start = pad_end - padded
        block_expert = jnp.clip(
            jnp.searchsorted(pad_end, jnp.arange(n_blocks, dtype=I32) * MOE_BLOCK, side='right'),
            0, N_EXPERTS - 1).astype(I32)
        block_new = jnp.concatenate(
            [jnp.ones((1,), I32), (block_expert[1:] != block_expert[:-1]).astype(I32)])
        n_used = (pad_end[-1:] // MOE_BLOCK).astype(I32)

        dest = _dest(pad_start.astype(I32), eidx, rank)
        xs = _dispatch(_tile_major(dest, disp_tile), hslab.reshape(t, SLABS, LANES), n_rows)
        ys = _experts(block_expert, block_new, n_used, xs.reshape(n_rows * SLABS, LANES),
                      w_gate[layer], w_up[layer], w_down[layer])
        x2 = _combine(_tile_major(dest, comb_tile), gate.T, ys.reshape(n_rows, SLABS, LANES),
                      x1, hrow, ws_gate[layer].astype(BF16), ws_up[layer].astype(BF16),
                      ws_down[layer].astype(BF16), mod3, norm_final.reshape(1, d), seq, comb_tile,
                      final_norm=(layer == depth - 1))
    return x2.reshape(bsz, seq, d)
```

```python
import functools
import math

import jax
import jax.numpy as jnp
import numpy as np
from jax import lax
from jax.experimental import pallas as pl
from jax.experimental.pallas import tpu as pltpu

F32 = jnp.float32
BF16 = jnp.bfloat16
I32 = jnp.int32

D_MODEL = 1024
RET_HEADS = 4
RET_HEAD_DIM = 256
RET_WIDTH = RET_HEADS * RET_HEAD_DIM
RET_CHUNK = 128
ROPE_BASE = 10000.0
ATT_HEADS = 16
ATT_HEAD_DIM = 64
ATT_WIDTH = ATT_HEADS * ATT_HEAD_DIM
DILATION_PATTERNS = ((128, 1), (512, 4), (2048, 16))
REL_BUCKETS = 32
REL_MAX_DISTANCE = 1024
N_EXPERTS = 256
TOP_K = 8
N_GROUPS = 8
TOPK_GROUPS = 4
EXPERT_HIDDEN = 256
ROUTED_SCALE = 2.5
MOE_BLOCK = 256
NORM_EPS = 1e-6
PROJ_WIDTH = 4 * RET_WIDTH + 3 * ATT_WIDTH + 2 * D_MODEL

LANES = 128
SUBLANES = 8
SLABS = D_MODEL // LANES
ATT_RADIUS = 64
ATT_QBLK = 128
ATT_KWIN = 256
ATT_UNROLL = 4
NEG_BIG = -1e30
MIB = 1024 * 1024


def _cparams(sem, vmem_mib):
    return pltpu.CompilerParams(dimension_semantics=sem, vmem_limit_bytes=vmem_mib * MIB)


def _sigmoid(x):
    return 1.0 / (1.0 + jnp.exp(-x))


def _silu(x):
    return x * _sigmoid(x)


def _rms(x, gain):
    return x * lax.rsqrt(jnp.mean(x * x, axis=-1, keepdims=True) + NORM_EPS) * gain


def _adaln_body(c_ref, w_ref, b_ref, o_ref):
    cond = _silu(c_ref[...])
    o_ref[...] = jnp.dot(cond, w_ref[...], preferred_element_type=F32,
                         precision=lax.Precision.HIGHEST) + b_ref[...]


def _adaln(c, w, b):
    bsz, d = c.shape
    n = w.shape[1]
    return pl.pallas_call(
        _adaln_body,
        grid=(n // d,),
        in_specs=[pl.BlockSpec((bsz, d), lambda j: (0, 0)),
                  pl.BlockSpec((d, d), lambda j: (0, j)),
                  pl.BlockSpec((1, d), lambda j: (0, j))],
        out_specs=pl.BlockSpec((bsz, d), lambda j: (0, j)),
        out_shape=jax.ShapeDtypeStruct((bsz, n), F32),
        compiler_params=_cparams(("arbitrary",), 32),
        name="adaln",
    )(c, w, b.reshape(1, n))


def _inproj_body(x_ref, gain_ref, mod_ref, w_ref, o_ref, h_scr):
    @pl.when(pl.program_id(1) == 0)
    def _():
        y = _rms(x_ref[...], gain_ref[...])
        h = y * (1.0 + mod_ref[0, 1:2, :]) + mod_ref[0, 0:1, :]
        h_scr[...] = h.astype(BF16)

    o_ref[...] = jnp.dot(h_scr[...], w_ref[...], preferred_element_type=F32).astype(BF16)


def _inproj(x2, gain, mod3, w_bf, seq, tm=1024, tn=1536):
    t, d = x2.shape
    n = w_bf.shape[1]
    return pl.pallas_call(
        _inproj_body,
        grid=(t // tm, n // tn),
        in_specs=[pl.BlockSpec((tm, d), lambda i, j: (i, 0)),
                  pl.BlockSpec((1, d), lambda i, j: (0, 0)),
                  pl.BlockSpec((1, 6, d), lambda i, j: ((i * tm) // seq, 0, 0)),
                  pl.BlockSpec((d, tn), lambda i, j: (0, j))],
        out_specs=pl.BlockSpec((tm, tn), lambda i, j: (i, j)),
        out_shape=jax.ShapeDtypeStruct((t, n), BF16),
        scratch_shapes=[pltpu.VMEM((tm, d), BF16)],
        compiler_params=_cparams(("arbitrary", "arbitrary"), 48),
        name="inproj",
    )(x2, gain, mod3, w_bf)


def _ret_body(lg_ref, q_ref, k_ref, v_ref, g_ref, cos_ref, sin_ref, o_ref,
              qr_scr, kr_scr, accf_scr, accb_scr, stf_scr, stb_scr):
    head = pl.program_id(1)
    seq = q_ref.shape[0]
    n_chunks = seq // RET_CHUNK
    half = RET_HEAD_DIM // 2
    rot_rows = 256

    def rot_step(c, carry):
        r = pl.ds(pl.multiple_of(c * rot_rows, rot_rows), rot_rows)
        cs = cos_ref[r, :]
        sn = sin_ref[r, :]
        for src, dst, scale in ((q_ref, qr_scr, 1.0), (k_ref, kr_scr, RET_HEAD_DIM ** -0.5)):
            t = src[r, :].astype(F32)
            t1, t2 = t[:, :half], t[:, half:]
            dst[r, :half] = ((t1 * cs - t2 * sn) * scale).astype(BF16)
            dst[r, half:] = ((t1 * sn + t2 * cs) * scale).astype(BF16)
        return carry

    lax.fori_loop(0, seq // rot_rows, rot_step, 0)

    ri = lax.broadcasted_iota(I32, (RET_CHUNK, RET_CHUNK), 0)
    ci = lax.broadcasted_iota(I32, (RET_CHUNK, RET_CHUNK), 1)
    rowpos = lax.broadcasted_iota(I32, (RET_CHUNK, RET_HEAD_DIM), 0).astype(F32)

    def decay_tables(lg, forward):
        if forward:
            diff = (ri - ci).astype(F32)
            allowed = ri >= ci
            q_scale = jnp.exp(lg * (rowpos + 1.0))
            k_scale = jnp.exp(lg * (RET_CHUNK - 1.0 - rowpos))
        else:
            diff = (ci - ri).astype(F32)
            allowed = ci > ri
            q_scale = jnp.exp(lg * (RET_CHUNK - rowpos))
            k_scale = jnp.exp(lg * rowpos)
        intra = jnp.where(allowed, jnp.exp(lg * jnp.where(allowed, diff, 0.0)), 0.0)
        chunk_decay = jnp.exp(lg * jnp.full((1, RET_HEAD_DIM), float(RET_CHUNK), F32))
        return intra, q_scale, k_scale, chunk_decay

    def chunk_update(c, tables, st_scr, acc_scr):
        intra, q_scale, k_scale, chunk_decay = tables
        r = pl.ds(pl.multiple_of(c * RET_CHUNK, RET_CHUNK), RET_CHUNK)
        q = qr_scr[r, :]
        k = kr_scr[r, :]
        v = v_ref[r, :]
        scores = lax.dot_general(q, k, (((1,), (1,)), ((), ())),
                                 preferred_element_type=F32) * intra
        state = st_scr[...]
        acc_scr[r, :] = (jnp.dot(scores.astype(BF16), v, preferred_element_type=F32)
                         + jnp.dot((q.astype(F32) * q_scale).astype(BF16), state.astype(BF16),
                                   preferred_element_type=F32))
        k_t = (k.astype(F32) * k_scale).T.astype(BF16)
        st_scr[...] = state * chunk_decay + jnp.dot(k_t, v, preferred_element_type=F32)

    fwd_tables = decay_tables(lg_ref[0, head], True)
    bwd_tables = decay_tables(lg_ref[1, head], False)
    stf_scr[...] = jnp.zeros_like(stf_scr)
    stb_scr[...] = jnp.zeros_like(stb_scr)

    def step(i, carry):
        chunk_update(i, fwd_tables, stf_scr, accf_scr)
        chunk_update(n_chunks - 1 - i, bwd_tables, stb_scr, accb_scr)
        return carry

    lax.fori_loop(0, n_chunks, step, 0)

    def finish(c, carry):
        r = pl.ds(pl.multiple_of(c * rot_rows, rot_rows), rot_rows)
        o = accf_scr[r, :] + accb_scr[r, :]
        mu = jnp.mean(o, axis=-1, keepdims=True)
        oc = o - mu
        var = jnp.mean(oc * oc, axis=-1, keepdims=True)
        o = oc * lax.rsqrt(var + NORM_EPS)
        o_ref[r, :] = (o * _silu(g_ref[r, :].astype(F32))).astype(BF16)
        return carry

    lax.fori_loop(0, seq // rot_rows, finish, 0)


def _retention(proj3, log_gamma, cos, sin):
    bsz, seq, _ = proj3.shape
    hd = RET_HEAD_DIM

    def col(section):
        return pl.BlockSpec((None, seq, hd), lambda b, h, lg: (b, 0, section * RET_HEADS + h))

    return pl.pallas_call(
        _ret_body,
        grid_spec=pltpu.PrefetchScalarGridSpec(
            num_scalar_prefetch=1,
            grid=(bsz, RET_HEADS),
            in_specs=[col(0), col(1), col(2), col(3),
                      pl.BlockSpec((seq, hd // 2), lambda b, h, lg: (0, 0)),
                      pl.BlockSpec((seq, hd // 2), lambda b, h, lg: (0, 0))],
            out_specs=pl.BlockSpec((None, seq, hd), lambda b, h, lg: (b, 0, h)),
            scratch_shapes=[pltpu.VMEM((seq, hd), BF16), pltpu.VMEM((seq, hd), BF16),
                            pltpu.VMEM((seq, hd), F32), pltpu.VMEM((seq, hd), F32),
                            pltpu.VMEM((hd, hd), F32), pltpu.VMEM((hd, hd), F32)]),
        out_shape=jax.ShapeDtypeStruct((bsz, seq, RET_WIDTH), BF16),
        compiler_params=_cparams(("arbitrary", "arbitrary"), 56),
        name="retention",
    )(log_gamma, proj3, proj3, proj3, proj3, cos, sin)


def _t5_bucket(rel):
    half = REL_BUCKETS // 2
    max_exact = half // 2
    n = jnp.abs(rel)
    large = max_exact + (jnp.log(jnp.maximum(n, 1).astype(F32) / max_exact)
                         / math.log(REL_MAX_DISTANCE / max_exact) * (half - max_exact)).astype(I32)
    large = jnp.minimum(large, half - 1)
    return jnp.where(rel > 0, half, 0) + jnp.where(n < max_exact, n, large)


def _band_buckets():
    qi = jnp.arange(ATT_QBLK, dtype=I32)[:, None]
    kj = jnp.arange(ATT_KWIN, dtype=I32)[None, :]
    tables = []
    for _, dilation in DILATION_PATTERNS:
        cases = []
        for offset in (0, -ATT_RADIUS, ATT_QBLK - ATT_KWIN):
            rel = kj + offset - qi
            cases.append(jnp.where(jnp.abs(rel) <= ATT_RADIUS, _t5_bucket(rel * dilation), -1))
        tables.append(jnp.stack(cases, axis=0))
    return jnp.stack(tables, axis=0)


def _bias_body(t5_ref, bucket_ref, o_ref):
    bucket = bucket_ref[...]

    def head(h, carry):
        acc = jnp.full(bucket.shape, NEG_BIG, F32)
        for b in range(REL_BUCKETS):
            acc = jnp.where(bucket == b, t5_ref[b, h], acc)
        o_ref[h] = acc
        return carry

    lax.fori_loop(0, ATT_HEADS, head, 0)


def _attention_bias(t5_bias):
    buckets = _band_buckets()
    n_pat, n_case = buckets.shape[:2]
    return pl.pallas_call(
        _bias_body,
        grid_spec=pltpu.PrefetchScalarGridSpec(
            num_scalar_prefetch=1,
            grid=(n_pat, n_case),
            in_specs=[pl.BlockSpec((None, None, ATT_QBLK, ATT_KWIN), lambda p, c, t5: (p, c, 0, 0))],
            out_specs=pl.BlockSpec((None, ATT_HEADS, None, ATT_QBLK, ATT_KWIN),
                                   lambda p, c, t5: (p, 0, c, 0, 0))),
        out_shape=jax.ShapeDtypeStruct((n_pat, ATT_HEADS, n_case, ATT_QBLK, ATT_KWIN), F32),
        compiler_params=_cparams(("arbitrary", "arbitrary"), 32),
        name="attn_bias",
    )(t5_bias.astype(F32), buckets)


def _attn_body(q_ref, k_ref, v_ref, bias_ref, o_ref,
               qf, kf, vf, qd, kd, vd, od, ld, og, lgs):
    seq = q_ref.shape[0]
    rows = 256
    lane = lax.broadcasted_iota(I32, (ATT_QBLK, LANES), 1)
    head0 = lane < ATT_HEAD_DIM

    def to_f32(c, carry):
        r = pl.ds(pl.multiple_of(c * rows, rows), rows)
        qf[r, :] = q_ref[r, :].astype(F32)
        kf[r, :] = k_ref[r, :].astype(F32)
        vf[r, :] = v_ref[r, :].astype(F32)
        return carry

    lax.fori_loop(0, seq // rows, to_f32, 0)

    def band_blocks(g, length, q_src, k_src, v_src, o_dst, l_dst):
        n_qb = length // ATT_QBLK
        unroll = min(ATT_UNROLL, n_qb)

        def qgroup(it, carry):
            for u in range(unroll):
                qblock(it * unroll + u)
            return carry

        def qblock(qi):
            qs = pl.multiple_of(qi * ATT_QBLK, ATT_QBLK)
            ws = pl.multiple_of(jnp.clip(qs - ATT_RADIUS, 0, length - ATT_KWIN), ATT_RADIUS)
            case = jnp.where(qi == 0, 0, jnp.where(qi == n_qb - 1, 2, 1))
            q = q_src[pl.ds(qs, ATT_QBLK), :]
            k = k_src[pl.ds(ws, ATT_KWIN), :]
            v = v_src[pl.ds(ws, ATT_KWIN), :]
            outs, lses = [], []
            for hh in range(2):
                mask = head0 if hh == 0 else jnp.logical_not(head0)
                qm = jnp.where(mask, q, jnp.zeros_like(q))
                s = lax.dot_general(qm, k, (((1,), (1,)), ((), ())), preferred_element_type=F32)
                s = s * (ATT_HEAD_DIM ** -0.5) + bias_ref[g, hh, case]
                m = jnp.max(s, axis=-1, keepdims=True)
                p = jnp.exp(s - m)
                l = jnp.sum(p, axis=-1, keepdims=True)
                outs.append(jnp.dot(p.astype(BF16), v, preferred_element_type=F32) / l)
                lses.append(m + jnp.log(l))
            o_dst[pl.ds(qs, ATT_QBLK), :] = jnp.where(head0, outs[0], outs[1])
            l_dst[pl.ds(qs, ATT_QBLK), :] = jnp.where(head0, lses[0], lses[1])

        lax.fori_loop(0, n_qb // unroll, qgroup, 0)

    for g, (_, dilation) in enumerate(DILATION_PATTERNS):
        length = seq // dilation
        if dilation == 1:
            band_blocks(g, length, q_ref, k_ref, v_ref, og.at[g], lgs.at[g])
            continue

        def residue(r, carry, g=g, dilation=dilation, length=length):
            strided = pl.ds(r, length, stride=dilation)
            dense = pl.ds(0, length)
            qd[dense, :] = qf[strided, :].astype(BF16)
            kd[dense, :] = kf[strided, :].astype(BF16)
            vd[dense, :] = vf[strided, :].astype(BF16)
            band_blocks(g, length, qd, kd, vd, od, ld)
            og[g, strided, :] = od[dense, :]
            lgs[g, strided, :] = ld[dense, :]
            return carry

        lax.fori_loop(0, dilation, residue, 0)

    def merge(c, carry):
        r = pl.ds(pl.multiple_of(c * rows, rows), rows)
        l0, l1, l2 = lgs[0, r, :], lgs[1, r, :], lgs[2, r, :]
        m = jnp.maximum(jnp.maximum(l0, l1), l2)
        w0, w1, w2 = jnp.exp(l0 - m), jnp.exp(l1 - m), jnp.exp(l2 - m)
        num = w0 * og[0, r, :] + w1 * og[1, r, :] + w2 * og[2, r, :]
        o_ref[r, :] = (num / (w0 + w1 + w2)).astype(BF16)
        return carry

    lax.fori_loop(0, seq // rows, merge, 0)


def _attention(proj3, bias_tab):
    bsz, seq, _ = proj3.shape
    n_pat = len(DILATION_PATTERNS)
    pairs = ATT_HEADS // 2
    base = 4 * RET_WIDTH // LANES

    def col(section):
        return pl.BlockSpec((None, seq, LANES),
                            lambda b, hp: (b, 0, base + section * (ATT_WIDTH // LANES) + hp))

    return pl.pallas_call(
        _attn_body,
        grid=(bsz, pairs),
        in_specs=[col(0), col(1), col(2),
                  pl.BlockSpec((n_pat, 2, 3, ATT_QBLK, ATT_KWIN), lambda b, hp: (0, hp, 0, 0, 0))],
        out_specs=pl.BlockSpec((None, seq, LANES), lambda b, hp: (b, 0, hp)),
        out_shape=jax.ShapeDtypeStruct((bsz, seq, ATT_WIDTH), BF16),
        scratch_shapes=[pltpu.VMEM((seq, LANES), F32)] * 3
                       + [pltpu.VMEM((seq // 4, LANES), BF16)] * 3
                       + [pltpu.VMEM((seq // 4, LANES), F32)] * 2
                       + [pltpu.VMEM((n_pat, seq, LANES), F32)] * 2,
        compiler_params=_cparams(("arbitrary", "arbitrary"), 56),
        name="attention",
    )(proj3, proj3, proj3, bias_tab)


def _mix_body(ret_ref, att_ref, gr_ref, ga_ref, x_ref, mod_ref, gain_ref,
              wr_ref, wa_ref, wo_ref, wrt_ref, x1_ref, hslab_ref, hrow_ref, logit_ref):
    tm = x_ref.shape[0]
    y_ret = jnp.dot(ret_ref[...], wr_ref[...], preferred_element_type=F32)
    y_att = jnp.dot(att_ref[...], wa_ref[...], preferred_element_type=F32)
    merged = (_sigmoid(gr_ref[...].astype(F32)) * y_ret
              + _sigmoid(ga_ref[...].astype(F32)) * y_att)
    mixed = jnp.dot(merged.astype(BF16), wo_ref[...], preferred_element_type=F32)
    x1 = x_ref[...] + mod_ref[0, 2:3, :] * mixed
    x1_ref[...] = x1
    h = _rms(x1, gain_ref[...]) * (1.0 + mod_ref[0, 4:5, :]) + mod_ref[0, 3:4, :]
    hrow_ref[...] = h.astype(BF16)
    for s in range(SLABS):
        hslab_ref[pl.ds(s, tm, stride=SLABS), :] = h[:, s * LANES:(s + 1) * LANES]
    logit_ref[...] = jnp.dot(h, wrt_ref[...], preferred_element_type=F32,
                             precision=lax.Precision.HIGHEST)


def _mix(ret2, att2, proj2, x2, mod3, gain, wr, wa, wo, w_router, seq, tm=512):
    t, d = x2.shape
    gate_base = (4 * RET_WIDTH + 3 * ATT_WIDTH) // d
    row = lambda i: (i, 0)
    const = lambda i: (0, 0)
    return pl.pallas_call(
        _mix_body,
        grid=(t // tm,),
        in_specs=[pl.BlockSpec((tm, d), row), pl.BlockSpec((tm, d), row),
                  pl.BlockSpec((tm, d), lambda i: (i, gate_base)),
                  pl.BlockSpec((tm, d), lambda i: (i, gate_base + 1)),
                  pl.BlockSpec((tm, d), row),
                  pl.BlockSpec((1, 6, d), lambda i: ((i * tm) // seq, 0, 0)),
                  pl.BlockSpec((1, d), const),
                  pl.BlockSpec((d, d), const), pl.BlockSpec((d, d), const),
                  pl.BlockSpec((d, d), const), pl.BlockSpec((d, N_EXPERTS), const)],
        out_specs=[pl.BlockSpec((tm, d), row),
                   pl.BlockSpec((tm * SLABS, LANES), row),
                   pl.BlockSpec((tm, d), row),
                   pl.BlockSpec((tm, N_EXPERTS), row)],
        out_shape=[jax.ShapeDtypeStruct((t, d), F32),
                   jax.ShapeDtypeStruct((t * SLABS, LANES), F32),
                   jax.ShapeDtypeStruct((t, d), BF16),
                   jax.ShapeDtypeStruct((t, N_EXPERTS), F32)],
        compiler_params=_cparams(("arbitrary",), 56),
        name="mix",
    )(ret2, att2, proj2, proj2, x2, mod3, gain, wr, wa, wo, w_router)


def _route_body(logit_ref, bias_ref, eidx_ref, gate_ref, rank_ref, cnt_ref, carry_scr, tri_scr):
    tn = logit_ref.shape[0]
    per_group = N_EXPERTS // N_GROUPS

    @pl.when(pl.program_id(0) == 0)
    def _():
        carry_scr[...] = jnp.zeros_like(carry_scr)
        r = lax.broadcasted_iota(I32, (tn, tn), 0)
        c = lax.broadcasted_iota(I32, (tn, tn), 1)
        tri_scr[...] = jnp.where(r < c, 1.0, 0.0).astype(BF16)

    scores = _sigmoid(logit_ref[...]).T
    choice = scores + jnp.concatenate([bias_ref[...]] * (tn // LANES), axis=1)
    neg_inf = -jnp.inf

    sub = lax.broadcasted_iota(I32, (per_group, tn), 0).astype(F32)
    group_score = []
    for g in range(N_GROUPS):
        cg = choice[g * per_group:(g + 1) * per_group, :]
        m1 = jnp.max(cg, axis=0, keepdims=True)
        first = jnp.min(jnp.where(cg == m1, sub, float(per_group)), axis=0, keepdims=True)
        m2 = jnp.max(jnp.where(sub == first, neg_inf, cg), axis=0, keepdims=True)
        group_score.append(m1 + m2)

    masked = []
    for a in range(N_GROUPS):
        beaten = jnp.zeros((1, tn), F32)
        for b in range(N_GROUPS):
            if b == a:
                continue
            wins = (group_score[b] >= group_score[a]) if b < a else (group_score[b] > group_score[a])
            beaten = beaten + jnp.where(wins, 1.0, 0.0)
        keep = beaten < float(TOPK_GROUPS)
        masked.append(jnp.where(keep, choice[a * per_group:(a + 1) * per_group, :], neg_inf))
    work = jnp.concatenate(masked, axis=0)

    eid = lax.broadcasted_iota(I32, (N_EXPERTS, tn), 0).astype(F32)
    picked = jnp.zeros((N_EXPERTS, tn), F32)
    idx_rows, gate_rows = [], []
    for _ in range(TOP_K):
        m = jnp.max(work, axis=0, keepdims=True)
        idx = jnp.min(jnp.where(work == m, eid, float(N_EXPERTS)), axis=0, keepdims=True)
        sel = eid == idx
        gate_rows.append(jnp.sum(jnp.where(sel, scores, 0.0), axis=0, keepdims=True))
        picked = picked + jnp.where(sel, 1.0, 0.0)
        work = jnp.where(sel, neg_inf, work)
        idx_rows.append(idx)

    before = (jnp.dot(picked.astype(BF16), tri_scr[...], preferred_element_type=F32)
              + carry_scr[:, 0:1])
    rank_rows = [jnp.sum(jnp.where(eid == idx, before, 0.0), axis=0, keepdims=True)
                 for idx in idx_rows]
    carry = carry_scr[...] + jnp.sum(picked, axis=1, keepdims=True)
    carry_scr[...] = carry
    cnt_ref[...] = carry

    gates = jnp.concatenate(gate_rows, axis=0)
    gates = gates / jnp.sum(gates, axis=0, keepdims=True) * ROUTED_SCALE
    eidx_ref[...] = jnp.concatenate(idx_rows, axis=0).astype(I32)
    gate_ref[...] = gates
    rank_ref[...] = jnp.concatenate(rank_rows, axis=0).astype(I32)


def _route(logits, bias_b, tn=512):
    t = logits.shape[0]
    tok = lambda i: (0, i)
    return pl.pallas_call(
        _route_body,
        grid=(t // tn,),
        in_specs=[pl.BlockSpec((tn, N_EXPERTS), lambda i: (i, 0)),
                  pl.BlockSpec((N_EXPERTS, LANES), lambda i: (0, 0))],
        out_specs=[pl.BlockSpec((TOP_K, tn), tok), pl.BlockSpec((TOP_K, tn), tok),
                   pl.BlockSpec((TOP_K, tn), tok),
                   pl.BlockSpec((N_EXPERTS, LANES), lambda i: (0, 0))],
        out_shape=[jax.ShapeDtypeStruct((TOP_K, t), I32), jax.ShapeDtypeStruct((TOP_K, t), F32),
                   jax.ShapeDtypeStruct((TOP_K, t), I32),
                   jax.ShapeDtypeStruct((N_EXPERTS, LANES), F32)],
        scratch_shapes=[pltpu.VMEM((N_EXPERTS, LANES), F32), pltpu.VMEM((tn, tn), BF16)],
        compiler_params=_cparams(("arbitrary",), 48),
        name="route",
    )(logits, bias_b)


def _dest_body(start_ref, eidx_ref, rank_ref, dest_ref):
    e = eidx_ref[...]

    def body(j, acc):
        return jnp.where(e == j, start_ref[j], acc)

    dest_ref[...] = rank_ref[...] + lax.fori_loop(0, N_EXPERTS, body, jnp.zeros_like(e))


def _dest(pad_start, eidx, rank, tn=2048):
    t = eidx.shape[1]
    tok = lambda i, s: (0, i)
    return pl.pallas_call(
        _dest_body,
        grid_spec=pltpu.PrefetchScalarGridSpec(
            num_scalar_prefetch=1,
            grid=(t // tn,),
            in_specs=[pl.BlockSpec((TOP_K, tn), tok), pl.BlockSpec((TOP_K, tn), tok)],
            out_specs=pl.BlockSpec((TOP_K, tn), tok)),
        out_shape=jax.ShapeDtypeStruct((TOP_K, t), I32),
        compiler_params=_cparams(("arbitrary",), 32),
        name="dest",
    )(pad_start, eidx, rank)


def _dispatch_body(dest_hbm, h_ref, xs_hbm, dest_smem, idx_sem, row_sem):
    tt = dest_smem.shape[0] // TOP_K
    idx_copy = pltpu.make_async_copy(dest_hbm.at[pl.program_id(0)], dest_smem, idx_sem)
    idx_copy.start()
    idx_copy.wait()

    def row_copy(t, dst_row):
        src = h_ref.at[pl.ds(pl.multiple_of(t * SLABS, SLABS), SLABS)]
        return pltpu.make_async_copy(src, xs_hbm.at[dst_row], row_sem)

    def issue(t, carry):
        for k in range(TOP_K):
            row_copy(t, dest_smem[k * tt + t]).start()
        return carry

    lax.fori_loop(0, tt, issue, 0)

    def drain(t, carry):
        for k in range(TOP_K):
            row_copy(0, 0).wait()
        return carry

    lax.fori_loop(0, tt, drain, 0)


def _dispatch(dest_tiles, hslab2, n_rows):
    n_tiles, width = dest_tiles.shape
    tt = width // TOP_K
    return pl.pallas_call(
        _dispatch_body,
        grid=(n_tiles,),
        in_specs=[pl.BlockSpec(memory_space=pl.ANY),
                  pl.BlockSpec((tt * SLABS, LANES), lambda i: (i, 0))],
        out_specs=pl.BlockSpec(memory_space=pl.ANY),
        out_shape=jax.ShapeDtypeStruct((n_rows, SLABS, LANES), F32),
        scratch_shapes=[pltpu.SMEM((width,), I32), pltpu.SemaphoreType.DMA, pltpu.SemaphoreType.DMA],
        compiler_params=_cparams(("arbitrary",), 32),
        name="dispatch",
    )(dest_tiles, hslab2)


def _experts_body(bexp_ref, bnew_ref, nused_ref, xs_ref, wg_ref, wu_ref, wd_ref, ys_ref,
                  wgu_s, wd_s):
    i = pl.program_id(0)
    hid = wd_s.shape[0]

    @pl.when(i < nused_ref[0])
    def _():
        @pl.when(bnew_ref[i] == 1)
        def _():
            wgu_s[:, :hid] = wg_ref[...].astype(BF16)
            wgu_s[:, hid:] = wu_ref[...].astype(BF16)
            wd_s[...] = wd_ref[...].astype(BF16)

        x = jnp.concatenate([xs_ref[pl.ds(s, MOE_BLOCK, stride=SLABS), :] for s in range(SLABS)],
                            axis=1).astype(BF16)
        gu = jnp.dot(x, wgu_s[...], preferred_element_type=F32)
        hg, hu = gu[:, :hid], gu[:, hid:]
        y = jnp.dot((_silu(hg) * hu).astype(BF16), wd_s[...], preferred_element_type=F32)
        for s in range(SLABS):
            ys_ref[pl.ds(s, MOE_BLOCK, stride=SLABS), :] = y[:, s * LANES:(s + 1) * LANES]


def _experts(block_expert, block_new, n_used, xs2, w_gate, w_up, w_down):
    n_blocks = block_expert.shape[0]
    d, hid = w_gate.shape[1], w_gate.shape[2]
    rows = MOE_BLOCK * SLABS

    def blk(i, be, bn, nu):
        return (jnp.minimum(i, nu[0] - 1), 0)

    def wsel(i, be, bn, nu):
        return (be[jnp.minimum(i, nu[0] - 1)], 0, 0)

    return pl.pallas_call(
        _experts_body,
        grid_spec=pltpu.PrefetchScalarGridSpec(
            num_scalar_prefetch=3,
            grid=(n_blocks,),
            in_specs=[pl.BlockSpec((rows, LANES), blk),
                      pl.BlockSpec((None, d, hid), wsel),
                      pl.BlockSpec((None, d, hid), wsel),
                      pl.BlockSpec((None, hid, d), wsel)],
            out_specs=pl.BlockSpec((rows, LANES), blk),
            scratch_shapes=[pltpu.VMEM((d, 2 * hid), BF16), pltpu.VMEM((hid, d), BF16)]),
        out_shape=jax.ShapeDtypeStruct(xs2.shape, F32),
        compiler_params=_cparams(("arbitrary",), 32),
        name="experts",
    )(block_expert, block_new, n_used, xs2, w_gate, w_up, w_down)


def _combine_body(dest_hbm, gate_ref, ys_hbm, x1_ref, h_ref, wsg_ref, wsu_ref, wsd_ref,
                  mod_ref, gain_ref, o_ref, dest_smem0, dest_smem1, buf, idx_sem, row_sem,
                  *, final_norm):
    tc = x1_ref.shape[0]
    i = pl.program_id(0)
    has_next = i + 1 < pl.num_programs(0)
    cur = i % 2
    dest_smem = (dest_smem0, dest_smem1)

    def index_copy(tile, slot):
        return pltpu.make_async_copy(dest_hbm.at[tile], dest_smem[slot], idx_sem)

    def row_copy(slot, k, t, src_row):
        dst = buf.at[slot, k, pl.ds(pl.multiple_of(t * SLABS, SLABS), SLABS)]
        return pltpu.make_async_copy(ys_hbm.at[src_row], dst, row_sem.at[slot])

    def start_rows(slot):
        def issue(t, carry):
            for k in range(TOP_K):
                row_copy(slot, k, t, dest_smem[slot][k * tc + t]).start()
            return carry

        lax.fori_loop(0, tc, issue, 0)

    @pl.when(i == 0)
    def _():
        index_copy(0, 0).start()
        index_copy(0, 0).wait()
        start_rows(0)

    for slot in range(2):
        @pl.when(jnp.logical_and(has_next, cur != slot))
        def _(slot=slot):
            index_copy(i + 1, slot).start()

    h = h_ref[...]
    hid = (_silu(jnp.dot(h, wsg_ref[...], preferred_element_type=F32))
           * jnp.dot(h, wsu_ref[...], preferred_element_type=F32))
    y = jnp.dot(hid.astype(BF16), wsd_ref[...], preferred_element_type=F32)

    for slot in range(2):
        @pl.when(jnp.logical_and(has_next, cur != slot))
        def _(slot=slot):
            index_copy(i + 1, slot).wait()
            start_rows(slot)

    def drain(t, carry):
        for k in range(TOP_K):
            row_copy(cur, k, 0, 0).wait()
        return carry

    lax.fori_loop(0, tc, drain, 0)

    gates = gate_ref[...]
    for k in range(TOP_K):
        rows = jnp.concatenate([buf[cur, k, pl.ds(s, tc, stride=SLABS), :] for s in range(SLABS)],
                               axis=1)
        y = y + gates[:, k:k + 1] * rows
    x2 = x1_ref[...] + mod_ref[0, 5:6, :] * y
    o_ref[...] = _rms(x2, gain_ref[...]) if final_norm else x2


def _combine(dest_tiles, gate_t, ys3, x1, hrow, wsg, wsu, wsd, mod3, gain, seq, tc, final_norm):
    t, d = x1.shape
    hid = wsg.shape[1]
    row = lambda i: (i, 0)
    const = lambda i: (0, 0)
    return pl.pallas_call(
        functools.partial(_combine_body, final_norm=final_norm),
        grid=(t // tc,),
        in_specs=[pl.BlockSpec(memory_space=pl.ANY),
                  pl.BlockSpec((tc, TOP_K), row),
                  pl.BlockSpec(memory_space=pl.ANY),
                  pl.BlockSpec((tc, d), row), pl.BlockSpec((tc, d), row),
                  pl.BlockSpec((d, hid), const), pl.BlockSpec((d, hid), const),
                  pl.BlockSpec((hid, d), const),
                  pl.BlockSpec((1, 6, d), lambda i: ((i * tc) // seq, 0, 0)),
                  pl.BlockSpec((1, d), const)],
        out_specs=pl.BlockSpec((tc, d), row),
        out_shape=jax.ShapeDtypeStruct((t, d), F32),
        scratch_shapes=[pltpu.SMEM((TOP_K * tc,), I32), pltpu.SMEM((TOP_K * tc,), I32),
                        pltpu.VMEM((2, TOP_K, tc * SLABS, LANES), F32),
                        pltpu.SemaphoreType.DMA, pltpu.SemaphoreType.DMA((2,))],
        compiler_params=_cparams(("arbitrary",), 56),
        name="combine",
    )(dest_tiles, gate_t, ys3, x1, hrow, wsg, wsu, wsd, mod3, gain)


def _tile_major(a, tile):
    k, t = a.shape
    return a.reshape(k, t // tile, tile).transpose(1, 0, 2).reshape(t // tile, k * tile)


def kernel(x, c, w_ada, b_ada, norm_mix, w_in, ret_decay, t5_bias, w_ret_up, w_att_up, w_o,
           norm_ffn, w_router, router_bias, w_gate, w_up, w_down, ws_gate, ws_up, ws_down, norm_final):
    bsz, seq, d = x.shape
    depth = w_ada.shape[0]
    t = bsz * seq
    assert d == D_MODEL and seq % (ATT_KWIN * DILATION_PATTERNS[-1][1]) == 0

    half = RET_HEAD_DIM // 2
    inv_freq = ROPE_BASE ** (-jnp.arange(half, dtype=F32) / half)
    ang = jnp.arange(seq, dtype=F32)[:, None] * inv_freq[None, :]
    cos, sin = jnp.cos(ang), jnp.sin(ang)
    bias_tab = _attention_bias(t5_bias)

    n_assign = t * TOP_K
    n_blocks = -(-n_assign // MOE_BLOCK) + N_EXPERTS
    n_rows = n_blocks * MOE_BLOCK
    disp_tile = min(1024, t)
    comb_tile = min(256, t)

    x2 = x.reshape(t, d)
    for layer in range(depth):
        mod3 = _adaln(c, w_ada[layer], b_ada[layer]).reshape(bsz, 6, d)
        proj = _inproj(x2, norm_mix[layer].reshape(1, d), mod3, w_in[layer].astype(BF16), seq)
        proj3 = proj.reshape(bsz, seq, PROJ_WIDTH)
        log_gamma = jnp.log1p(-jnp.exp(ret_decay[layer].astype(F32)))
        ret = _retention(proj3, log_gamma, cos, sin)
        att = _attention(proj3, bias_tab)
        x1, hslab, hrow, logits = _mix(
            ret.reshape(t, RET_WIDTH), att.reshape(t, ATT_WIDTH), proj, x2, mod3,
            norm_ffn[layer].reshape(1, d), w_ret_up[layer].astype(BF16),
            w_att_up[layer].astype(BF16), w_o[layer].astype(BF16), w_router[layer], seq)

        bias_b = jnp.broadcast_to(router_bias[layer].astype(F32)[:, None], (N_EXPERTS, LANES))
        eidx, gate, rank, counts = _route(logits, bias_b)

        counts = counts[:, 0].astype(I32)
        padded = (counts + MOE_BLOCK - 1) // MOE_BLOCK * MOE_BLOCK
        pad_end = jnp.cumsum(padded)
        pad_start = pad_end - padded
        block_expert = jnp.clip(
            jnp.searchsorted(pad_end, jnp.arange(n_blocks, dtype=I32) * MOE_BLOCK, side='right'),
            0, N_EXPERTS - 1).astype(I32)
        block_new = jnp.concatenate(
            [jnp.ones((1,), I32), (block_expert[1:] != block_expert[:-1]).astype(I32)])
        n_used = (pad_end[-1:] // MOE_BLOCK).astype(I32)

        dest = _dest(pad_start.astype(I32), eidx, rank)
        xs = _dispatch(_tile_major(dest, disp_tile), hslab, n_rows)
        ys = _experts(block_expert, block_new, n_used, xs.reshape(n_rows * SLABS, LANES),
                      w_gate[layer], w_up[layer], w_down[layer])
        x2 = _combine(_tile_major(dest, comb_tile), gate.T, ys.reshape(n_rows, SLABS, LANES),
                      x1, hrow, ws_gate[layer].astype(BF16), ws_up[layer].astype(BF16),
                      ws_down[layer].astype(BF16), mod3, norm_final.reshape(1, d), seq, comb_tile,
                      final_norm=(layer == depth - 1))
    return x2.reshape(bsz, seq, d)
```

```python
import functools
import math

import jax
import jax.numpy as jnp
import numpy as np
from jax import lax
from jax.experimental import pallas as pl
from jax.experimental.pallas import tpu as pltpu

F32 = jnp.float32
BF16 = jnp.bfloat16
I32 = jnp.int32

D_MODEL = 1024
RET_HEADS = 4
RET_HEAD_DIM = 256
RET_WIDTH = RET_HEADS * RET_HEAD_DIM
RET_CHUNK = 128
ROPE_BASE = 10000.0
ATT_HEADS = 16
ATT_HEAD_DIM = 64
ATT_WIDTH = ATT_HEADS * ATT_HEAD_DIM
DILATION_PATTERNS = ((128, 1), (512, 4), (2048, 16))
REL_BUCKETS = 32
REL_MAX_DISTANCE = 1024
N_EXPERTS = 256
TOP_K = 8
N_GROUPS = 8
TOPK_GROUPS = 4
EXPERT_HIDDEN = 256
ROUTED_SCALE = 2.5
MOE_BLOCK = 256
NORM_EPS = 1e-6
PROJ_WIDTH = 4 * RET_WIDTH + 3 * ATT_WIDTH + 2 * D_MODEL

LANES = 128
SUBLANES = 8
SLABS = D_MODEL // LANES
ATT_RADIUS = 64
ATT_QBLK = 128
ATT_KWIN = 256
COMBINE_ROWS = 32
ATT_UNROLL = 4
NEG_BIG = -1e30
MIB = 1024 * 1024


def _cparams(sem, vmem_mib):
    return pltpu.CompilerParams(dimension_semantics=sem, vmem_limit_bytes=vmem_mib * MIB)


def _sigmoid(x):
    return 1.0 / (1.0 + jnp.exp(-x))


def _silu(x):
    return x * _sigmoid(x)


def _rms(x, gain):
    return x * lax.rsqrt(jnp.mean(x * x, axis=-1, keepdims=True) + NORM_EPS) * gain


def _adaln_body(c_ref, w_ref, b_ref, o_ref):
    cond = _silu(c_ref[...])
    o_ref[...] = jnp.dot(cond, w_ref[...], preferred_element_type=F32,
                         precision=lax.Precision.HIGHEST) + b_ref[...]


def _adaln(c, w, b):
    bsz, d = c.shape
    n = w.shape[1]
    return pl.pallas_call(
        _adaln_body,
        grid=(n // d,),
        in_specs=[pl.BlockSpec((bsz, d), lambda j: (0, 0)),
                  pl.BlockSpec((d, d), lambda j: (0, j)),
                  pl.BlockSpec((1, d), lambda j: (0, j))],
        out_specs=pl.BlockSpec((bsz, d), lambda j: (0, j)),
        out_shape=jax.ShapeDtypeStruct((bsz, n), F32),
        compiler_params=_cparams(("arbitrary",), 32),
        name="adaln",
    )(c, w, b.reshape(1, n))


def _inproj_body(x_ref, gain_ref, mod_ref, w_ref, o_ref, h_scr):
    @pl.when(pl.program_id(1) == 0)
    def _():
        y = _rms(x_ref[...], gain_ref[...])
        h = y * (1.0 + mod_ref[0, 1:2, :]) + mod_ref[0, 0:1, :]
        h_scr[...] = h.astype(BF16)

    o_ref[...] = jnp.dot(h_scr[...], w_ref[...], preferred_element_type=F32).astype(BF16)


def _inproj(x2, gain, mod3, w_bf, seq, tm=1024, tn=1536):
    t, d = x2.shape
    n = w_bf.shape[1]
    return pl.pallas_call(
        _inproj_body,
        grid=(t // tm, n // tn),
        in_specs=[pl.BlockSpec((tm, d), lambda i, j: (i, 0)),
                  pl.BlockSpec((1, d), lambda i, j: (0, 0)),
                  pl.BlockSpec((1, 6, d), lambda i, j: ((i * tm) // seq, 0, 0)),
                  pl.BlockSpec((d, tn), lambda i, j: (0, j))],
        out_specs=pl.BlockSpec((tm, tn), lambda i, j: (i, j)),
        out_shape=jax.ShapeDtypeStruct((t, n), BF16),
        scratch_shapes=[pltpu.VMEM((tm, d), BF16)],
        compiler_params=_cparams(("arbitrary", "arbitrary"), 48),
        name="inproj",
    )(x2, gain, mod3, w_bf)


def _ret_body(lg_ref, q_ref, k_ref, v_ref, g_ref, cos_ref, sin_ref, o_ref,
              qr_scr, kr_scr, accf_scr, accb_scr, stf_scr, stb_scr):
    head = pl.program_id(1)
    seq = q_ref.shape[0]
    n_chunks = seq // RET_CHUNK
    half = RET_HEAD_DIM // 2
    rot_rows = 256

    def rot_step(c, carry):
        r = pl.ds(pl.multiple_of(c * rot_rows, rot_rows), rot_rows)
        cs = cos_ref[r, :]
        sn = sin_ref[r, :]
        for src, dst, scale in ((q_ref, qr_scr, 1.0), (k_ref, kr_scr, RET_HEAD_DIM ** -0.5)):
            t = src[r, :].astype(F32)
            t1, t2 = t[:, :half], t[:, half:]
            dst[r, :half] = ((t1 * cs - t2 * sn) * scale).astype(BF16)
            dst[r, half:] = ((t1 * sn + t2 * cs) * scale).astype(BF16)
        return carry

    lax.fori_loop(0, seq // rot_rows, rot_step, 0)

    ri = lax.broadcasted_iota(I32, (RET_CHUNK, RET_CHUNK), 0)
    ci = lax.broadcasted_iota(I32, (RET_CHUNK, RET_CHUNK), 1)
    rowpos = lax.broadcasted_iota(I32, (RET_CHUNK, RET_HEAD_DIM), 0).astype(F32)

    def decay_tables(lg, forward):
        if forward:
            diff = (ri - ci).astype(F32)
            allowed = ri >= ci
            q_scale = jnp.exp(lg * (rowpos + 1.0))
            k_scale = jnp.exp(lg * (RET_CHUNK - 1.0 - rowpos))
        else:
            diff = (ci - ri).astype(F32)
            allowed = ci > ri
            q_scale = jnp.exp(lg * (RET_CHUNK - rowpos))
            k_scale = jnp.exp(lg * rowpos)
        intra = jnp.where(allowed, jnp.exp(lg * jnp.where(allowed, diff, 0.0)), 0.0)
        chunk_decay = jnp.exp(lg * jnp.full((1, RET_HEAD_DIM), float(RET_CHUNK), F32))
        return intra, q_scale, k_scale, chunk_decay

    def chunk_update(c, tables, st_scr, acc_scr):
        intra, q_scale, k_scale, chunk_decay = tables
        r = pl.ds(pl.multiple_of(c * RET_CHUNK, RET_CHUNK), RET_CHUNK)
        q = qr_scr[r, :]
        k = kr_scr[r, :]
        v = v_ref[r, :]
        scores = lax.dot_general(q, k, (((1,), (1,)), ((), ())),
                                 preferred_element_type=F32) * intra
        state = st_scr[...]
        acc_scr[r, :] = (jnp.dot(scores.astype(BF16), v, preferred_element_type=F32)
                         + jnp.dot((q.astype(F32) * q_scale).astype(BF16), state.astype(BF16),
                                   preferred_element_type=F32))
        k_t = (k.astype(F32) * k_scale).T.astype(BF16)
        st_scr[...] = state * chunk_decay + jnp.dot(k_t, v, preferred_element_type=F32)

    fwd_tables = decay_tables(lg_ref[0, head], True)
    bwd_tables = decay_tables(lg_ref[1, head], False)
    stf_scr[...] = jnp.zeros_like(stf_scr)
    stb_scr[...] = jnp.zeros_like(stb_scr)

    def step(i, carry):
        chunk_update(i, fwd_tables, stf_scr, accf_scr)
        chunk_update(n_chunks - 1 - i, bwd_tables, stb_scr, accb_scr)
        return carry

    lax.fori_loop(0, n_chunks, step, 0)

    def finish(c, carry):
        r = pl.ds(pl.multiple_of(c * rot_rows, rot_rows), rot_rows)
        o = accf_scr[r, :] + accb_scr[r, :]
        mu = jnp.mean(o, axis=-1, keepdims=True)
        oc = o - mu
        var = jnp.mean(oc * oc, axis=-1, keepdims=True)
        o = oc * lax.rsqrt(var + NORM_EPS)
        o_ref[r, :] = (o * _silu(g_ref[r, :].astype(F32))).astype(BF16)
        return carry

    lax.fori_loop(0, seq // rot_rows, finish, 0)


def _retention(proj3, log_gamma, cos, sin):
    bsz, seq, _ = proj3.shape
    hd = RET_HEAD_DIM

    def col(section):
        return pl.BlockSpec((None, seq, hd), lambda b, h, lg: (b, 0, section * RET_HEADS + h))

    return pl.pallas_call(
        _ret_body,
        grid_spec=pltpu.PrefetchScalarGridSpec(
            num_scalar_prefetch=1,
            grid=(bsz, RET_HEADS),
            in_specs=[col(0), col(1), col(2), col(3),
                      pl.BlockSpec((seq, hd // 2), lambda b, h, lg: (0, 0)),
                      pl.BlockSpec((seq, hd // 2), lambda b, h, lg: (0, 0))],
            out_specs=pl.BlockSpec((None, seq, hd), lambda b, h, lg: (b, 0, h)),
            scratch_shapes=[pltpu.VMEM((seq, hd), BF16), pltpu.VMEM((seq, hd), BF16),
                            pltpu.VMEM((seq, hd), F32), pltpu.VMEM((seq, hd), F32),
                            pltpu.VMEM((hd, hd), F32), pltpu.VMEM((hd, hd), F32)]),
        out_shape=jax.ShapeDtypeStruct((bsz, seq, RET_WIDTH), BF16),
        compiler_params=_cparams(("arbitrary", "arbitrary"), 56),
        name="retention",
    )(log_gamma, proj3, proj3, proj3, proj3, cos, sin)


def _t5_bucket(rel):
    half = REL_BUCKETS // 2
    max_exact = half // 2
    n = jnp.abs(rel)
    large = max_exact + (jnp.log(jnp.maximum(n, 1).astype(F32) / max_exact)
                         / math.log(REL_MAX_DISTANCE / max_exact) * (half - max_exact)).astype(I32)
    large = jnp.minimum(large, half - 1)
    return jnp.where(rel > 0, half, 0) + jnp.where(n < max_exact, n, large)


def _band_buckets():
    qi = jnp.arange(ATT_QBLK, dtype=I32)[:, None]
    kj = jnp.arange(ATT_KWIN, dtype=I32)[None, :]
    tables = []
    for _, dilation in DILATION_PATTERNS:
        cases = []
        for offset in (0, -ATT_RADIUS, ATT_QBLK - ATT_KWIN):
            rel = kj + offset - qi
            cases.append(jnp.where(jnp.abs(rel) <= ATT_RADIUS, _t5_bucket(rel * dilation), -1))
        tables.append(jnp.stack(cases, axis=0))
    return jnp.stack(tables, axis=0)


def _bias_body(t5_ref, bucket_ref, o_ref):
    bucket = bucket_ref[...]

    def head(h, carry):
        acc = jnp.full(bucket.shape, NEG_BIG, F32)
        for b in range(REL_BUCKETS):
            acc = jnp.where(bucket == b, t5_ref[b, h], acc)
        o_ref[h] = acc
        return carry

    lax.fori_loop(0, ATT_HEADS, head, 0)


def _attention_bias(t5_bias):
    buckets = _band_buckets()
    n_pat, n_case = buckets.shape[:2]
    return pl.pallas_call(
        _bias_body,
        grid_spec=pltpu.PrefetchScalarGridSpec(
            num_scalar_prefetch=1,
            grid=(n_pat, n_case),
            in_specs=[pl.BlockSpec((None, None, ATT_QBLK, ATT_KWIN), lambda p, c, t5: (p, c, 0, 0))],
            out_specs=pl.BlockSpec((None, ATT_HEADS, None, ATT_QBLK, ATT_KWIN),
                                   lambda p, c, t5: (p, 0, c, 0, 0))),
        out_shape=jax.ShapeDtypeStruct((n_pat, ATT_HEADS, n_case, ATT_QBLK, ATT_KWIN), F32),
        compiler_params=_cparams(("arbitrary", "arbitrary"), 32),
        name="attn_bias",
    )(t5_bias.astype(F32), buckets)


def _attn_body(q_ref, k_ref, v_ref, bias_ref, o_ref,
               qf, kf, vf, qd, kd, vd, od, ld, og, lgs):
    seq = q_ref.shape[0]
    rows = 256
    lane = lax.broadcasted_iota(I32, (ATT_QBLK, LANES), 1)
    head0 = lane < ATT_HEAD_DIM

    def to_f32(c, carry):
        r = pl.ds(pl.multiple_of(c * rows, rows), rows)
        qf[r, :] = q_ref[r, :].astype(F32)
        kf[r, :] = k_ref[r, :].astype(F32)
        vf[r, :] = v_ref[r, :].astype(F32)
        return carry

    lax.fori_loop(0, seq // rows, to_f32, 0)

    def band_blocks(g, length, n_seg, q_src, k_src, v_src, o_dst, l_dst):
        n_qb = length // ATT_QBLK
        n_blocks = n_seg * n_qb
        unroll = min(ATT_UNROLL, n_blocks)

        def qgroup(it, carry):
            for u in range(unroll):
                qblock(it * unroll + u)
            return carry

        def qblock(b):
            qi = b % n_qb
            base = (b // n_qb) * length
            qs = pl.multiple_of(base + qi * ATT_QBLK, ATT_QBLK)
            ws = pl.multiple_of(
                base + jnp.clip(qi * ATT_QBLK - ATT_RADIUS, 0, length - ATT_KWIN), ATT_RADIUS)
            case = jnp.where(qi == 0, 0, jnp.where(qi == n_qb - 1, 2, 1))
            q = q_src[pl.ds(qs, ATT_QBLK), :]
            k = k_src[pl.ds(ws, ATT_KWIN), :]
            v = v_src[pl.ds(ws, ATT_KWIN), :]
            outs, lses = [], []
            for hh in range(2):
                mask = head0 if hh == 0 else jnp.logical_not(head0)
                qm = jnp.where(mask, q, jnp.zeros_like(q))
                s = lax.dot_general(qm, k, (((1,), (1,)), ((), ())), preferred_element_type=F32)
                s = s * (ATT_HEAD_DIM ** -0.5) + bias_ref[g, hh, case]
                m = jnp.max(s, axis=-1, keepdims=True)
                p = jnp.exp(s - m)
                l = jnp.sum(p, axis=-1, keepdims=True)
                outs.append(jnp.dot(p.astype(BF16), v, preferred_element_type=F32) / l)
                lses.append(m + jnp.log(l))
            o_dst[pl.ds(qs, ATT_QBLK), :] = jnp.where(head0, outs[0], outs[1])
            l_dst[pl.ds(qs, ATT_QBLK), :] = jnp.where(head0, lses[0], lses[1])

        lax.fori_loop(0, n_blocks // unroll, qgroup, 0)

    for g, (_, dilation) in enumerate(DILATION_PATTERNS):
        length = seq // dilation
        if dilation == 1:
            band_blocks(g, length, 1, q_ref, k_ref, v_ref, og.at[g], lgs.at[g])
            continue
        n_seg = max(1, min(dilation, ATT_UNROLL * ATT_QBLK // length))

        def residues(it, carry, g=g, dilation=dilation, length=length, n_seg=n_seg):
            for j in range(n_seg):
                strided = pl.ds(it * n_seg + j, length, stride=dilation)
                dense = pl.ds(j * length, length)
                qd[dense, :] = qf[strided, :].astype(BF16)
                kd[dense, :] = kf[strided, :].astype(BF16)
                vd[dense, :] = vf[strided, :].astype(BF16)
            band_blocks(g, length, n_seg, qd, kd, vd, od, ld)
            for j in range(n_seg):
                strided = pl.ds(it * n_seg + j, length, stride=dilation)
                dense = pl.ds(j * length, length)
                og[g, strided, :] = od[dense, :]
                lgs[g, strided, :] = ld[dense, :]
            return carry

        lax.fori_loop(0, dilation // n_seg, residues, 0)

    def merge(c, carry):
        r = pl.ds(pl.multiple_of(c * rows, rows), rows)
        l0, l1, l2 = lgs[0, r, :], lgs[1, r, :], lgs[2, r, :]
        m = jnp.maximum(jnp.maximum(l0, l1), l2)
        w0, w1, w2 = jnp.exp(l0 - m), jnp.exp(l1 - m), jnp.exp(l2 - m)
        num = w0 * og[0, r, :] + w1 * og[1, r, :] + w2 * og[2, r, :]
        o_ref[r, :] = (num / (w0 + w1 + w2)).astype(BF16)
        return carry

    lax.fori_loop(0, seq // rows, merge, 0)


def _attention(proj3, bias_tab):
    bsz, seq, _ = proj3.shape
    n_pat = len(DILATION_PATTERNS)
    pairs = ATT_HEADS // 2
    base = 4 * RET_WIDTH // LANES

    def col(section):
        return pl.BlockSpec((None, seq, LANES),
                            lambda b, hp: (b, 0, base + section * (ATT_WIDTH // LANES) + hp))

    return pl.pallas_call(
        _attn_body,
        grid=(bsz, pairs),
        in_specs=[col(0), col(1), col(2),
                  pl.BlockSpec((n_pat, 2, 3, ATT_QBLK, ATT_KWIN), lambda b, hp: (0, hp, 0, 0, 0))],
        out_specs=pl.BlockSpec((None, seq, LANES), lambda b, hp: (b, 0, hp)),
        out_shape=jax.ShapeDtypeStruct((bsz, seq, ATT_WIDTH), BF16),
        scratch_shapes=[pltpu.VMEM((seq, LANES), F32)] * 3
                       + [pltpu.VMEM((seq // 4, LANES), BF16)] * 3
                       + [pltpu.VMEM((seq // 4, LANES), F32)] * 2
                       + [pltpu.VMEM((n_pat, seq, LANES), F32)] * 2,
        compiler_params=_cparams(("arbitrary", "arbitrary"), 56),
        name="attention",
    )(proj3, proj3, proj3, bias_tab)


def _mix_body(ret_ref, att_ref, gr_ref, ga_ref, x_ref, mod_ref, gain_ref,
              wr_ref, wa_ref, wo_ref, wrt_ref, x1_ref, hslab_ref, hrow_ref, logit_ref):
    tm = x_ref.shape[0]
    y_ret = jnp.dot(ret_ref[...], wr_ref[...], preferred_element_type=F32)
    y_att = jnp.dot(att_ref[...], wa_ref[...], preferred_element_type=F32)
    merged = (_sigmoid(gr_ref[...].astype(F32)) * y_ret
              + _sigmoid(ga_ref[...].astype(F32)) * y_att)
    mixed = jnp.dot(merged.astype(BF16), wo_ref[...], preferred_element_type=F32)
    x1 = x_ref[...] + mod_ref[0, 2:3, :] * mixed
    x1_ref[...] = x1
    h = _rms(x1, gain_ref[...]) * (1.0 + mod_ref[0, 4:5, :]) + mod_ref[0, 3:4, :]
    hrow_ref[...] = h.astype(BF16)
    for s in range(SLABS):
        hslab_ref[pl.ds(s, tm, stride=SLABS), :] = h[:, s * LANES:(s + 1) * LANES]
    logit_ref[...] = jnp.dot(h, wrt_ref[...], preferred_element_type=F32,
                             precision=lax.Precision.HIGHEST)


def _mix(ret2, att2, proj2, x2, mod3, gain, wr, wa, wo, w_router, seq, tm=512):
    t, d = x2.shape
    gate_base = (4 * RET_WIDTH + 3 * ATT_WIDTH) // d
    row = lambda i: (i, 0)
    const = lambda i: (0, 0)
    return pl.pallas_call(
        _mix_body,
        grid=(t // tm,),
        in_specs=[pl.BlockSpec((tm, d), row), pl.BlockSpec((tm, d), row),
                  pl.BlockSpec((tm, d), lambda i: (i, gate_base)),
                  pl.BlockSpec((tm, d), lambda i: (i, gate_base + 1)),
                  pl.BlockSpec((tm, d), row),
                  pl.BlockSpec((1, 6, d), lambda i: ((i * tm) // seq, 0, 0)),
                  pl.BlockSpec((1, d), const),
                  pl.BlockSpec((d, d), const), pl.BlockSpec((d, d), const),
                  pl.BlockSpec((d, d), const), pl.BlockSpec((d, N_EXPERTS), const)],
        out_specs=[pl.BlockSpec((tm, d), row),
                   pl.BlockSpec((tm * SLABS, LANES), row),
                   pl.BlockSpec((tm, d), row),
                   pl.BlockSpec((tm, N_EXPERTS), row)],
        out_shape=[jax.ShapeDtypeStruct((t, d), F32),
                   jax.ShapeDtypeStruct((t * SLABS, LANES), F32),
                   jax.ShapeDtypeStruct((t, d), BF16),
                   jax.ShapeDtypeStruct((t, N_EXPERTS), F32)],
        compiler_params=_cparams(("arbitrary",), 56),
        name="mix",
    )(ret2, att2, proj2, proj2, x2, mod3, gain, wr, wa, wo, w_router)


def _route_body(logit_ref, bias_ref, eidx_ref, gate_ref, rank_ref, cnt_ref, carry_scr, tri_scr):
    tn = logit_ref.shape[0]
    per_group = N_EXPERTS // N_GROUPS

    @pl.when(pl.program_id(0) == 0)
    def _():
        carry_scr[...] = jnp.zeros_like(carry_scr)
        r = lax.broadcasted_iota(I32, (tn, tn), 0)
        c = lax.broadcasted_iota(I32, (tn, tn), 1)
        tri_scr[...] = jnp.where(r < c, 1.0, 0.0).astype(BF16)

    scores = _sigmoid(logit_ref[...]).T
    choice = scores + jnp.concatenate([bias_ref[...]] * (tn // LANES), axis=1)
    neg_inf = -jnp.inf

    sub = lax.broadcasted_iota(I32, (per_group, tn), 0).astype(F32)
    group_score = []
    for g in range(N_GROUPS):
        cg = choice[g * per_group:(g + 1) * per_group, :]
        m1 = jnp.max(cg, axis=0, keepdims=True)
        first = jnp.min(jnp.where(cg == m1, sub, float(per_group)), axis=0, keepdims=True)
        m2 = jnp.max(jnp.where(sub == first, neg_inf, cg), axis=0, keepdims=True)
        group_score.append(m1 + m2)

    masked = []
    for a in range(N_GROUPS):
        beaten = jnp.zeros((1, tn), F32)
        for b in range(N_GROUPS):
            if b == a:
                continue
            wins = (group_score[b] >= group_score[a]) if b < a else (group_score[b] > group_score[a])
            beaten = beaten + jnp.where(wins, 1.0, 0.0)
        keep = beaten < float(TOPK_GROUPS)
        masked.append(jnp.where(keep, choice[a * per_group:(a + 1) * per_group, :], neg_inf))
    work = jnp.concatenate(masked, axis=0)

    eid = lax.broadcasted_iota(I32, (N_EXPERTS, tn), 0).astype(F32)
    picked = jnp.zeros((N_EXPERTS, tn), F32)
    idx_rows, gate_rows = [], []
    for _ in range(TOP_K):
        m = jnp.max(work, axis=0, keepdims=True)
        idx = jnp.min(jnp.where(work == m, eid, float(N_EXPERTS)), axis=0, keepdims=True)
        sel = eid == idx
        gate_rows.append(jnp.sum(jnp.where(sel, scores, 0.0), axis=0, keepdims=True))
        picked = picked + jnp.where(sel, 1.0, 0.0)
        work = jnp.where(sel, neg_inf, work)
        idx_rows.append(idx)

    before = (jnp.dot(picked.astype(BF16), tri_scr[...], preferred_element_type=F32)
              + carry_scr[:, 0:1])
    rank_rows = [jnp.sum(jnp.where(eid == idx, before, 0.0), axis=0, keepdims=True)
                 for idx in idx_rows]
    carry = carry_scr[...] + jnp.sum(picked, axis=1, keepdims=True)
    carry_scr[...] = carry
    cnt_ref[...] = carry

    gates = jnp.concatenate(gate_rows, axis=0)
    gates = gates / jnp.sum(gates, axis=0, keepdims=True) * ROUTED_SCALE
    eidx_ref[...] = jnp.concatenate(idx_rows, axis=0).astype(I32)
    gate_ref[...] = gates
    rank_ref[...] = jnp.concatenate(rank_rows, axis=0).astype(I32)


def _route(logits, bias_b, tn=512):
    t = logits.shape[0]
    tok = lambda i: (0, i)
    return pl.pallas_call(
        _route_body,
        grid=(t // tn,),
        in_specs=[pl.BlockSpec((tn, N_EXPERTS), lambda i: (i, 0)),
                  pl.BlockSpec((N_EXPERTS, LANES), lambda i: (0, 0))],
        out_specs=[pl.BlockSpec((TOP_K, tn), tok), pl.BlockSpec((TOP_K, tn), tok),
                   pl.BlockSpec((TOP_K, tn), tok),
                   pl.BlockSpec((N_EXPERTS, LANES), lambda i: (0, 0))],
        out_shape=[jax.ShapeDtypeStruct((TOP_K, t), I32), jax.ShapeDtypeStruct((TOP_K, t), F32),
                   jax.ShapeDtypeStruct((TOP_K, t), I32),
                   jax.ShapeDtypeStruct((N_EXPERTS, LANES), F32)],
        scratch_shapes=[pltpu.VMEM((N_EXPERTS, LANES), F32), pltpu.VMEM((tn, tn), BF16)],
        compiler_params=_cparams(("arbitrary",), 48),
        name="route",
    )(logits, bias_b)


def _dest_body(start_ref, eidx_ref, rank_ref, dest_ref):
    e = eidx_ref[...]

    def body(j, acc):
        return jnp.where(e == j, start_ref[j], acc)

    row = rank_ref[...] + lax.fori_loop(0, N_EXPERTS, body, jnp.zeros_like(e))
    dest_ref[...] = row * SLABS


def _dest(pad_start, eidx, rank, tn=2048):
    t = eidx.shape[1]
    tok = lambda i, s: (0, i)
    return pl.pallas_call(
        _dest_body,
        grid_spec=pltpu.PrefetchScalarGridSpec(
            num_scalar_prefetch=1,
            grid=(t // tn,),
            in_specs=[pl.BlockSpec((TOP_K, tn), tok), pl.BlockSpec((TOP_K, tn), tok)],
            out_specs=pl.BlockSpec((TOP_K, tn), tok)),
        out_shape=jax.ShapeDtypeStruct((TOP_K, t), I32),
        compiler_params=_cparams(("arbitrary",), 32),
        name="dest",
    )(pad_start, eidx, rank)


def _dispatch_body(dest_hbm, h_ref, xs_hbm, dest_smem, idx_sem, row_sem):
    tt = dest_smem.shape[0] // TOP_K
    idx_copy = pltpu.make_async_copy(dest_hbm.at[pl.program_id(0)], dest_smem, idx_sem)
    idx_copy.start()
    idx_copy.wait()

    def issue(t, carry):
        src = h_ref.at[pl.ds(pl.multiple_of(t * SLABS, SLABS), SLABS)]
        for k in range(TOP_K):
            dst = xs_hbm.at[pl.ds(pl.multiple_of(dest_smem[k * tt + t], SLABS), SLABS)]
            pltpu.make_async_copy(src, dst, row_sem).start(priority=k % 2)
        return carry

    lax.fori_loop(0, tt, issue, 0)

    for k in range(TOP_K):
        pltpu.make_async_copy(h_ref, xs_hbm.at[pl.ds(0, tt * SLABS)], row_sem).wait()


def _dispatch(dest_tiles, hslab2, n_rows):
    n_tiles, width = dest_tiles.shape
    tt = width // TOP_K
    return pl.pallas_call(
        _dispatch_body,
        grid=(n_tiles,),
        in_specs=[pl.BlockSpec(memory_space=pl.ANY),
                  pl.BlockSpec((tt * SLABS, LANES), lambda i: (i, 0))],
        out_specs=pl.BlockSpec(memory_space=pl.ANY),
        out_shape=jax.ShapeDtypeStruct((n_rows * SLABS, LANES), F32),
        scratch_shapes=[pltpu.SMEM((width,), I32), pltpu.SemaphoreType.DMA, pltpu.SemaphoreType.DMA],
        compiler_params=_cparams(("arbitrary",), 32),
        name="dispatch",
    )(dest_tiles, hslab2)


def _experts_body(bexp_ref, bnew_ref, nused_ref, xs_ref, wg_ref, wu_ref, wd_ref, ys_ref,
                  wgu_s, wd_s):
    i = pl.program_id(0)
    hid = wd_s.shape[0]

    @pl.when(i < nused_ref[0])
    def _():
        @pl.when(bnew_ref[i] == 1)
        def _():
            wgu_s[:, :hid] = wg_ref[...].astype(BF16)
            wgu_s[:, hid:] = wu_ref[...].astype(BF16)
            wd_s[...] = wd_ref[...].astype(BF16)

        x = jnp.concatenate([xs_ref[pl.ds(s, MOE_BLOCK, stride=SLABS), :] for s in range(SLABS)],
                            axis=1).astype(BF16)
        gu = jnp.dot(x, wgu_s[...], preferred_element_type=F32)
        hg, hu = gu[:, :hid], gu[:, hid:]
        y = jnp.dot((_silu(hg) * hu).astype(BF16), wd_s[...], preferred_element_type=F32)
        for s in range(SLABS):
            ys_ref[pl.ds(s, MOE_BLOCK, stride=SLABS), :] = y[:, s * LANES:(s + 1) * LANES]


def _experts(block_expert, block_new, n_used, xs2, w_gate, w_up, w_down):
    n_blocks = block_expert.shape[0]
    d, hid = w_gate.shape[1], w_gate.shape[2]
    rows = MOE_BLOCK * SLABS

    def blk(i, be, bn, nu):
        return (jnp.minimum(i, nu[0] - 1), 0)

    def wsel(i, be, bn, nu):
        return (be[jnp.minimum(i, nu[0] - 1)], 0, 0)

    return pl.pallas_call(
        _experts_body,
        grid_spec=pltpu.PrefetchScalarGridSpec(
            num_scalar_prefetch=3,
            grid=(n_blocks,),
            in_specs=[pl.BlockSpec((rows, LANES), blk),
                      pl.BlockSpec((None, d, hid), wsel),
                      pl.BlockSpec((None, d, hid), wsel),
                      pl.BlockSpec((None, hid, d), wsel)],
            out_specs=pl.BlockSpec((rows, LANES), blk),
            scratch_shapes=[pltpu.VMEM((d, 2 * hid), BF16), pltpu.VMEM((hid, d), BF16)]),
        out_shape=jax.ShapeDtypeStruct(xs2.shape, F32),
        compiler_params=_cparams(("arbitrary",), 32),
        name="experts",
    )(block_expert, block_new, n_used, xs2, w_gate, w_up, w_down)


def _combine_body(dest_hbm, gate_ref, ys_hbm, x1_ref, h_ref, wsg_ref, wsu_ref, wsd_ref,
                  mod_ref, gain_ref, o_ref, dest_smem0, dest_smem1, buf, shared_scr, idx_sem, row_sem,
                  *, final_norm):
    tc = x1_ref.shape[0]
    i = pl.program_id(0)
    has_next = i + 1 < pl.num_programs(0)
    cur = i % 2
    dest_smem = (dest_smem0, dest_smem1)

    def index_copy(tile, slot):
        return pltpu.make_async_copy(dest_hbm.at[tile], dest_smem[slot], idx_sem)

    def start_rows(slot):
        def issue(t, carry):
            for k in range(TOP_K):
                src = ys_hbm.at[pl.ds(pl.multiple_of(dest_smem[slot][k * tc + t], SLABS), SLABS)]
                dst = buf.at[slot, k, pl.ds(pl.multiple_of(t * SLABS, SLABS), SLABS)]
                pltpu.make_async_copy(src, dst, row_sem.at[slot]).start(priority=k % 2)
            return carry

        lax.fori_loop(0, tc, issue, 0)

    @pl.when(i == 0)
    def _():
        index_copy(0, 0).start()
        index_copy(0, 0).wait()
        start_rows(0)

    for slot in range(2):
        @pl.when(jnp.logical_and(has_next, cur != slot))
        def _(slot=slot):
            index_copy(i + 1, slot).start()

    h = h_ref[...]
    hid = (_silu(jnp.dot(h, wsg_ref[...], preferred_element_type=F32))
           * jnp.dot(h, wsu_ref[...], preferred_element_type=F32))
    shared_scr[...] = jnp.dot(hid.astype(BF16), wsd_ref[...], preferred_element_type=F32)

    for slot in range(2):
        @pl.when(jnp.logical_and(has_next, cur != slot))
        def _(slot=slot):
            index_copy(i + 1, slot).wait()
            start_rows(slot)

    for k in range(TOP_K):
        pltpu.make_async_copy(ys_hbm.at[pl.ds(0, tc * SLABS)], buf.at[cur, k],
                              row_sem.at[cur]).wait()

    gate_f = mod_ref[0, 5:6, :]
    gain = gain_ref[...]

    def finish(c, carry):
        r0 = pl.multiple_of(c * COMBINE_ROWS, COMBINE_ROWS)
        rows = pl.ds(r0, COMBINE_ROWS)
        y = shared_scr[rows, :]
        gates = gate_ref[rows, :]
        for k in range(TOP_K):
            picked = jnp.concatenate(
                [buf[cur, k, pl.ds(r0 * SLABS + s, COMBINE_ROWS, stride=SLABS), :]
                 for s in range(SLABS)], axis=1)
            y = y + gates[:, k:k + 1] * picked
        x2 = x1_ref[rows, :] + gate_f * y
        o_ref[rows, :] = _rms(x2, gain) if final_norm else x2
        return carry

    lax.fori_loop(0, tc // COMBINE_ROWS, finish, 0)


def _combine(dest_tiles, gate_t, ys2, x1, hrow, wsg, wsu, wsd, mod3, gain, seq, tc, final_norm):
    t, d = x1.shape
    hid = wsg.shape[1]
    row = lambda i: (i, 0)
    const = lambda i: (0, 0)
    return pl.pallas_call(
        functools.partial(_combine_body, final_norm=final_norm),
        grid=(t // tc,),
        in_specs=[pl.BlockSpec(memory_space=pl.ANY),
                  pl.BlockSpec((tc, TOP_K), row),
                  pl.BlockSpec(memory_space=pl.ANY),
                  pl.BlockSpec((tc, d), row), pl.BlockSpec((tc, d), row),
                  pl.BlockSpec((d, hid), const), pl.BlockSpec((d, hid), const),
                  pl.BlockSpec((hid, d), const),
                  pl.BlockSpec((1, 6, d), lambda i: ((i * tc) // seq, 0, 0)),
                  pl.BlockSpec((1, d), const)],
        out_specs=pl.BlockSpec((tc, d), row),
        out_shape=jax.ShapeDtypeStruct((t, d), F32),
        scratch_shapes=[pltpu.SMEM((TOP_K * tc,), I32), pltpu.SMEM((TOP_K * tc,), I32),
                        pltpu.VMEM((2, TOP_K, tc * SLABS, LANES), F32),
                        pltpu.VMEM((tc, d), F32),
                        pltpu.SemaphoreType.DMA, pltpu.SemaphoreType.DMA((2,))],
        compiler_params=_cparams(("arbitrary",), 56),
        name="combine",
    )(dest_tiles, gate_t, ys2, x1, hrow, wsg, wsu, wsd, mod3, gain)


def _tile_major(a, tile):
    k, t = a.shape
    return a.reshape(k, t // tile, tile).transpose(1, 0, 2).reshape(t // tile, k * tile)


def kernel(x, c, w_ada, b_ada, norm_mix, w_in, ret_decay, t5_bias, w_ret_up, w_att_up, w_o,
           norm_ffn, w_router, router_bias, w_gate, w_up, w_down, ws_gate, ws_up, ws_down, norm_final):
    bsz, seq, d = x.shape
    depth = w_ada.shape[0]
    t = bsz * seq
    assert d == D_MODEL and seq % (ATT_KWIN * DILATION_PATTERNS[-1][1]) == 0

    half = RET_HEAD_DIM // 2
    inv_freq = ROPE_BASE ** (-jnp.arange(half, dtype=F32) / half)
    ang = jnp.arange(seq, dtype=F32)[:, None] * inv_freq[None, :]
    cos, sin = jnp.cos(ang), jnp.sin(ang)
    bias_tab = _attention_bias(t5_bias)

    n_assign = t * TOP_K
    n_blocks = -(-n_assign // MOE_BLOCK) + N_EXPERTS
    n_rows = n_blocks * MOE_BLOCK
    disp_tile = min(1024, t)
    comb_tile = min(256, t)

    x2 = x.reshape(t, d)
    for layer in range(depth):
        mod3 = _adaln(c, w_ada[layer], b_ada[layer]).reshape(bsz, 6, d)
        proj = _inproj(x2, norm_mix[layer].reshape(1, d), mod3, w_in[layer].astype(BF16), seq)
        proj3 = proj.reshape(bsz, seq, PROJ_WIDTH)
        log_gamma = jnp.log1p(-jnp.exp(ret_decay[layer].astype(F32)))
        ret = _retention(proj3, log_gamma, cos, sin)
        att = _attention(proj3, bias_tab)
        x1, hslab, hrow, logits = _mix(
            ret.reshape(t, RET_WIDTH), att.reshape(t, ATT_WIDTH), proj, x2, mod3,
            norm_ffn[layer].reshape(1, d), w_ret_up[layer].astype(BF16),
            w_att_up[layer].astype(BF16), w_o[layer].astype(BF16), w_router[layer], seq)

        bias_b = jnp.broadcast_to(router_bias[layer].astype(F32)[:, None], (N_EXPERTS, LANES))
        eidx, gate, rank, counts = _route(logits, bias_b)

        counts = counts[:, 0].astype(I32)
        padded = (counts + MOE_BLOCK - 1) // MOE_BLOCK * MOE_BLOCK
        pad_end = jnp.cumsum(padded)
        pad_start = pad_end - padded
        block_row = jnp.arange(n_blocks, dtype=I32) * MOE_BLOCK
        block_expert = jnp.minimum(
            jnp.sum((pad_end[None, :] <= block_row[:, None]).astype(I32), axis=1), N_EXPERTS - 1)
        block_new = jnp.concatenate(
            [jnp.ones((1,), I32), (block_expert[1:] != block_expert[:-1]).astype(I32)])
        n_used = (pad_end[-1:] // MOE_BLOCK).astype(I32)

        dest = _dest(pad_start.astype(I32), eidx, rank)
        xs = _dispatch(_tile_major(dest, disp_tile), hslab, n_rows)
        ys = _experts(block_expert, block_new, n_used, xs,
                      w_gate[layer], w_up[layer], w_down[layer])
        x2 = _combine(_tile_major(dest, comb_tile), gate.T, ys,
                      x1, hrow, ws_gate[layer].astype(BF16), ws_up[layer].astype(BF16),
                      ws_down[layer].astype(BF16), mod3, norm_final.reshape(1, d), seq, comb_tile,
                      final_norm=(layer == depth - 1))
    return x2.reshape(bsz, seq, d)
```

```python
import functools
import math

import jax
import jax.numpy as jnp
import numpy as np
from jax import lax
from jax.experimental import pallas as pl
from jax.experimental.pallas import tpu as pltpu

F32 = jnp.float32
BF16 = jnp.bfloat16
I32 = jnp.int32

D_MODEL = 1024
RET_HEADS = 4
RET_HEAD_DIM = 256
RET_WIDTH = RET_HEADS * RET_HEAD_DIM
RET_CHUNK = 128
ROPE_BASE = 10000.0
ATT_HEADS = 16
ATT_HEAD_DIM = 64
ATT_WIDTH = ATT_HEADS * ATT_HEAD_DIM
DILATION_PATTERNS = ((128, 1), (512, 4), (2048, 16))
REL_BUCKETS = 32
REL_MAX_DISTANCE = 1024
N_EXPERTS = 256
TOP_K = 8
N_GROUPS = 8
TOPK_GROUPS = 4
EXPERT_HIDDEN = 256
ROUTED_SCALE = 2.5
MOE_BLOCK = 256
NORM_EPS = 1e-6
PROJ_WIDTH = 4 * RET_WIDTH + 3 * ATT_WIDTH + 2 * D_MODEL

LANES = 128
SUBLANES = 8
SLABS = D_MODEL // LANES
ATT_RADIUS = 64
ATT_QBLK = 128
ATT_KWIN = 256
COMBINE_ROWS = 32
COMBINE_UNROLL = 8
ATT_UNROLL = 4
NEG_BIG = -1e30
MIB = 1024 * 1024


def _cparams(sem, vmem_mib):
    return pltpu.CompilerParams(dimension_semantics=sem, vmem_limit_bytes=vmem_mib * MIB)


def _sigmoid(x):
    return 1.0 / (1.0 + jnp.exp(-x))


def _silu(x):
    return x * _sigmoid(x)


def _rms(x, gain):
    return x * lax.rsqrt(jnp.mean(x * x, axis=-1, keepdims=True) + NORM_EPS) * gain


def _rows_to_tiles(rows, stage_scr, tiles_ref):
    n = rows.shape[0]
    for s in range(SLABS):
        stage_scr[pl.ds(s, n, stride=SLABS), :] = rows[:, s * LANES:(s + 1) * LANES]
    tiles_ref[...] = stage_scr[...].reshape(n, SLABS, LANES).astype(BF16)


def _tiles_to_rows(tiles, stage_scr):
    n = tiles.shape[0]
    stage_scr[...] = tiles.astype(F32).reshape(n * SLABS, LANES)
    return jnp.concatenate([stage_scr[pl.ds(s, n, stride=SLABS), :] for s in range(SLABS)], axis=1)


def _adaln_body(c_ref, w_ref, b_ref, o_ref):
    cond = _silu(c_ref[...])
    o_ref[...] = jnp.dot(cond, w_ref[...], preferred_element_type=F32,
                         precision=lax.Precision.HIGHEST) + b_ref[...]


def _adaln(c, w, b):
    bsz, d = c.shape
    n = w.shape[1]
    return pl.pallas_call(
        _adaln_body,
        grid=(n // d,),
        in_specs=[pl.BlockSpec((bsz, d), lambda j: (0, 0)),
                  pl.BlockSpec((d, d), lambda j: (0, j)),
                  pl.BlockSpec((1, d), lambda j: (0, j))],
        out_specs=pl.BlockSpec((bsz, d), lambda j: (0, j)),
        out_shape=jax.ShapeDtypeStruct((bsz, n), F32),
        compiler_params=_cparams(("arbitrary",), 32),
        name="adaln",
    )(c, w, b.reshape(1, n))


def _inproj_body(x_ref, gain_ref, mod_ref, w_ref, o_ref, h_scr):
    @pl.when(pl.program_id(1) == 0)
    def _():
        y = _rms(x_ref[...], gain_ref[...])
        h = y * (1.0 + mod_ref[0, 1:2, :]) + mod_ref[0, 0:1, :]
        h_scr[...] = h.astype(BF16)

    o_ref[...] = jnp.dot(h_scr[...], w_ref[...], preferred_element_type=F32).astype(BF16)


def _inproj(x2, gain, mod3, w_bf, seq, tm=1024, tn=1536):
    t, d = x2.shape
    n = w_bf.shape[1]
    return pl.pallas_call(
        _inproj_body,
        grid=(t // tm, n // tn),
        in_specs=[pl.BlockSpec((tm, d), lambda i, j: (i, 0)),
                  pl.BlockSpec((1, d), lambda i, j: (0, 0)),
                  pl.BlockSpec((1, 6, d), lambda i, j: ((i * tm) // seq, 0, 0)),
                  pl.BlockSpec((d, tn), lambda i, j: (0, j))],
        out_specs=pl.BlockSpec((tm, tn), lambda i, j: (i, j)),
        out_shape=jax.ShapeDtypeStruct((t, n), BF16),
        scratch_shapes=[pltpu.VMEM((tm, d), BF16)],
        compiler_params=_cparams(("arbitrary", "arbitrary"), 48),
        name="inproj",
    )(x2, gain, mod3, w_bf)


def _ret_body(lg_ref, q_ref, k_ref, v_ref, g_ref, cos_ref, sin_ref, o_ref,
              qr_scr, kr_scr, accf_scr, accb_scr, stf_scr, stb_scr):
    head = pl.program_id(1)
    seq = q_ref.shape[0]
    n_chunks = seq // RET_CHUNK
    half = RET_HEAD_DIM // 2
    rot_rows = 256

    def rot_step(c, carry):
        r = pl.ds(pl.multiple_of(c * rot_rows, rot_rows), rot_rows)
        cs = cos_ref[r, :]
        sn = sin_ref[r, :]
        for src, dst, scale in ((q_ref, qr_scr, 1.0), (k_ref, kr_scr, RET_HEAD_DIM ** -0.5)):
            t = src[r, :].astype(F32)
            t1, t2 = t[:, :half], t[:, half:]
            dst[r, :half] = ((t1 * cs - t2 * sn) * scale).astype(BF16)
            dst[r, half:] = ((t1 * sn + t2 * cs) * scale).astype(BF16)
        return carry

    lax.fori_loop(0, seq // rot_rows, rot_step, 0)

    ri = lax.broadcasted_iota(I32, (RET_CHUNK, RET_CHUNK), 0)
    ci = lax.broadcasted_iota(I32, (RET_CHUNK, RET_CHUNK), 1)
    rowpos = lax.broadcasted_iota(I32, (RET_CHUNK, RET_HEAD_DIM), 0).astype(F32)

    def decay_tables(lg, forward):
        if forward:
            diff = (ri - ci).astype(F32)
            allowed = ri >= ci
            q_scale = jnp.exp(lg * (rowpos + 1.0))
            k_scale = jnp.exp(lg * (RET_CHUNK - 1.0 - rowpos))
        else:
            diff = (ci - ri).astype(F32)
            allowed = ci > ri
            q_scale = jnp.exp(lg * (RET_CHUNK - rowpos))
            k_scale = jnp.exp(lg * rowpos)
        intra = jnp.where(allowed, jnp.exp(lg * jnp.where(allowed, diff, 0.0)), 0.0)
        chunk_decay = jnp.exp(lg * jnp.full((1, RET_HEAD_DIM), float(RET_CHUNK), F32))
        return intra, q_scale, k_scale, chunk_decay

    def chunk_update(c, tables, st_scr, acc_scr):
        intra, q_scale, k_scale, chunk_decay = tables
        r = pl.ds(pl.multiple_of(c * RET_CHUNK, RET_CHUNK), RET_CHUNK)
        q = qr_scr[r, :]
        k = kr_scr[r, :]
        v = v_ref[r, :]
        scores = lax.dot_general(q, k, (((1,), (1,)), ((), ())),
                                 preferred_element_type=F32) * intra
        state = st_scr[...]
        acc_scr[r, :] = (jnp.dot(scores.astype(BF16), v, preferred_element_type=F32)
                         + jnp.dot((q.astype(F32) * q_scale).astype(BF16), state.astype(BF16),
                                   preferred_element_type=F32))
        k_t = (k.astype(F32) * k_scale).T.astype(BF16)
        st_scr[...] = state * chunk_decay + jnp.dot(k_t, v, preferred_element_type=F32)

    fwd_tables = decay_tables(lg_ref[0, head], True)
    bwd_tables = decay_tables(lg_ref[1, head], False)
    stf_scr[...] = jnp.zeros_like(stf_scr)
    stb_scr[...] = jnp.zeros_like(stb_scr)

    def step(i, carry):
        chunk_update(i, fwd_tables, stf_scr, accf_scr)
        chunk_update(n_chunks - 1 - i, bwd_tables, stb_scr, accb_scr)
        return carry

    lax.fori_loop(0, n_chunks, step, 0)

    def finish(c, carry):
        r = pl.ds(pl.multiple_of(c * rot_rows, rot_rows), rot_rows)
        o = accf_scr[r, :] + accb_scr[r, :]
        mu = jnp.mean(o, axis=-1, keepdims=True)
        oc = o - mu
        var = jnp.mean(oc * oc, axis=-1, keepdims=True)
        o = oc * lax.rsqrt(var + NORM_EPS)
        o_ref[r, :] = (o * _silu(g_ref[r, :].astype(F32))).astype(BF16)
        return carry

    lax.fori_loop(0, seq // rot_rows, finish, 0)


def _retention(proj3, log_gamma, cos, sin):
    bsz, seq, _ = proj3.shape
    hd = RET_HEAD_DIM

    def col(section):
        return pl.BlockSpec((None, seq, hd), lambda b, h, lg: (b, 0, section * RET_HEADS + h))

    return pl.pallas_call(
        _ret_body,
        grid_spec=pltpu.PrefetchScalarGridSpec(
            num_scalar_prefetch=1,
            grid=(bsz, RET_HEADS),
            in_specs=[col(0), col(1), col(2), col(3),
                      pl.BlockSpec((seq, hd // 2), lambda b, h, lg: (0, 0)),
                      pl.BlockSpec((seq, hd // 2), lambda b, h, lg: (0, 0))],
            out_specs=pl.BlockSpec((None, seq, hd), lambda b, h, lg: (b, 0, h)),
            scratch_shapes=[pltpu.VMEM((seq, hd), BF16), pltpu.VMEM((seq, hd), BF16),
                            pltpu.VMEM((seq, hd), F32), pltpu.VMEM((seq, hd), F32),
                            pltpu.VMEM((hd, hd), F32), pltpu.VMEM((hd, hd), F32)]),
        out_shape=jax.ShapeDtypeStruct((bsz, seq, RET_WIDTH), BF16),
        compiler_params=_cparams(("arbitrary", "arbitrary"), 56),
        name="retention",
    )(log_gamma, proj3, proj3, proj3, proj3, cos, sin)


def _t5_bucket(rel):
    half = REL_BUCKETS // 2
    max_exact = half // 2
    n = jnp.abs(rel)
    large = max_exact + (jnp.log(jnp.maximum(n, 1).astype(F32) / max_exact)
                         / math.log(REL_MAX_DISTANCE / max_exact) * (half - max_exact)).astype(I32)
    large = jnp.minimum(large, half - 1)
    return jnp.where(rel > 0, half, 0) + jnp.where(n < max_exact, n, large)


def _band_buckets():
    qi = jnp.arange(ATT_QBLK, dtype=I32)[:, None]
    kj = jnp.arange(ATT_KWIN, dtype=I32)[None, :]
    tables = []
    for _, dilation in DILATION_PATTERNS:
        cases = []
        for offset in (0, -ATT_RADIUS, ATT_QBLK - ATT_KWIN):
            rel = kj + offset - qi
            cases.append(jnp.where(jnp.abs(rel) <= ATT_RADIUS, _t5_bucket(rel * dilation), -1))
        tables.append(jnp.stack(cases, axis=0))
    return jnp.stack(tables, axis=0)


def _bias_body(t5_ref, bucket_ref, o_ref):
    bucket = bucket_ref[...]

    def head(h, carry):
        acc = jnp.full(bucket.shape, NEG_BIG, F32)
        for b in range(REL_BUCKETS):
            acc = jnp.where(bucket == b, t5_ref[b, h], acc)
        o_ref[h] = acc
        return carry

    lax.fori_loop(0, ATT_HEADS, head, 0)


def _attention_bias(t5_bias):
    buckets = _band_buckets()
    n_pat, n_case = buckets.shape[:2]
    return pl.pallas_call(
        _bias_body,
        grid_spec=pltpu.PrefetchScalarGridSpec(
            num_scalar_prefetch=1,
            grid=(n_pat, n_case),
            in_specs=[pl.BlockSpec((None, None, ATT_QBLK, ATT_KWIN), lambda p, c, t5: (p, c, 0, 0))],
            out_specs=pl.BlockSpec((None, ATT_HEADS, None, ATT_QBLK, ATT_KWIN),
                                   lambda p, c, t5: (p, 0, c, 0, 0))),
        out_shape=jax.ShapeDtypeStruct((n_pat, ATT_HEADS, n_case, ATT_QBLK, ATT_KWIN), F32),
        compiler_params=_cparams(("arbitrary", "arbitrary"), 32),
        name="attn_bias",
    )(t5_bias.astype(F32), buckets)


def _attn_body(q_ref, k_ref, v_ref, bias_ref, o_ref,
               qf, kf, vf, qd, kd, vd, od, ld, og, lgs):
    seq = q_ref.shape[0]
    rows = 256
    lane = lax.broadcasted_iota(I32, (ATT_QBLK, LANES), 1)
    head0 = lane < ATT_HEAD_DIM

    def to_f32(c, carry):
        r = pl.ds(pl.multiple_of(c * rows, rows), rows)
        qf[r, :] = q_ref[r, :].astype(F32)
        kf[r, :] = k_ref[r, :].astype(F32)
        vf[r, :] = v_ref[r, :].astype(F32)
        return carry

    lax.fori_loop(0, seq // rows, to_f32, 0)

    def band_blocks(g, length, n_seg, q_src, k_src, v_src, o_dst, l_dst):
        n_qb = length // ATT_QBLK
        n_blocks = n_seg * n_qb
        unroll = min(ATT_UNROLL, n_blocks)

        def qgroup(it, carry):
            for u in range(unroll):
                qblock(it * unroll + u)
            return carry

        def qblock(b):
            qi = b % n_qb
            base = (b // n_qb) * length
            qs = pl.multiple_of(base + qi * ATT_QBLK, ATT_QBLK)
            ws = pl.multiple_of(
                base + jnp.clip(qi * ATT_QBLK - ATT_RADIUS, 0, length - ATT_KWIN), ATT_RADIUS)
            case = jnp.where(qi == 0, 0, jnp.where(qi == n_qb - 1, 2, 1))
            q = q_src[pl.ds(qs, ATT_QBLK), :]
            k = k_src[pl.ds(ws, ATT_KWIN), :]
            v = v_src[pl.ds(ws, ATT_KWIN), :]
            outs, lses = [], []
            for hh in range(2):
                mask = head0 if hh == 0 else jnp.logical_not(head0)
                qm = jnp.where(mask, q, jnp.zeros_like(q))
                s = lax.dot_general(qm, k, (((1,), (1,)), ((), ())), preferred_element_type=F32)
                s = s * (ATT_HEAD_DIM ** -0.5) + bias_ref[g, hh, case]
                m = jnp.max(s, axis=-1, keepdims=True)
                p = jnp.exp(s - m)
                l = jnp.sum(p, axis=-1, keepdims=True)
                outs.append(jnp.dot(p.astype(BF16), v, preferred_element_type=F32) / l)
                lses.append(m + jnp.log(l))
            o_dst[pl.ds(qs, ATT_QBLK), :] = jnp.where(head0, outs[0], outs[1])
            l_dst[pl.ds(qs, ATT_QBLK), :] = jnp.where(head0, lses[0], lses[1])

        lax.fori_loop(0, n_blocks // unroll, qgroup, 0)

    for g, (_, dilation) in enumerate(DILATION_PATTERNS):
        length = seq // dilation
        if dilation == 1:
            band_blocks(g, length, 1, q_ref, k_ref, v_ref, og.at[g], lgs.at[g])
            continue
        n_seg = max(1, min(dilation, ATT_UNROLL * ATT_QBLK // length))

        def residues(it, carry, g=g, dilation=dilation, length=length, n_seg=n_seg):
            for j in range(n_seg):
                strided = pl.ds(it * n_seg + j, length, stride=dilation)
                dense = pl.ds(j * length, length)
                qd[dense, :] = qf[strided, :].astype(BF16)
                kd[dense, :] = kf[strided, :].astype(BF16)
                vd[dense, :] = vf[strided, :].astype(BF16)
            band_blocks(g, length, n_seg, qd, kd, vd, od, ld)
            for j in range(n_seg):
                strided = pl.ds(it * n_seg + j, length, stride=dilation)
                dense = pl.ds(j * length, length)
                og[g, strided, :] = od[dense, :]
                lgs[g, strided, :] = ld[dense, :]
            return carry

        lax.fori_loop(0, dilation // n_seg, residues, 0)

    def merge(c, carry):
        r = pl.ds(pl.multiple_of(c * rows, rows), rows)
        l0, l1, l2 = lgs[0, r, :], lgs[1, r, :], lgs[2, r, :]
        m = jnp.maximum(jnp.maximum(l0, l1), l2)
        w0, w1, w2 = jnp.exp(l0 - m), jnp.exp(l1 - m), jnp.exp(l2 - m)
        num = w0 * og[0, r, :] + w1 * og[1, r, :] + w2 * og[2, r, :]
        o_ref[r, :] = (num / (w0 + w1 + w2)).astype(BF16)
        return carry

    lax.fori_loop(0, seq // rows, merge, 0)


def _attention(proj3, bias_tab):
    bsz, seq, _ = proj3.shape
    n_pat = len(DILATION_PATTERNS)
    pairs = ATT_HEADS // 2
    base = 4 * RET_WIDTH // LANES

    def col(section):
        return pl.BlockSpec((None, seq, LANES),
                            lambda b, hp: (b, 0, base + section * (ATT_WIDTH // LANES) + hp))

    return pl.pallas_call(
        _attn_body,
        grid=(bsz, pairs),
        in_specs=[col(0), col(1), col(2),
                  pl.BlockSpec((n_pat, 2, 3, ATT_QBLK, ATT_KWIN), lambda b, hp: (0, hp, 0, 0, 0))],
        out_specs=pl.BlockSpec((None, seq, LANES), lambda b, hp: (b, 0, hp)),
        out_shape=jax.ShapeDtypeStruct((bsz, seq, ATT_WIDTH), BF16),
        scratch_shapes=[pltpu.VMEM((seq, LANES), F32)] * 3
                       + [pltpu.VMEM((seq // 4, LANES), BF16)] * 3
                       + [pltpu.VMEM((seq // 4, LANES), F32)] * 2
                       + [pltpu.VMEM((n_pat, seq, LANES), F32)] * 2,
        compiler_params=_cparams(("arbitrary", "arbitrary"), 56),
        name="attention",
    )(proj3, proj3, proj3, bias_tab)


def _mix_body(ret_ref, att_ref, gr_ref, ga_ref, x_ref, mod_ref, gain_ref,
              wr_ref, wa_ref, wo_ref, wrt_ref, x1_ref, hslab_ref, hrow_ref, logit_ref, slab_scr):
    tm = x_ref.shape[0]
    y_ret = jnp.dot(ret_ref[...], wr_ref[...], preferred_element_type=F32)
    y_att = jnp.dot(att_ref[...], wa_ref[...], preferred_element_type=F32)
    merged = (_sigmoid(gr_ref[...].astype(F32)) * y_ret
              + _sigmoid(ga_ref[...].astype(F32)) * y_att)
    mixed = jnp.dot(merged.astype(BF16), wo_ref[...], preferred_element_type=F32)
    x1 = x_ref[...] + mod_ref[0, 2:3, :] * mixed
    x1_ref[...] = x1
    h = _rms(x1, gain_ref[...]) * (1.0 + mod_ref[0, 4:5, :]) + mod_ref[0, 3:4, :]
    hrow_ref[...] = h.astype(BF16)
    _rows_to_tiles(h, slab_scr, hslab_ref)
    logit_ref[...] = jnp.dot(h, wrt_ref[...], preferred_element_type=F32,
                             precision=lax.Precision.HIGHEST)


def _mix(ret2, att2, proj2, x2, mod3, gain, wr, wa, wo, w_router, seq, tm=512):
    t, d = x2.shape
    gate_base = (4 * RET_WIDTH + 3 * ATT_WIDTH) // d
    row = lambda i: (i, 0)
    const = lambda i: (0, 0)
    return pl.pallas_call(
        _mix_body,
        grid=(t // tm,),
        in_specs=[pl.BlockSpec((tm, d), row), pl.BlockSpec((tm, d), row),
                  pl.BlockSpec((tm, d), lambda i: (i, gate_base)),
                  pl.BlockSpec((tm, d), lambda i: (i, gate_base + 1)),
                  pl.BlockSpec((tm, d), row),
                  pl.BlockSpec((1, 6, d), lambda i: ((i * tm) // seq, 0, 0)),
                  pl.BlockSpec((1, d), const),
                  pl.BlockSpec((d, d), const), pl.BlockSpec((d, d), const),
                  pl.BlockSpec((d, d), const), pl.BlockSpec((d, N_EXPERTS), const)],
        out_specs=[pl.BlockSpec((tm, d), row),
                   pl.BlockSpec((tm, SLABS, LANES), lambda i: (i, 0, 0)),
                   pl.BlockSpec((tm, d), row),
                   pl.BlockSpec((tm, N_EXPERTS), row)],
        out_shape=[jax.ShapeDtypeStruct((t, d), F32),
                   jax.ShapeDtypeStruct((t, SLABS, LANES), BF16),
                   jax.ShapeDtypeStruct((t, d), BF16),
                   jax.ShapeDtypeStruct((t, N_EXPERTS), F32)],
        scratch_shapes=[pltpu.VMEM((tm * SLABS, LANES), F32)],
        compiler_params=_cparams(("arbitrary",), 56),
        name="mix",
    )(ret2, att2, proj2, proj2, x2, mod3, gain, wr, wa, wo, w_router)


def _route_body(logit_ref, bias_ref, eidx_ref, gate_ref, rank_ref, cnt_ref, carry_scr, tri_scr):
    tn = logit_ref.shape[0]
    per_group = N_EXPERTS // N_GROUPS

    @pl.when(pl.program_id(0) == 0)
    def _():
        carry_scr[...] = jnp.zeros_like(carry_scr)
        r = lax.broadcasted_iota(I32, (tn, tn), 0)
        c = lax.broadcasted_iota(I32, (tn, tn), 1)
        tri_scr[...] = jnp.where(r < c, 1.0, 0.0).astype(BF16)

    scores = _sigmoid(logit_ref[...]).T
    choice = scores + jnp.concatenate([bias_ref[...]] * (tn // LANES), axis=1)
    neg_inf = -jnp.inf

    sub = lax.broadcasted_iota(I32, (per_group, tn), 0).astype(F32)
    group_score = []
    for g in range(N_GROUPS):
        cg = choice[g * per_group:(g + 1) * per_group, :]
        m1 = jnp.max(cg, axis=0, keepdims=True)
        first = jnp.min(jnp.where(cg == m1, sub, float(per_group)), axis=0, keepdims=True)
        m2 = jnp.max(jnp.where(sub == first, neg_inf, cg), axis=0, keepdims=True)
        group_score.append(m1 + m2)

    masked = []
    for a in range(N_GROUPS):
        beaten = jnp.zeros((1, tn), F32)
        for b in range(N_GROUPS):
            if b == a:
                continue
            wins = (group_score[b] >= group_score[a]) if b < a else (group_score[b] > group_score[a])
            beaten = beaten + jnp.where(wins, 1.0, 0.0)
        keep = beaten < float(TOPK_GROUPS)
        masked.append(jnp.where(keep, choice[a * per_group:(a + 1) * per_group, :], neg_inf))
    work = jnp.concatenate(masked, axis=0)

    eid = lax.broadcasted_iota(I32, (N_EXPERTS, tn), 0).astype(F32)
    picked = jnp.zeros((N_EXPERTS, tn), F32)
    idx_rows, gate_rows = [], []
    for _ in range(TOP_K):
        m = jnp.max(work, axis=0, keepdims=True)
        idx = jnp.min(jnp.where(work == m, eid, float(N_EXPERTS)), axis=0, keepdims=True)
        sel = eid == idx
        gate_rows.append(jnp.sum(jnp.where(sel, scores, 0.0), axis=0, keepdims=True))
        picked = picked + jnp.where(sel, 1.0, 0.0)
        work = jnp.where(sel, neg_inf, work)
        idx_rows.append(idx)

    before = (jnp.dot(picked.astype(BF16), tri_scr[...], preferred_element_type=F32)
              + carry_scr[:, 0:1])
    rank_rows = [jnp.sum(jnp.where(eid == idx, before, 0.0), axis=0, keepdims=True)
                 for idx in idx_rows]
    carry = carry_scr[...] + jnp.sum(picked, axis=1, keepdims=True)
    carry_scr[...] = carry
    cnt_ref[...] = carry

    gates = jnp.concatenate(gate_rows, axis=0)
    gates = gates / jnp.sum(gates, axis=0, keepdims=True) * ROUTED_SCALE
    eidx_ref[...] = jnp.concatenate(idx_rows, axis=0).astype(I32)
    gate_ref[...] = gates
    rank_ref[...] = jnp.concatenate(rank_rows, axis=0).astype(I32)


def _route(logits, bias_b, tn=512):
    t = logits.shape[0]
    tok = lambda i: (0, i)
    return pl.pallas_call(
        _route_body,
        grid=(t // tn,),
        in_specs=[pl.BlockSpec((tn, N_EXPERTS), lambda i: (i, 0)),
                  pl.BlockSpec((N_EXPERTS, LANES), lambda i: (0, 0))],
        out_specs=[pl.BlockSpec((TOP_K, tn), tok), pl.BlockSpec((TOP_K, tn), tok),
                   pl.BlockSpec((TOP_K, tn), tok),
                   pl.BlockSpec((N_EXPERTS, LANES), lambda i: (0, 0))],
        out_shape=[jax.ShapeDtypeStruct((TOP_K, t), I32), jax.ShapeDtypeStruct((TOP_K, t), F32),
                   jax.ShapeDtypeStruct((TOP_K, t), I32),
                   jax.ShapeDtypeStruct((N_EXPERTS, LANES), F32)],
        scratch_shapes=[pltpu.VMEM((N_EXPERTS, LANES), F32), pltpu.VMEM((tn, tn), BF16)],
        compiler_params=_cparams(("arbitrary",), 48),
        name="route",
    )(logits, bias_b)


def _dest_body(start_ref, eidx_ref, rank_ref, dest_ref):
    e = eidx_ref[...]

    def body(j, acc):
        return jnp.where(e == j, start_ref[j], acc)

    dest_ref[...] = rank_ref[...] + lax.fori_loop(0, N_EXPERTS, body, jnp.zeros_like(e))


def _dest(pad_start, eidx, rank, tn=2048):
    t = eidx.shape[1]
    tok = lambda i, s: (0, i)
    return pl.pallas_call(
        _dest_body,
        grid_spec=pltpu.PrefetchScalarGridSpec(
            num_scalar_prefetch=1,
            grid=(t // tn,),
            in_specs=[pl.BlockSpec((TOP_K, tn), tok), pl.BlockSpec((TOP_K, tn), tok)],
            out_specs=pl.BlockSpec((TOP_K, tn), tok)),
        out_shape=jax.ShapeDtypeStruct((TOP_K, t), I32),
        compiler_params=_cparams(("arbitrary",), 32),
        name="dest",
    )(pad_start, eidx, rank)


def _dispatch_body(dest_hbm, h_ref, xs_hbm, dest_smem, idx_sem, row_sem):
    tt = dest_smem.shape[0] // TOP_K
    idx_copy = pltpu.make_async_copy(dest_hbm.at[pl.program_id(0)], dest_smem, idx_sem)
    idx_copy.start()
    idx_copy.wait()

    def issue(t, carry):
        for k in range(TOP_K):
            pltpu.make_async_copy(h_ref.at[t], xs_hbm.at[dest_smem[k * tt + t]],
                                  row_sem).start(priority=k % 2)
        return carry

    lax.fori_loop(0, tt, issue, 0)

    for k in range(TOP_K):
        pltpu.make_async_copy(h_ref, xs_hbm.at[pl.ds(0, tt)], row_sem).wait()


def _dispatch(dest_tiles, htiles, n_rows):
    n_tiles, width = dest_tiles.shape
    tt = width // TOP_K
    return pl.pallas_call(
        _dispatch_body,
        grid=(n_tiles,),
        in_specs=[pl.BlockSpec(memory_space=pl.ANY),
                  pl.BlockSpec((tt, SLABS, LANES), lambda i: (i, 0, 0))],
        out_specs=pl.BlockSpec(memory_space=pl.ANY),
        out_shape=jax.ShapeDtypeStruct((n_rows, SLABS, LANES), BF16),
        scratch_shapes=[pltpu.SMEM((width,), I32), pltpu.SemaphoreType.DMA, pltpu.SemaphoreType.DMA],
        compiler_params=_cparams(("arbitrary",), 32),
        name="dispatch",
    )(dest_tiles, htiles)


def _experts_body(bexp_ref, bnew_ref, nused_ref, xs_ref, wg_ref, wu_ref, wd_ref, ys_ref,
                  wgu_s, wd_s, stage_scr):
    i = pl.program_id(0)
    hid = wd_s.shape[0]

    @pl.when(i < nused_ref[0])
    def _():
        @pl.when(bnew_ref[i] == 1)
        def _():
            wgu_s[:, :hid] = wg_ref[...].astype(BF16)
            wgu_s[:, hid:] = wu_ref[...].astype(BF16)
            wd_s[...] = wd_ref[...].astype(BF16)

        x = _tiles_to_rows(xs_ref[...], stage_scr).astype(BF16)
        gu = jnp.dot(x, wgu_s[...], preferred_element_type=F32)
        hg, hu = gu[:, :hid], gu[:, hid:]
        y = jnp.dot((_silu(hg) * hu).astype(BF16), wd_s[...], preferred_element_type=F32)
        _rows_to_tiles(y, stage_scr, ys_ref)


def _experts(block_expert, block_new, n_used, xs, w_gate, w_up, w_down):
    n_blocks = block_expert.shape[0]
    d, hid = w_gate.shape[1], w_gate.shape[2]
    tile_block = (MOE_BLOCK, SLABS, LANES)

    def blk(i, be, bn, nu):
        return (jnp.minimum(i, nu[0] - 1), 0, 0)

    def wsel(i, be, bn, nu):
        return (be[jnp.minimum(i, nu[0] - 1)], 0, 0)

    return pl.pallas_call(
        _experts_body,
        grid_spec=pltpu.PrefetchScalarGridSpec(
            num_scalar_prefetch=3,
            grid=(n_blocks,),
            in_specs=[pl.BlockSpec(tile_block, blk),
                      pl.BlockSpec((None, d, hid), wsel),
                      pl.BlockSpec((None, d, hid), wsel),
                      pl.BlockSpec((None, hid, d), wsel)],
            out_specs=pl.BlockSpec(tile_block, blk),
            scratch_shapes=[pltpu.VMEM((d, 2 * hid), BF16), pltpu.VMEM((hid, d), BF16),
                            pltpu.VMEM((MOE_BLOCK * SLABS, LANES), F32)]),
        out_shape=jax.ShapeDtypeStruct(xs.shape, BF16),
        compiler_params=_cparams(("arbitrary",), 32),
        name="experts",
    )(block_expert, block_new, n_used, xs, w_gate, w_up, w_down)


def _combine_body(dest_hbm, gate_hbm, ys_hbm, x1_ref, h_ref, wsg_ref, wsu_ref, wsd_ref,
                  mod_ref, gain_ref, o_ref, dest_smem0, dest_smem1, gate_smem, buf, shared_scr,
                  routed_scr, idx_sem, gate_sem, row_sem, *, final_norm):
    tc = x1_ref.shape[0]
    i = pl.program_id(0)
    has_next = i + 1 < pl.num_programs(0)
    cur = i % 2
    dest_smem = (dest_smem0, dest_smem1)

    def index_copy(tile, slot):
        return pltpu.make_async_copy(dest_hbm.at[tile], dest_smem[slot], idx_sem)

    def start_rows(slot):
        def issue(t, carry):
            for k in range(TOP_K):
                pltpu.make_async_copy(ys_hbm.at[dest_smem[slot][k * tc + t]], buf.at[slot, k, t],
                                      row_sem.at[slot]).start(priority=k % 2)
            return carry

        lax.fori_loop(0, tc, issue, 0)

    gate_copy = pltpu.make_async_copy(gate_hbm.at[i], gate_smem, gate_sem)
    gate_copy.start()

    @pl.when(i == 0)
    def _():
        index_copy(0, 0).start()
        index_copy(0, 0).wait()
        start_rows(0)

    for slot in range(2):
        @pl.when(jnp.logical_and(has_next, cur != slot))
        def _(slot=slot):
            index_copy(i + 1, slot).start()

    h = h_ref[...]
    hid = (_silu(jnp.dot(h, wsg_ref[...], preferred_element_type=F32))
           * jnp.dot(h, wsu_ref[...], preferred_element_type=F32))
    shared_scr[...] = jnp.dot(hid.astype(BF16), wsd_ref[...], preferred_element_type=F32)

    for slot in range(2):
        @pl.when(jnp.logical_and(has_next, cur != slot))
        def _(slot=slot):
            index_copy(i + 1, slot).wait()
            start_rows(slot)

    for k in range(TOP_K):
        pltpu.make_async_copy(ys_hbm.at[pl.ds(0, tc)], buf.at[cur, k], row_sem.at[cur]).wait()
    gate_copy.wait()

    def weigh(g, carry):
        for u in range(COMBINE_UNROLL):
            t = g * COMBINE_UNROLL + u
            acc = gate_smem[t] * buf[cur, 0, t].astype(F32)
            for k in range(1, TOP_K):
                acc = acc + gate_smem[k * tc + t] * buf[cur, k, t].astype(F32)
            routed_scr[pl.ds(pl.multiple_of(t * SLABS, SLABS), SLABS), :] = acc
        return carry

    lax.fori_loop(0, tc // COMBINE_UNROLL, weigh, 0)

    gate_f = mod_ref[0, 5:6, :]
    gain = gain_ref[...]

    def finish(c, carry):
        r0 = pl.multiple_of(c * COMBINE_ROWS, COMBINE_ROWS)
        rows = pl.ds(r0, COMBINE_ROWS)
        routed = jnp.concatenate(
            [routed_scr[pl.ds(r0 * SLABS + s, COMBINE_ROWS, stride=SLABS), :]
             for s in range(SLABS)], axis=1)
        x2 = x1_ref[rows, :] + gate_f * (shared_scr[rows, :] + routed)
        o_ref[rows, :] = _rms(x2, gain) if final_norm else x2
        return carry

    lax.fori_loop(0, tc // COMBINE_ROWS, finish, 0)


def _combine(dest_tiles, gate_tiles, ys, x1, hrow, wsg, wsu, wsd, mod3, gain, seq, tc, final_norm):
    t, d = x1.shape
    hid = wsg.shape[1]
    row = lambda i: (i, 0)
    const = lambda i: (0, 0)
    return pl.pallas_call(
        functools.partial(_combine_body, final_norm=final_norm),
        grid=(t // tc,),
        in_specs=[pl.BlockSpec(memory_space=pl.ANY),
                  pl.BlockSpec(memory_space=pl.ANY),
                  pl.BlockSpec(memory_space=pl.ANY),
                  pl.BlockSpec((tc, d), row), pl.BlockSpec((tc, d), row),
                  pl.BlockSpec((d, hid), const), pl.BlockSpec((d, hid), const),
                  pl.BlockSpec((hid, d), const),
                  pl.BlockSpec((1, 6, d), lambda i: ((i * tc) // seq, 0, 0)),
                  pl.BlockSpec((1, d), const)],
        out_specs=pl.BlockSpec((tc, d), row),
        out_shape=jax.ShapeDtypeStruct((t, d), F32),
        scratch_shapes=[pltpu.SMEM((TOP_K * tc,), I32), pltpu.SMEM((TOP_K * tc,), I32),
                        pltpu.SMEM((TOP_K * tc,), F32),
                        pltpu.VMEM((2, TOP_K, tc, SLABS, LANES), BF16),
                        pltpu.VMEM((tc, d), F32),
                        pltpu.VMEM((tc * SLABS, LANES), F32),
                        pltpu.SemaphoreType.DMA, pltpu.SemaphoreType.DMA,
                        pltpu.SemaphoreType.DMA((2,))],
        compiler_params=_cparams(("arbitrary",), 56),
        name="combine",
    )(dest_tiles, gate_tiles, ys, x1, hrow, wsg, wsu, wsd, mod3, gain)


def _tile_major(a, tile):
    k, t = a.shape
    return a.reshape(k, t // tile, tile).transpose(1, 0, 2).reshape(t // tile, k * tile)


def kernel(x, c, w_ada, b_ada, norm_mix, w_in, ret_decay, t5_bias, w_ret_up, w_att_up, w_o,
           norm_ffn, w_router, router_bias, w_gate, w_up, w_down, ws_gate, ws_up, ws_down, norm_final):
    bsz, seq, d = x.shape
    depth = w_ada.shape[0]
    t = bsz * seq
    assert d == D_MODEL and seq % (ATT_KWIN * DILATION_PATTERNS[-1][1]) == 0

    half = RET_HEAD_DIM // 2
    inv_freq = ROPE_BASE ** (-jnp.arange(half, dtype=F32) / half)
    ang = jnp.arange(seq, dtype=F32)[:, None] * inv_freq[None, :]
    cos, sin = jnp.cos(ang), jnp.sin(ang)
    bias_tab = _attention_bias(t5_bias)

    n_assign = t * TOP_K
    n_blocks = -(-n_assign // MOE_BLOCK) + N_EXPERTS
    n_rows = n_blocks * MOE_BLOCK
    disp_tile = min(1024, t)
    comb_tile = min(256, t)

    x2 = x.reshape(t, d)
    for layer in range(depth):
        mod3 = _adaln(c, w_ada[layer], b_ada[layer]).reshape(bsz, 6, d)
        proj = _inproj(x2, norm_mix[layer].reshape(1, d), mod3, w_in[layer].astype(BF16), seq)
        proj3 = proj.reshape(bsz, seq, PROJ_WIDTH)
        log_gamma = jnp.log1p(-jnp.exp(ret_decay[layer].astype(F32)))
        ret = _retention(proj3, log_gamma, cos, sin)
        att = _attention(proj3, bias_tab)
        x1, hslab, hrow, logits = _mix(
            ret.reshape(t, RET_WIDTH), att.reshape(t, ATT_WIDTH), proj, x2, mod3,
            norm_ffn[layer].reshape(1, d), w_ret_up[layer].astype(BF16),
            w_att_up[layer].astype(BF16), w_o[layer].astype(BF16), w_router[layer], seq)

        bias_b = jnp.broadcast_to(router_bias[layer].astype(F32)[:, None], (N_EXPERTS, LANES))
        eidx, gate, rank, counts = _route(logits, bias_b)

        counts = counts[:, 0].astype(I32)
        padded = (counts + MOE_BLOCK - 1) // MOE_BLOCK * MOE_BLOCK
        pad_end = jnp.cumsum(padded)
        pad_start = pad_end - padded
        block_row = jnp.arange(n_blocks, dtype=I32) * MOE_BLOCK
        block_expert = jnp.minimum(
            jnp.sum((pad_end[None, :] <= block_row[:, None]).astype(I32), axis=1), N_EXPERTS - 1)
        block_new = jnp.concatenate(
            [jnp.ones((1,), I32), (block_expert[1:] != block_expert[:-1]).astype(I32)])
        n_used = (pad_end[-1:] // MOE_BLOCK).astype(I32)

        dest = _dest(pad_start.astype(I32), eidx, rank)
        xs = _dispatch(_tile_major(dest, disp_tile), hslab, n_rows)
        ys = _experts(block_expert, block_new, n_used, xs,
                      w_gate[layer], w_up[layer], w_down[layer])
        x2 = _combine(_tile_major(dest, comb_tile), _tile_major(gate, comb_tile), ys,
                      x1, hrow, ws_gate[layer].astype(BF16), ws_up[layer].astype(BF16),
                      ws_down[layer].astype(BF16), mod3, norm_final.reshape(1, d), seq, comb_tile,
                      final_norm=(layer == depth - 1))
    return x2.reshape(bsz, seq, d)
```

```python
import functools
import math

import jax
import jax.numpy as jnp
import numpy as np
from jax import lax
from jax.experimental import pallas as pl
from jax.experimental.pallas import tpu as pltpu

F32 = jnp.float32
BF16 = jnp.bfloat16
I32 = jnp.int32

D_MODEL = 1024
RET_HEADS = 4
RET_HEAD_DIM = 256
RET_WIDTH = RET_HEADS * RET_HEAD_DIM
RET_CHUNK = 128
ROPE_BASE = 10000.0
ATT_HEADS = 16
ATT_HEAD_DIM = 64
ATT_WIDTH = ATT_HEADS * ATT_HEAD_DIM
DILATION_PATTERNS = ((128, 1), (512, 4), (2048, 16))
REL_BUCKETS = 32
REL_MAX_DISTANCE = 1024
N_EXPERTS = 256
TOP_K = 8
N_GROUPS = 8
TOPK_GROUPS = 4
EXPERT_HIDDEN = 256
ROUTED_SCALE = 2.5
MOE_BLOCK = 256
NORM_EPS = 1e-6
PROJ_WIDTH = 4 * RET_WIDTH + 3 * ATT_WIDTH + 2 * D_MODEL

LANES = 128
SUBLANES = 8
SLABS = D_MODEL // LANES
ATT_RADIUS = 64
ATT_QBLK = 128
ATT_KWIN = 256
COMBINE_ROWS = 32
COMBINE_UNROLL = 8
ATT_UNROLL = 8
NEG_BIG = -1e30
MIB = 1024 * 1024


def _cparams(sem, vmem_mib):
    return pltpu.CompilerParams(dimension_semantics=sem, vmem_limit_bytes=vmem_mib * MIB)


def _sigmoid(x):
    return 1.0 / (1.0 + jnp.exp(-x))


def _silu(x):
    return x * _sigmoid(x)


def _rms(x, gain):
    return x * lax.rsqrt(jnp.mean(x * x, axis=-1, keepdims=True) + NORM_EPS) * gain


def _rows_to_tiles(rows, stage_scr, tiles_ref):
    n = rows.shape[0]
    for s in range(SLABS):
        stage_scr[pl.ds(s, n, stride=SLABS), :] = rows[:, s * LANES:(s + 1) * LANES]
    tiles_ref[...] = stage_scr[...].reshape(n, SLABS, LANES).astype(BF16)


def _tiles_to_rows(tiles, stage_scr):
    n = tiles.shape[0]
    stage_scr[...] = tiles.astype(F32).reshape(n * SLABS, LANES)
    return jnp.concatenate([stage_scr[pl.ds(s, n, stride=SLABS), :] for s in range(SLABS)], axis=1)


def _adaln_body(c_ref, w_ref, b_ref, o_ref):
    cond = _silu(c_ref[...])
    o_ref[...] = jnp.dot(cond, w_ref[...], preferred_element_type=F32,
                         precision=lax.Precision.HIGHEST) + b_ref[...]


def _adaln(c, w, b):
    bsz, d = c.shape
    n = w.shape[1]
    return pl.pallas_call(
        _adaln_body,
        grid=(n // d,),
        in_specs=[pl.BlockSpec((bsz, d), lambda j: (0, 0)),
                  pl.BlockSpec((d, d), lambda j: (0, j)),
                  pl.BlockSpec((1, d), lambda j: (0, j))],
        out_specs=pl.BlockSpec((bsz, d), lambda j: (0, j)),
        out_shape=jax.ShapeDtypeStruct((bsz, n), F32),
        compiler_params=_cparams(("arbitrary",), 32),
        name="adaln",
    )(c, w, b.reshape(1, n))


def _inproj_body(x_ref, gain_ref, mod_ref, w_ref, o_ref, h_scr):
    @pl.when(pl.program_id(1) == 0)
    def _():
        y = _rms(x_ref[...], gain_ref[...])
        h = y * (1.0 + mod_ref[0, 1:2, :]) + mod_ref[0, 0:1, :]
        h_scr[...] = h.astype(BF16)

    o_ref[...] = jnp.dot(h_scr[...], w_ref[...], preferred_element_type=F32).astype(BF16)


def _inproj(x2, gain, mod3, w_bf, seq, tm=1024, tn=1536):
    t, d = x2.shape
    n = w_bf.shape[1]
    return pl.pallas_call(
        _inproj_body,
        grid=(t // tm, n // tn),
        in_specs=[pl.BlockSpec((tm, d), lambda i, j: (i, 0)),
                  pl.BlockSpec((1, d), lambda i, j: (0, 0)),
                  pl.BlockSpec((1, 6, d), lambda i, j: ((i * tm) // seq, 0, 0)),
                  pl.BlockSpec((d, tn), lambda i, j: (0, j))],
        out_specs=pl.BlockSpec((tm, tn), lambda i, j: (i, j)),
        out_shape=jax.ShapeDtypeStruct((t, n), BF16),
        scratch_shapes=[pltpu.VMEM((tm, d), BF16)],
        compiler_params=_cparams(("arbitrary", "arbitrary"), 48),
        name="inproj",
    )(x2, gain, mod3, w_bf)


def _ret_body(lg_ref, q_ref, k_ref, v_ref, g_ref, cos_ref, sin_ref, o_ref,
              qr_scr, kr_scr, accf_scr, accb_scr, stf_scr, stb_scr):
    head = pl.program_id(1)
    seq = q_ref.shape[0]
    n_chunks = seq // RET_CHUNK
    half = RET_HEAD_DIM // 2
    rot_rows = 256

    def rot_step(c, carry):
        r = pl.ds(pl.multiple_of(c * rot_rows, rot_rows), rot_rows)
        cs = cos_ref[r, :]
        sn = sin_ref[r, :]
        for src, dst, scale in ((q_ref, qr_scr, 1.0), (k_ref, kr_scr, RET_HEAD_DIM ** -0.5)):
            t = src[r, :].astype(F32)
            t1, t2 = t[:, :half], t[:, half:]
            dst[r, :half] = ((t1 * cs - t2 * sn) * scale).astype(BF16)
            dst[r, half:] = ((t1 * sn + t2 * cs) * scale).astype(BF16)
        return carry

    lax.fori_loop(0, seq // rot_rows, rot_step, 0)

    ri = lax.broadcasted_iota(I32, (RET_CHUNK, RET_CHUNK), 0)
    ci = lax.broadcasted_iota(I32, (RET_CHUNK, RET_CHUNK), 1)
    rowpos = lax.broadcasted_iota(I32, (RET_CHUNK, RET_HEAD_DIM), 0).astype(F32)

    def decay_tables(lg, forward):
        if forward:
            diff = (ri - ci).astype(F32)
            allowed = ri >= ci
            q_scale = jnp.exp(lg * (rowpos + 1.0))
            k_scale = jnp.exp(lg * (RET_CHUNK - 1.0 - rowpos))
        else:
            diff = (ci - ri).astype(F32)
            allowed = ci > ri
            q_scale = jnp.exp(lg * (RET_CHUNK - rowpos))
            k_scale = jnp.exp(lg * rowpos)
        intra = jnp.where(allowed, jnp.exp(lg * jnp.where(allowed, diff, 0.0)), 0.0)
        chunk_decay = jnp.exp(lg * jnp.full((1, RET_HEAD_DIM), float(RET_CHUNK), F32))
        return intra, q_scale, k_scale, chunk_decay

    def chunk_update(c, tables, st_scr, acc_scr):
        intra, q_scale, k_scale, chunk_decay = tables
        r = pl.ds(pl.multiple_of(c * RET_CHUNK, RET_CHUNK), RET_CHUNK)
        q = qr_scr[r, :]
        k = kr_scr[r, :]
        v = v_ref[r, :]
        scores = lax.dot_general(q, k, (((1,), (1,)), ((), ())),
                                 preferred_element_type=F32) * intra
        state = st_scr[...]
        acc_scr[r, :] = (jnp.dot(scores.astype(BF16), v, preferred_element_type=F32)
                         + jnp.dot((q.astype(F32) * q_scale).astype(BF16), state.astype(BF16),
                                   preferred_element_type=F32))
        k_t = (k.astype(F32) * k_scale).T.astype(BF16)
        st_scr[...] = state * chunk_decay + jnp.dot(k_t, v, preferred_element_type=F32)

    fwd_tables = decay_tables(lg_ref[0, head], True)
    bwd_tables = decay_tables(lg_ref[1, head], False)
    stf_scr[...] = jnp.zeros_like(stf_scr)
    stb_scr[...] = jnp.zeros_like(stb_scr)

    def step(i, carry):
        chunk_update(i, fwd_tables, stf_scr, accf_scr)
        chunk_update(n_chunks - 1 - i, bwd_tables, stb_scr, accb_scr)
        return carry

    lax.fori_loop(0, n_chunks, step, 0)

    def finish(c, carry):
        r = pl.ds(pl.multiple_of(c * rot_rows, rot_rows), rot_rows)
        o = accf_scr[r, :] + accb_scr[r, :]
        mu = jnp.mean(o, axis=-1, keepdims=True)
        oc = o - mu
        var = jnp.mean(oc * oc, axis=-1, keepdims=True)
        o = oc * lax.rsqrt(var + NORM_EPS)
        o_ref[r, :] = (o * _silu(g_ref[r, :].astype(F32))).astype(BF16)
        return carry

    lax.fori_loop(0, seq // rot_rows, finish, 0)


def _retention(proj3, log_gamma, cos, sin):
    bsz, seq, _ = proj3.shape
    hd = RET_HEAD_DIM

    def col(section):
        return pl.BlockSpec((None, seq, hd), lambda b, h, lg: (b, 0, section * RET_HEADS + h))

    return pl.pallas_call(
        _ret_body,
        grid_spec=pltpu.PrefetchScalarGridSpec(
            num_scalar_prefetch=1,
            grid=(bsz, RET_HEADS),
            in_specs=[col(0), col(1), col(2), col(3),
                      pl.BlockSpec((seq, hd // 2), lambda b, h, lg: (0, 0)),
                      pl.BlockSpec((seq, hd // 2), lambda b, h, lg: (0, 0))],
            out_specs=pl.BlockSpec((None, seq, hd), lambda b, h, lg: (b, 0, h)),
            scratch_shapes=[pltpu.VMEM((seq, hd), BF16), pltpu.VMEM((seq, hd), BF16),
                            pltpu.VMEM((seq, hd), F32), pltpu.VMEM((seq, hd), F32),
                            pltpu.VMEM((hd, hd), F32), pltpu.VMEM((hd, hd), F32)]),
        out_shape=jax.ShapeDtypeStruct((bsz, seq, RET_WIDTH), BF16),
        compiler_params=_cparams(("arbitrary", "arbitrary"), 56),
        name="retention",
    )(log_gamma, proj3, proj3, proj3, proj3, cos, sin)


def _t5_bucket(rel):
    half = REL_BUCKETS // 2
    max_exact = half // 2
    n = jnp.abs(rel)
    large = max_exact + (jnp.log(jnp.maximum(n, 1).astype(F32) / max_exact)
                         / math.log(REL_MAX_DISTANCE / max_exact) * (half - max_exact)).astype(I32)
    large = jnp.minimum(large, half - 1)
    return jnp.where(rel > 0, half, 0) + jnp.where(n < max_exact, n, large)


def _band_buckets():
    qi = jnp.arange(ATT_QBLK, dtype=I32)[:, None]
    kj = jnp.arange(ATT_KWIN, dtype=I32)[None, :]
    tables = []
    for _, dilation in DILATION_PATTERNS:
        cases = []
        for offset in (0, -ATT_RADIUS, ATT_QBLK - ATT_KWIN):
            rel = kj + offset - qi
            cases.append(jnp.where(jnp.abs(rel) <= ATT_RADIUS, _t5_bucket(rel * dilation), -1))
        tables.append(jnp.stack(cases, axis=0))
    return jnp.stack(tables, axis=0)


def _bias_body(t5_ref, bucket_ref, o_ref):
    bucket = bucket_ref[...]

    def head(h, carry):
        acc = jnp.full(bucket.shape, NEG_BIG, F32)
        for b in range(REL_BUCKETS):
            acc = jnp.where(bucket == b, t5_ref[b, h], acc)
        o_ref[h] = acc
        return carry

    lax.fori_loop(0, ATT_HEADS, head, 0)


def _attention_bias(t5_bias):
    buckets = _band_buckets()
    n_pat, n_case = buckets.shape[:2]
    return pl.pallas_call(
        _bias_body,
        grid_spec=pltpu.PrefetchScalarGridSpec(
            num_scalar_prefetch=1,
            grid=(n_pat, n_case),
            in_specs=[pl.BlockSpec((None, None, ATT_QBLK, ATT_KWIN), lambda p, c, t5: (p, c, 0, 0))],
            out_specs=pl.BlockSpec((None, ATT_HEADS, None, ATT_QBLK, ATT_KWIN),
                                   lambda p, c, t5: (p, 0, c, 0, 0))),
        out_shape=jax.ShapeDtypeStruct((n_pat, ATT_HEADS, n_case, ATT_QBLK, ATT_KWIN), F32),
        compiler_params=_cparams(("arbitrary", "arbitrary"), 32),
        name="attn_bias",
    )(t5_bias.astype(F32), buckets)


def _attn_body(q_ref, k_ref, v_ref, bias_ref, o_ref,
               qf, kf, vf, qb, qd, kd, vd, od, ld, og, lgs):
    seq = q_ref.shape[0]
    rows = 256
    lane = lax.broadcasted_iota(I32, (ATT_QBLK, LANES), 1)
    head0 = lane < ATT_HEAD_DIM

    def to_f32(c, carry):
        r = pl.ds(pl.multiple_of(c * rows, rows), rows)
        q = q_ref[r, :].astype(F32) * (ATT_HEAD_DIM ** -0.5)
        qf[r, :] = q
        qb[r, :] = q.astype(BF16)
        kf[r, :] = k_ref[r, :].astype(F32)
        vf[r, :] = v_ref[r, :].astype(F32)
        return carry

    lax.fori_loop(0, seq // rows, to_f32, 0)

    def band_blocks(g, length, n_seg, q_src, k_src, v_src, o_dst, l_dst):
        n_qb = length // ATT_QBLK
        n_blocks = n_seg * n_qb
        unroll = min(ATT_UNROLL, n_blocks)

        def qgroup(it, carry):
            for u in range(unroll):
                qblock(it * unroll + u)
            return carry

        def qblock(b):
            qi = b % n_qb
            base = (b // n_qb) * length
            qs = pl.multiple_of(base + qi * ATT_QBLK, ATT_QBLK)
            ws = pl.multiple_of(
                base + jnp.clip(qi * ATT_QBLK - ATT_RADIUS, 0, length - ATT_KWIN), ATT_RADIUS)
            case = jnp.where(qi == 0, 0, jnp.where(qi == n_qb - 1, 2, 1))
            q = q_src[pl.ds(qs, ATT_QBLK), :]
            k = k_src[pl.ds(ws, ATT_KWIN), :]
            v = v_src[pl.ds(ws, ATT_KWIN), :]
            outs, lses = [], []
            for hh in range(2):
                mask = head0 if hh == 0 else jnp.logical_not(head0)
                qm = jnp.where(mask, q, jnp.zeros_like(q))
                s = lax.dot_general(qm, k, (((1,), (1,)), ((), ())), preferred_element_type=F32)
                s = s + bias_ref[g, hh, case]
                m = jnp.max(s, axis=-1, keepdims=True)
                p = jnp.exp(s - m)
                l = jnp.sum(p, axis=-1, keepdims=True)
                outs.append(jnp.dot(p.astype(BF16), v, preferred_element_type=F32) / l)
                lses.append(m + jnp.log(l))
            o_dst[pl.ds(qs, ATT_QBLK), :] = jnp.where(head0, outs[0], outs[1])
            l_dst[pl.ds(qs, ATT_QBLK), :] = jnp.where(head0, lses[0], lses[1])

        lax.fori_loop(0, n_blocks // unroll, qgroup, 0)

    for g, (_, dilation) in enumerate(DILATION_PATTERNS):
        length = seq // dilation
        if dilation == 1:
            band_blocks(g, length, 1, qb, k_ref, v_ref, og.at[g], lgs.at[g])
            continue
        n_seg = max(1, min(dilation, ATT_UNROLL * ATT_QBLK // length))

        def residues(it, carry, g=g, dilation=dilation, length=length, n_seg=n_seg):
            for j in range(n_seg):
                strided = pl.ds(it * n_seg + j, length, stride=dilation)
                dense = pl.ds(j * length, length)
                qd[dense, :] = qf[strided, :].astype(BF16)
                kd[dense, :] = kf[strided, :].astype(BF16)
                vd[dense, :] = vf[strided, :].astype(BF16)
            band_blocks(g, length, n_seg, qd, kd, vd, od, ld)
            for j in range(n_seg):
                strided = pl.ds(it * n_seg + j, length, stride=dilation)
                dense = pl.ds(j * length, length)
                og[g, strided, :] = od[dense, :]
                lgs[g, strided, :] = ld[dense, :]
            return carry

        lax.fori_loop(0, dilation // n_seg, residues, 0)

    def merge(c, carry):
        r = pl.ds(pl.multiple_of(c * rows, rows), rows)
        l0, l1, l2 = lgs[0, r, :], lgs[1, r, :], lgs[2, r, :]
        m = jnp.maximum(jnp.maximum(l0, l1), l2)
        w0, w1, w2 = jnp.exp(l0 - m), jnp.exp(l1 - m), jnp.exp(l2 - m)
        num = w0 * og[0, r, :] + w1 * og[1, r, :] + w2 * og[2, r, :]
        o_ref[r, :] = (num / (w0 + w1 + w2)).astype(BF16)
        return carry

    lax.fori_loop(0, seq // rows, merge, 0)


def _attention(proj3, bias_tab):
    bsz, seq, _ = proj3.shape
    n_pat = len(DILATION_PATTERNS)
    pairs = ATT_HEADS // 2
    base = 4 * RET_WIDTH // LANES

    def col(section):
        return pl.BlockSpec((None, seq, LANES),
                            lambda b, hp: (b, 0, base + section * (ATT_WIDTH // LANES) + hp))

    return pl.pallas_call(
        _attn_body,
        grid=(bsz, pairs),
        in_specs=[col(0), col(1), col(2),
                  pl.BlockSpec((n_pat, 2, 3, ATT_QBLK, ATT_KWIN), lambda b, hp: (0, hp, 0, 0, 0))],
        out_specs=pl.BlockSpec((None, seq, LANES), lambda b, hp: (b, 0, hp)),
        out_shape=jax.ShapeDtypeStruct((bsz, seq, ATT_WIDTH), BF16),
        scratch_shapes=[pltpu.VMEM((seq, LANES), F32)] * 3
                       + [pltpu.VMEM((seq, LANES), BF16)]
                       + [pltpu.VMEM((seq // 4, LANES), BF16)] * 3
                       + [pltpu.VMEM((seq // 4, LANES), F32)] * 2
                       + [pltpu.VMEM((n_pat, seq, LANES), F32)] * 2,
        compiler_params=_cparams(("arbitrary", "arbitrary"), 56),
        name="attention",
    )(proj3, proj3, proj3, bias_tab)


def _mix_body(ret_ref, att_ref, gr_ref, ga_ref, x_ref, mod_ref, gain_ref,
              wr_ref, wa_ref, wo_ref, wrt_ref, x1_ref, hslab_ref, hrow_ref, logit_ref, slab_scr):
    tm = x_ref.shape[0]
    y_ret = jnp.dot(ret_ref[...], wr_ref[...], preferred_element_type=F32)
    y_att = jnp.dot(att_ref[...], wa_ref[...], preferred_element_type=F32)
    merged = (_sigmoid(gr_ref[...].astype(F32)) * y_ret
              + _sigmoid(ga_ref[...].astype(F32)) * y_att)
    mixed = jnp.dot(merged.astype(BF16), wo_ref[...], preferred_element_type=F32)
    x1 = x_ref[...] + mod_ref[0, 2:3, :] * mixed
    x1_ref[...] = x1
    h = _rms(x1, gain_ref[...]) * (1.0 + mod_ref[0, 4:5, :]) + mod_ref[0, 3:4, :]
    hrow_ref[...] = h.astype(BF16)
    _rows_to_tiles(h, slab_scr, hslab_ref)
    logit_ref[...] = jnp.dot(h, wrt_ref[...], preferred_element_type=F32,
                             precision=lax.Precision.HIGHEST)


def _mix(ret2, att2, proj2, x2, mod3, gain, wr, wa, wo, w_router, seq, tm=512):
    t, d = x2.shape
    gate_base = (4 * RET_WIDTH + 3 * ATT_WIDTH) // d
    row = lambda i: (i, 0)
    const = lambda i: (0, 0)
    return pl.pallas_call(
        _mix_body,
        grid=(t // tm,),
        in_specs=[pl.BlockSpec((tm, d), row), pl.BlockSpec((tm, d), row),
                  pl.BlockSpec((tm, d), lambda i: (i, gate_base)),
                  pl.BlockSpec((tm, d), lambda i: (i, gate_base + 1)),
                  pl.BlockSpec((tm, d), row),
                  pl.BlockSpec((1, 6, d), lambda i: ((i * tm) // seq, 0, 0)),
                  pl.BlockSpec((1, d), const),
                  pl.BlockSpec((d, d), const), pl.BlockSpec((d, d), const),
                  pl.BlockSpec((d, d), const), pl.BlockSpec((d, N_EXPERTS), const)],
        out_specs=[pl.BlockSpec((tm, d), row),
                   pl.BlockSpec((tm, SLABS, LANES), lambda i: (i, 0, 0)),
                   pl.BlockSpec((tm, d), row),
                   pl.BlockSpec((tm, N_EXPERTS), row)],
        out_shape=[jax.ShapeDtypeStruct((t, d), F32),
                   jax.ShapeDtypeStruct((t, SLABS, LANES), BF16),
                   jax.ShapeDtypeStruct((t, d), BF16),
                   jax.ShapeDtypeStruct((t, N_EXPERTS), F32)],
        scratch_shapes=[pltpu.VMEM((tm * SLABS, LANES), F32)],
        compiler_params=_cparams(("arbitrary",), 56),
        name="mix",
    )(ret2, att2, proj2, proj2, x2, mod3, gain, wr, wa, wo, w_router)


def _route_body(logit_ref, bias_ref, eidx_ref, gate_ref, rank_ref, cnt_ref, carry_scr, tri_scr):
    tn = logit_ref.shape[0]
    per_group = N_EXPERTS // N_GROUPS

    @pl.when(pl.program_id(0) == 0)
    def _():
        carry_scr[...] = jnp.zeros_like(carry_scr)
        r = lax.broadcasted_iota(I32, (tn, tn), 0)
        c = lax.broadcasted_iota(I32, (tn, tn), 1)
        tri_scr[...] = jnp.where(r < c, 1.0, 0.0).astype(BF16)

    scores = _sigmoid(logit_ref[...]).T
    choice = scores + jnp.concatenate([bias_ref[...]] * (tn // LANES), axis=1)
    neg_inf = -jnp.inf

    sub = lax.broadcasted_iota(I32, (per_group, tn), 0).astype(F32)
    group_score = []
    for g in range(N_GROUPS):
        cg = choice[g * per_group:(g + 1) * per_group, :]
        m1 = jnp.max(cg, axis=0, keepdims=True)
        first = jnp.min(jnp.where(cg == m1, sub, float(per_group)), axis=0, keepdims=True)
        m2 = jnp.max(jnp.where(sub == first, neg_inf, cg), axis=0, keepdims=True)
        group_score.append(m1 + m2)

    masked = []
    for a in range(N_GROUPS):
        beaten = jnp.zeros((1, tn), F32)
        for b in range(N_GROUPS):
            if b == a:
                continue
            wins = (group_score[b] >= group_score[a]) if b < a else (group_score[b] > group_score[a])
            beaten = beaten + jnp.where(wins, 1.0, 0.0)
        keep = beaten < float(TOPK_GROUPS)
        masked.append(jnp.where(keep, choice[a * per_group:(a + 1) * per_group, :], neg_inf))
    work = jnp.concatenate(masked, axis=0)

    eid = lax.broadcasted_iota(I32, (N_EXPERTS, tn), 0).astype(F32)
    picked = jnp.zeros((N_EXPERTS, tn), F32)
    idx_rows, gate_rows = [], []
    for _ in range(TOP_K):
        m = jnp.max(work, axis=0, keepdims=True)
        idx = jnp.min(jnp.where(work == m, eid, float(N_EXPERTS)), axis=0, keepdims=True)
        sel = eid == idx
        gate_rows.append(jnp.sum(jnp.where(sel, scores, 0.0), axis=0, keepdims=True))
        picked = picked + jnp.where(sel, 1.0, 0.0)
        work = jnp.where(sel, neg_inf, work)
        idx_rows.append(idx)

    before = (jnp.dot(picked.astype(BF16), tri_scr[...], preferred_element_type=F32)
              + carry_scr[:, 0:1])
    rank_rows = [jnp.sum(jnp.where(eid == idx, before, 0.0), axis=0, keepdims=True)
                 for idx in idx_rows]
    carry = carry_scr[...] + jnp.sum(picked, axis=1, keepdims=True)
    carry_scr[...] = carry
    cnt_ref[...] = carry

    gates = jnp.concatenate(gate_rows, axis=0)
    gates = gates / jnp.sum(gates, axis=0, keepdims=True) * ROUTED_SCALE
    eidx_ref[...] = jnp.concatenate(idx_rows, axis=0).astype(I32)
    gate_ref[...] = gates
    rank_ref[...] = jnp.concatenate(rank_rows, axis=0).astype(I32)


def _route(logits, bias_b, tn=512):
    t = logits.shape[0]
    tok = lambda i: (0, i)
    return pl.pallas_call(
        _route_body,
        grid=(t // tn,),
        in_specs=[pl.BlockSpec((tn, N_EXPERTS), lambda i: (i, 0)),
                  pl.BlockSpec((N_EXPERTS, LANES), lambda i: (0, 0))],
        out_specs=[pl.BlockSpec((TOP_K, tn), tok), pl.BlockSpec((TOP_K, tn), tok),
                   pl.BlockSpec((TOP_K, tn), tok),
                   pl.BlockSpec((N_EXPERTS, LANES), lambda i: (0, 0))],
        out_shape=[jax.ShapeDtypeStruct((TOP_K, t), I32), jax.ShapeDtypeStruct((TOP_K, t), F32),
                   jax.ShapeDtypeStruct((TOP_K, t), I32),
                   jax.ShapeDtypeStruct((N_EXPERTS, LANES), F32)],
        scratch_shapes=[pltpu.VMEM((N_EXPERTS, LANES), F32), pltpu.VMEM((tn, tn), BF16)],
        compiler_params=_cparams(("arbitrary",), 48),
        name="route",
    )(logits, bias_b)


def _dest_body(start_ref, eidx_ref, rank_ref, dest_ref):
    e = eidx_ref[...]

    def body(j, acc):
        return jnp.where(e == j, start_ref[j], acc)

    dest_ref[...] = rank_ref[...] + lax.fori_loop(0, N_EXPERTS, body, jnp.zeros_like(e))


def _dest(pad_start, eidx, rank, tn=2048):
    t = eidx.shape[1]
    tok = lambda i, s: (0, i)
    return pl.pallas_call(
        _dest_body,
        grid_spec=pltpu.PrefetchScalarGridSpec(
            num_scalar_prefetch=1,
            grid=(t // tn,),
            in_specs=[pl.BlockSpec((TOP_K, tn), tok), pl.BlockSpec((TOP_K, tn), tok)],
            out_specs=pl.BlockSpec((TOP_K, tn), tok)),
        out_shape=jax.ShapeDtypeStruct((TOP_K, t), I32),
        compiler_params=_cparams(("arbitrary",), 32),
        name="dest",
    )(pad_start, eidx, rank)


def _dispatch_body(dest_hbm, h_ref, xs_hbm, dest_smem, idx_sem, row_sem):
    tt = dest_smem.shape[0] // TOP_K
    idx_copy = pltpu.make_async_copy(dest_hbm.at[pl.program_id(0)], dest_smem, idx_sem)
    idx_copy.start()
    idx_copy.wait()

    def issue(t, carry):
        for k in range(TOP_K):
            pltpu.make_async_copy(h_ref.at[t], xs_hbm.at[dest_smem[t * TOP_K + k]],
                                  row_sem).start(priority=k % 2)
        return carry

    lax.fori_loop(0, tt, issue, 0)

    for k in range(TOP_K):
        pltpu.make_async_copy(h_ref, xs_hbm.at[pl.ds(0, tt)], row_sem).wait()


def _dispatch(dest_tiles, htiles, n_rows):
    n_tiles, width = dest_tiles.shape
    tt = width // TOP_K
    return pl.pallas_call(
        _dispatch_body,
        grid=(n_tiles,),
        in_specs=[pl.BlockSpec(memory_space=pl.ANY),
                  pl.BlockSpec((tt, SLABS, LANES), lambda i: (i, 0, 0))],
        out_specs=pl.BlockSpec(memory_space=pl.ANY),
        out_shape=jax.ShapeDtypeStruct((n_rows, SLABS, LANES), BF16),
        scratch_shapes=[pltpu.SMEM((width,), I32), pltpu.SemaphoreType.DMA, pltpu.SemaphoreType.DMA],
        compiler_params=_cparams(("arbitrary",), 32),
        name="dispatch",
    )(dest_tiles, htiles)


def _experts_body(bexp_ref, bnew_ref, nused_ref, xs_ref, wg_ref, wu_ref, wd_ref, ys_ref,
                  wgu_s, wd_s, stage_scr):
    i = pl.program_id(0)
    hid = wd_s.shape[0]

    @pl.when(i < nused_ref[0])
    def _():
        @pl.when(bnew_ref[i] == 1)
        def _():
            wgu_s[:, :hid] = wg_ref[...].astype(BF16)
            wgu_s[:, hid:] = wu_ref[...].astype(BF16)
            wd_s[...] = wd_ref[...].astype(BF16)

        x = _tiles_to_rows(xs_ref[...], stage_scr).astype(BF16)
        gu = jnp.dot(x, wgu_s[...], preferred_element_type=F32)
        hg, hu = gu[:, :hid], gu[:, hid:]
        y = jnp.dot((_silu(hg) * hu).astype(BF16), wd_s[...], preferred_element_type=F32)
        _rows_to_tiles(y, stage_scr, ys_ref)


def _experts(block_expert, block_new, n_used, xs, w_gate, w_up, w_down):
    n_blocks = block_expert.shape[0]
    d, hid = w_gate.shape[1], w_gate.shape[2]
    tile_block = (MOE_BLOCK, SLABS, LANES)

    def blk(i, be, bn, nu):
        return (jnp.minimum(i, nu[0] - 1), 0, 0)

    def wsel(i, be, bn, nu):
        return (be[jnp.minimum(i, nu[0] - 1)], 0, 0)

    return pl.pallas_call(
        _experts_body,
        grid_spec=pltpu.PrefetchScalarGridSpec(
            num_scalar_prefetch=3,
            grid=(n_blocks,),
            in_specs=[pl.BlockSpec(tile_block, blk),
                      pl.BlockSpec((None, d, hid), wsel),
                      pl.BlockSpec((None, d, hid), wsel),
                      pl.BlockSpec((None, hid, d), wsel)],
            out_specs=pl.BlockSpec(tile_block, blk),
            scratch_shapes=[pltpu.VMEM((d, 2 * hid), BF16), pltpu.VMEM((hid, d), BF16),
                            pltpu.VMEM((MOE_BLOCK * SLABS, LANES), F32)]),
        out_shape=jax.ShapeDtypeStruct(xs.shape, BF16),
        compiler_params=_cparams(("arbitrary",), 32),
        name="experts",
    )(block_expert, block_new, n_used, xs, w_gate, w_up, w_down)


def _combine_body(dest_hbm, gate_hbm, ys_hbm, x1_ref, h_ref, wsg_ref, wsu_ref, wsd_ref,
                  mod_ref, gain_ref, o_ref, dest_smem0, dest_smem1, gate_smem, buf, shared_scr,
                  routed_scr, idx_sem, gate_sem, row_sem, *, final_norm):
    tc = x1_ref.shape[0]
    i = pl.program_id(0)
    has_next = i + 1 < pl.num_programs(0)
    cur = i % 2
    dest_smem = (dest_smem0, dest_smem1)

    def index_copy(tile, slot):
        return pltpu.make_async_copy(dest_hbm.at[tile], dest_smem[slot], idx_sem)

    def start_rows(slot):
        def issue(t, carry):
            for k in range(TOP_K):
                pltpu.make_async_copy(ys_hbm.at[dest_smem[slot][t * TOP_K + k]],
                                      buf.at[slot, t * TOP_K + k],
                                      row_sem.at[slot]).start(priority=k % 2)
            return carry

        lax.fori_loop(0, tc, issue, 0)

    gate_copy = pltpu.make_async_copy(gate_hbm.at[i], gate_smem, gate_sem)
    gate_copy.start()

    @pl.when(i == 0)
    def _():
        index_copy(0, 0).start()
        index_copy(0, 0).wait()
        start_rows(0)

    for slot in range(2):
        @pl.when(jnp.logical_and(has_next, cur != slot))
        def _(slot=slot):
            index_copy(i + 1, slot).start()

    h = h_ref[...]
    hid = (_silu(jnp.dot(h, wsg_ref[...], preferred_element_type=F32))
           * jnp.dot(h, wsu_ref[...], preferred_element_type=F32))
    shared_scr[...] = jnp.dot(hid.astype(BF16), wsd_ref[...], preferred_element_type=F32)

    for slot in range(2):
        @pl.when(jnp.logical_and(has_next, cur != slot))
        def _(slot=slot):
            index_copy(i + 1, slot).wait()
            start_rows(slot)

    pltpu.make_async_copy(ys_hbm.at[pl.ds(0, tc * TOP_K)], buf.at[cur], row_sem.at[cur]).wait()
    gate_copy.wait()

    def weigh(g, carry):
        for u in range(COMBINE_UNROLL):
            t = g * COMBINE_UNROLL + u
            acc = gate_smem[t * TOP_K] * buf[cur, t * TOP_K].astype(F32)
            for k in range(1, TOP_K):
                acc = acc + gate_smem[t * TOP_K + k] * buf[cur, t * TOP_K + k].astype(F32)
            routed_scr[pl.ds(pl.multiple_of(t * SLABS, SLABS), SLABS), :] = acc
        return carry

    lax.fori_loop(0, tc // COMBINE_UNROLL, weigh, 0)

    gate_f = mod_ref[0, 5:6, :]
    gain = gain_ref[...]

    def finish(c, carry):
        r0 = pl.multiple_of(c * COMBINE_ROWS, COMBINE_ROWS)
        rows = pl.ds(r0, COMBINE_ROWS)
        routed = jnp.concatenate(
            [routed_scr[pl.ds(r0 * SLABS + s, COMBINE_ROWS, stride=SLABS), :]
             for s in range(SLABS)], axis=1)
        x2 = x1_ref[rows, :] + gate_f * (shared_scr[rows, :] + routed)
        o_ref[rows, :] = _rms(x2, gain) if final_norm else x2
        return carry

    lax.fori_loop(0, tc // COMBINE_ROWS, finish, 0)


def _combine(dest_tiles, gate_tiles, ys, x1, hrow, wsg, wsu, wsd, mod3, gain, seq, tc, final_norm):
    t, d = x1.shape
    hid = wsg.shape[1]
    row = lambda i: (i, 0)
    const = lambda i: (0, 0)
    return pl.pallas_call(
        functools.partial(_combine_body, final_norm=final_norm),
        grid=(t // tc,),
        in_specs=[pl.BlockSpec(memory_space=pl.ANY),
                  pl.BlockSpec(memory_space=pl.ANY),
                  pl.BlockSpec(memory_space=pl.ANY),
                  pl.BlockSpec((tc, d), row), pl.BlockSpec((tc, d), row),
                  pl.BlockSpec((d, hid), const), pl.BlockSpec((d, hid), const),
                  pl.BlockSpec((hid, d), const),
                  pl.BlockSpec((1, 6, d), lambda i: ((i * tc) // seq, 0, 0)),
                  pl.BlockSpec((1, d), const)],
        out_specs=pl.BlockSpec((tc, d), row),
        out_shape=jax.ShapeDtypeStruct((t, d), F32),
        scratch_shapes=[pltpu.SMEM((TOP_K * tc,), I32), pltpu.SMEM((TOP_K * tc,), I32),
                        pltpu.SMEM((TOP_K * tc,), F32),
                        pltpu.VMEM((2, tc * TOP_K, SLABS, LANES), BF16),
                        pltpu.VMEM((tc, d), F32),
                        pltpu.VMEM((tc * SLABS, LANES), F32),
                        pltpu.SemaphoreType.DMA, pltpu.SemaphoreType.DMA,
                        pltpu.SemaphoreType.DMA((2,))],
        compiler_params=_cparams(("arbitrary",), 56),
        name="combine",
    )(dest_tiles, gate_tiles, ys, x1, hrow, wsg, wsu, wsd, mod3, gain)


def _tile_major(a, tile):
    k, t = a.shape
    return a.reshape(k, t // tile, tile).transpose(1, 2, 0).reshape(t // tile, tile * k)


def kernel(x, c, w_ada, b_ada, norm_mix, w_in, ret_decay, t5_bias, w_ret_up, w_att_up, w_o,
           norm_ffn, w_router, router_bias, w_gate, w_up, w_down, ws_gate, ws_up, ws_down, norm_final):
    bsz, seq, d = x.shape
    depth = w_ada.shape[0]
    t = bsz * seq
    assert d == D_MODEL and seq % (ATT_KWIN * DILATION_PATTERNS[-1][1]) == 0

    half = RET_HEAD_DIM // 2
    inv_freq = ROPE_BASE ** (-jnp.arange(half, dtype=F32) / half)
    ang = jnp.arange(seq, dtype=F32)[:, None] * inv_freq[None, :]
    cos, sin = jnp.cos(ang), jnp.sin(ang)
    bias_tab = _attention_bias(t5_bias)

    n_assign = t * TOP_K
    n_blocks = -(-n_assign // MOE_BLOCK) + N_EXPERTS
    n_rows = n_blocks * MOE_BLOCK
    disp_tile = min(1024, t)
    comb_tile = min(256, t)

    x2 = x.reshape(t, d)
    for layer in range(depth):
        mod3 = _adaln(c, w_ada[layer], b_ada[layer]).reshape(bsz, 6, d)
        proj = _inproj(x2, norm_mix[layer].reshape(1, d), mod3, w_in[layer].astype(BF16), seq)
        proj3 = proj.reshape(bsz, seq, PROJ_WIDTH)
        log_gamma = jnp.log1p(-jnp.exp(ret_decay[layer].astype(F32)))
        ret = _retention(proj3, log_gamma, cos, sin)
        att = _attention(proj3, bias_tab)
        x1, hslab, hrow, logits = _mix(
            ret.reshape(t, RET_WIDTH), att.reshape(t, ATT_WIDTH), proj, x2, mod3,
            norm_ffn[layer].reshape(1, d), w_ret_up[layer].astype(BF16),
            w_att_up[layer].astype(BF16), w_o[layer].astype(BF16), w_router[layer], seq)

        bias_b = jnp.broadcast_to(router_bias[layer].astype(F32)[:, None], (N_EXPERTS, LANES))
        eidx, gate, rank, counts = _route(logits, bias_b)

        counts = counts[:, 0].astype(I32)
        padded = (counts + MOE_BLOCK - 1) // MOE_BLOCK * MOE_BLOCK
        pad_end = jnp.cumsum(padded)
        pad_start = pad_end - padded
        block_row = jnp.arange(n_blocks, dtype=I32) * MOE_BLOCK
        block_expert = jnp.minimum(
            jnp.sum((pad_end[None, :] <= block_row[:, None]).astype(I32), axis=1), N_EXPERTS - 1)
        block_new = jnp.concatenate(
            [jnp.ones((1,), I32), (block_expert[1:] != block_expert[:-1]).astype(I32)])
        n_used = (pad_end[-1:] // MOE_BLOCK).astype(I32)

        dest = _dest(pad_start.astype(I32), eidx, rank)
        xs = _dispatch(_tile_major(dest, disp_tile), hslab, n_rows)
        ys = _experts(block_expert, block_new, n_used, xs,
                      w_gate[layer], w_up[layer], w_down[layer])
        x2 = _combine(_tile_major(dest, comb_tile), _tile_major(gate, comb_tile), ys,
                      x1, hrow, ws_gate[layer].astype(BF16), ws_up[layer].astype(BF16),
                      ws_down[layer].astype(BF16), mod3, norm_final.reshape(1, d), seq, comb_tile,
                      final_norm=(layer == depth - 1))
    return x2.reshape(bsz, seq, d)
```

```python
import functools
import math

import jax
import jax.numpy as jnp
import numpy as np
from jax import lax
from jax.experimental import pallas as pl
from jax.experimental.pallas import tpu as pltpu

F32 = jnp.float32
BF16 = jnp.bfloat16
I32 = jnp.int32

D_MODEL = 1024
RET_HEADS = 4
RET_HEAD_DIM = 256
RET_WIDTH = RET_HEADS * RET_HEAD_DIM
RET_CHUNK = 128
ROPE_BASE = 10000.0
ATT_HEADS = 16
ATT_HEAD_DIM = 64
ATT_WIDTH = ATT_HEADS * ATT_HEAD_DIM
DILATION_PATTERNS = ((128, 1), (512, 4), (2048, 16))
REL_BUCKETS = 32
REL_MAX_DISTANCE = 1024
N_EXPERTS = 256
TOP_K = 8
N_GROUPS = 8
TOPK_GROUPS = 4
EXPERT_HIDDEN = 256
ROUTED_SCALE = 2.5
MOE_BLOCK = 256
NORM_EPS = 1e-6
PROJ_WIDTH = 4 * RET_WIDTH + 3 * ATT_WIDTH + 2 * D_MODEL

LANES = 128
SUBLANES = 8
SLABS = D_MODEL // LANES
ATT_RADIUS = 64
ATT_QBLK = 128
ATT_KWIN = 256
RET_UNROLL = 4
EXPERT_RING = 3
COMBINE_ROWS = 32
COMBINE_UNROLL = 8
ATT_UNROLL = 8
NEG_BIG = -1e30
MIB = 1024 * 1024


def _cparams(sem, vmem_mib):
    return pltpu.CompilerParams(dimension_semantics=sem, vmem_limit_bytes=vmem_mib * MIB)


def _sigmoid(x):
    return 1.0 / (1.0 + jnp.exp(-x))


def _silu(x):
    return x * _sigmoid(x)


def _rms(x, gain):
    return x * lax.rsqrt(jnp.mean(x * x, axis=-1, keepdims=True) + NORM_EPS) * gain


def _rows_to_tiles(rows, stage_scr, tiles_ref):
    n = rows.shape[0]
    for s in range(SLABS):
        stage_scr[pl.ds(s, n, stride=SLABS), :] = rows[:, s * LANES:(s + 1) * LANES]
    tiles_ref[...] = stage_scr[...].reshape(n, SLABS, LANES).astype(BF16)


def _tiles_to_rows(tiles, stage_scr):
    n = tiles.shape[0]
    stage_scr[...] = tiles.astype(F32).reshape(n * SLABS, LANES)
    return jnp.concatenate([stage_scr[pl.ds(s, n, stride=SLABS), :] for s in range(SLABS)], axis=1)


def _adaln_body(c_ref, w_ref, b_ref, o_ref):
    cond = _silu(c_ref[...])
    o_ref[...] = jnp.dot(cond, w_ref[...], preferred_element_type=F32,
                         precision=lax.Precision.HIGHEST) + b_ref[...]


def _adaln(c, w, b):
    bsz, d = c.shape
    n = w.shape[1]
    return pl.pallas_call(
        _adaln_body,
        grid=(n // d,),
        in_specs=[pl.BlockSpec((bsz, d), lambda j: (0, 0)),
                  pl.BlockSpec((d, d), lambda j: (0, j)),
                  pl.BlockSpec((1, d), lambda j: (0, j))],
        out_specs=pl.BlockSpec((bsz, d), lambda j: (0, j)),
        out_shape=jax.ShapeDtypeStruct((bsz, n), F32),
        compiler_params=_cparams(("arbitrary",), 32),
        name="adaln",
    )(c, w, b.reshape(1, n))


def _inproj_body(x_ref, gain_ref, mod_ref, w_ref, o_ref, h_scr):
    @pl.when(pl.program_id(1) == 0)
    def _():
        y = _rms(x_ref[...], gain_ref[...])
        h = y * (1.0 + mod_ref[0, 1:2, :]) + mod_ref[0, 0:1, :]
        h_scr[...] = h.astype(BF16)

    o_ref[...] = jnp.dot(h_scr[...], w_ref[...], preferred_element_type=F32).astype(BF16)


def _inproj(x2, gain, mod3, w_bf, seq, tm=1024, tn=3072):
    t, d = x2.shape
    n = w_bf.shape[1]
    return pl.pallas_call(
        _inproj_body,
        grid=(t // tm, n // tn),
        in_specs=[pl.BlockSpec((tm, d), lambda i, j: (i, 0)),
                  pl.BlockSpec((1, d), lambda i, j: (0, 0)),
                  pl.BlockSpec((1, 6, d), lambda i, j: ((i * tm) // seq, 0, 0)),
                  pl.BlockSpec((d, tn), lambda i, j: (0, j))],
        out_specs=pl.BlockSpec((tm, tn), lambda i, j: (i, j)),
        out_shape=jax.ShapeDtypeStruct((t, n), BF16),
        scratch_shapes=[pltpu.VMEM((tm, d), BF16)],
        compiler_params=_cparams(("arbitrary", "arbitrary"), 48),
        name="inproj",
    )(x2, gain, mod3, w_bf)


def _ret_body(lg_ref, q_ref, k_ref, v_ref, g_ref, cos_ref, sin_ref, o_ref,
              qr_scr, kr_scr, accf_scr, accb_scr, stf_scr, stb_scr):
    head = pl.program_id(1)
    seq = q_ref.shape[0]
    n_chunks = seq // RET_CHUNK
    half = RET_HEAD_DIM // 2
    rot_rows = 256

    def rot_step(c, carry):
        r = pl.ds(pl.multiple_of(c * rot_rows, rot_rows), rot_rows)
        cs = cos_ref[r, :]
        sn = sin_ref[r, :]
        for src, dst, scale in ((q_ref, qr_scr, 1.0), (k_ref, kr_scr, RET_HEAD_DIM ** -0.5)):
            t = src[r, :].astype(F32)
            t1, t2 = t[:, :half], t[:, half:]
            dst[r, :half] = ((t1 * cs - t2 * sn) * scale).astype(BF16)
            dst[r, half:] = ((t1 * sn + t2 * cs) * scale).astype(BF16)
        return carry

    lax.fori_loop(0, seq // rot_rows, rot_step, 0)

    ri = lax.broadcasted_iota(I32, (RET_CHUNK, RET_CHUNK), 0)
    ci = lax.broadcasted_iota(I32, (RET_CHUNK, RET_CHUNK), 1)
    rowpos = lax.broadcasted_iota(I32, (RET_CHUNK, RET_HEAD_DIM), 0).astype(F32)

    def decay_tables(lg, forward):
        if forward:
            diff = (ri - ci).astype(F32)
            allowed = ri >= ci
            q_scale = jnp.exp(lg * (rowpos + 1.0))
            k_scale = jnp.exp(lg * (RET_CHUNK - 1.0 - rowpos))
        else:
            diff = (ci - ri).astype(F32)
            allowed = ci > ri
            q_scale = jnp.exp(lg * (RET_CHUNK - rowpos))
            k_scale = jnp.exp(lg * rowpos)
        intra = jnp.where(allowed, jnp.exp(lg * jnp.where(allowed, diff, 0.0)), 0.0)
        chunk_decay = jnp.exp(lg * jnp.full((1, RET_HEAD_DIM), float(RET_CHUNK), F32))
        return intra, q_scale, k_scale, chunk_decay

    def chunk_update(c, tables, st_scr, acc_scr):
        intra, q_scale, k_scale, chunk_decay = tables
        r = pl.ds(pl.multiple_of(c * RET_CHUNK, RET_CHUNK), RET_CHUNK)
        q = qr_scr[r, :]
        k = kr_scr[r, :]
        v = v_ref[r, :]
        scores = lax.dot_general(q, k, (((1,), (1,)), ((), ())),
                                 preferred_element_type=F32) * intra
        state = st_scr[...]
        acc_scr[r, :] = (jnp.dot(scores.astype(BF16), v, preferred_element_type=F32)
                         + jnp.dot((q.astype(F32) * q_scale).astype(BF16), state.astype(BF16),
                                   preferred_element_type=F32))
        k_t = (k.astype(F32) * k_scale).T.astype(BF16)
        st_scr[...] = state * chunk_decay + jnp.dot(k_t, v, preferred_element_type=F32)

    fwd_tables = decay_tables(lg_ref[0, head], True)
    bwd_tables = decay_tables(lg_ref[1, head], False)
    stf_scr[...] = jnp.zeros_like(stf_scr)
    stb_scr[...] = jnp.zeros_like(stb_scr)

    def step(trip, carry):
        for u in range(RET_UNROLL):
            i = trip * RET_UNROLL + u
            chunk_update(i, fwd_tables, stf_scr, accf_scr)
            chunk_update(n_chunks - 1 - i, bwd_tables, stb_scr, accb_scr)
        return carry

    lax.fori_loop(0, n_chunks // RET_UNROLL, step, 0)

    def finish(c, carry):
        r = pl.ds(pl.multiple_of(c * rot_rows, rot_rows), rot_rows)
        o = accf_scr[r, :] + accb_scr[r, :]
        mu = jnp.mean(o, axis=-1, keepdims=True)
        oc = o - mu
        var = jnp.mean(oc * oc, axis=-1, keepdims=True)
        o = oc * lax.rsqrt(var + NORM_EPS)
        o_ref[r, :] = (o * _silu(g_ref[r, :].astype(F32))).astype(BF16)
        return carry

    lax.fori_loop(0, seq // rot_rows, finish, 0)


def _retention(proj3, log_gamma, cos, sin):
    bsz, seq, _ = proj3.shape
    hd = RET_HEAD_DIM

    def col(section):
        return pl.BlockSpec((None, seq, hd), lambda b, h, lg: (b, 0, section * RET_HEADS + h))

    return pl.pallas_call(
        _ret_body,
        grid_spec=pltpu.PrefetchScalarGridSpec(
            num_scalar_prefetch=1,
            grid=(bsz, RET_HEADS),
            in_specs=[col(0), col(1), col(2), col(3),
                      pl.BlockSpec((seq, hd // 2), lambda b, h, lg: (0, 0)),
                      pl.BlockSpec((seq, hd // 2), lambda b, h, lg: (0, 0))],
            out_specs=pl.BlockSpec((None, seq, hd), lambda b, h, lg: (b, 0, h)),
            scratch_shapes=[pltpu.VMEM((seq, hd), BF16), pltpu.VMEM((seq, hd), BF16),
                            pltpu.VMEM((seq, hd), F32), pltpu.VMEM((seq, hd), F32),
                            pltpu.VMEM((hd, hd), F32), pltpu.VMEM((hd, hd), F32)]),
        out_shape=jax.ShapeDtypeStruct((bsz, seq, RET_WIDTH), BF16),
        compiler_params=_cparams(("arbitrary", "arbitrary"), 56),
        name="retention",
    )(log_gamma, proj3, proj3, proj3, proj3, cos, sin)


def _t5_bucket(rel):
    half = REL_BUCKETS // 2
    max_exact = half // 2
    n = jnp.abs(rel)
    large = max_exact + (jnp.log(jnp.maximum(n, 1).astype(F32) / max_exact)
                         / math.log(REL_MAX_DISTANCE / max_exact) * (half - max_exact)).astype(I32)
    large = jnp.minimum(large, half - 1)
    return jnp.where(rel > 0, half, 0) + jnp.where(n < max_exact, n, large)


def _band_buckets():
    qi = jnp.arange(ATT_QBLK, dtype=I32)[:, None]
    kj = jnp.arange(ATT_KWIN, dtype=I32)[None, :]
    tables = []
    for _, dilation in DILATION_PATTERNS:
        cases = []
        for offset in (0, -ATT_RADIUS, ATT_QBLK - ATT_KWIN):
            rel = kj + offset - qi
            cases.append(jnp.where(jnp.abs(rel) <= ATT_RADIUS, _t5_bucket(rel * dilation), -1))
        tables.append(jnp.stack(cases, axis=0))
    return jnp.stack(tables, axis=0)


def _bias_body(t5_ref, bucket_ref, o_ref):
    bucket = bucket_ref[...]

    def head(h, carry):
        acc = jnp.full(bucket.shape, NEG_BIG, F32)
        for b in range(REL_BUCKETS):
            acc = jnp.where(bucket == b, t5_ref[b, h], acc)
        o_ref[h] = acc
        return carry

    lax.fori_loop(0, ATT_HEADS, head, 0)


def _attention_bias(t5_bias):
    buckets = _band_buckets()
    n_pat, n_case = buckets.shape[:2]
    return pl.pallas_call(
        _bias_body,
        grid_spec=pltpu.PrefetchScalarGridSpec(
            num_scalar_prefetch=1,
            grid=(n_pat, n_case),
            in_specs=[pl.BlockSpec((None, None, ATT_QBLK, ATT_KWIN), lambda p, c, t5: (p, c, 0, 0))],
            out_specs=pl.BlockSpec((None, ATT_HEADS, None, ATT_QBLK, ATT_KWIN),
                                   lambda p, c, t5: (p, 0, c, 0, 0))),
        out_shape=jax.ShapeDtypeStruct((n_pat, ATT_HEADS, n_case, ATT_QBLK, ATT_KWIN), F32),
        compiler_params=_cparams(("arbitrary", "arbitrary"), 32),
        name="attn_bias",
    )(t5_bias.astype(F32), buckets)


def _attn_body(q_ref, k_ref, v_ref, bias_ref, o_ref,
               qf, kf, vf, qb, qd, kd, vd, od, ld, og, lgs):
    seq = q_ref.shape[0]
    rows = 256
    lane = lax.broadcasted_iota(I32, (ATT_QBLK, LANES), 1)
    head0 = lane < ATT_HEAD_DIM

    def to_f32(c, carry):
        r = pl.ds(pl.multiple_of(c * rows, rows), rows)
        q = q_ref[r, :].astype(F32) * (ATT_HEAD_DIM ** -0.5)
        qf[r, :] = q
        qb[r, :] = q.astype(BF16)
        kf[r, :] = k_ref[r, :].astype(F32)
        vf[r, :] = v_ref[r, :].astype(F32)
        return carry

    lax.fori_loop(0, seq // rows, to_f32, 0)

    def band_blocks(g, length, n_seg, q_src, k_src, v_src, o_dst, l_dst):
        n_qb = length // ATT_QBLK
        n_blocks = n_seg * n_qb
        unroll = min(ATT_UNROLL, n_blocks)

        def qgroup(it, carry):
            for u in range(unroll):
                qblock(it * unroll + u)
            return carry

        def qblock(b):
            qi = b % n_qb
            base = (b // n_qb) * length
            qs = pl.multiple_of(base + qi * ATT_QBLK, ATT_QBLK)
            ws = pl.multiple_of(
                base + jnp.clip(qi * ATT_QBLK - ATT_RADIUS, 0, length - ATT_KWIN), ATT_RADIUS)
            case = jnp.where(qi == 0, 0, jnp.where(qi == n_qb - 1, 2, 1))
            q = q_src[pl.ds(qs, ATT_QBLK), :]
            k = k_src[pl.ds(ws, ATT_KWIN), :]
            v = v_src[pl.ds(ws, ATT_KWIN), :]
            outs, lses = [], []
            for hh in range(2):
                mask = head0 if hh == 0 else jnp.logical_not(head0)
                qm = jnp.where(mask, q, jnp.zeros_like(q))
                s = lax.dot_general(qm, k, (((1,), (1,)), ((), ())), preferred_element_type=F32)
                s = s + bias_ref[g, hh, case]
                m = jnp.max(s, axis=-1, keepdims=True)
                p = jnp.exp(s - m)
                l = jnp.sum(p, axis=-1, keepdims=True)
                outs.append(jnp.dot(p.astype(BF16), v, preferred_element_type=F32) / l)
                lses.append(m + jnp.log(l))
            o_dst[pl.ds(qs, ATT_QBLK), :] = jnp.where(head0, outs[0], outs[1])
            l_dst[pl.ds(qs, ATT_QBLK), :] = jnp.where(head0, lses[0], lses[1])

        lax.fori_loop(0, n_blocks // unroll, qgroup, 0)

    for g, (_, dilation) in enumerate(DILATION_PATTERNS):
        length = seq // dilation
        if dilation == 1:
            band_blocks(g, length, 1, qb, k_ref, v_ref, og.at[g], lgs.at[g])
            continue
        n_seg = max(1, min(dilation, ATT_UNROLL * ATT_QBLK // length))

        def residues(it, carry, g=g, dilation=dilation, length=length, n_seg=n_seg):
            for j in range(n_seg):
                strided = pl.ds(it * n_seg + j, length, stride=dilation)
                dense = pl.ds(j * length, length)
                qd[dense, :] = qf[strided, :].astype(BF16)
                kd[dense, :] = kf[strided, :].astype(BF16)
                vd[dense, :] = vf[strided, :].astype(BF16)
            band_blocks(g, length, n_seg, qd, kd, vd, od, ld)
            for j in range(n_seg):
                strided = pl.ds(it * n_seg + j, length, stride=dilation)
                dense = pl.ds(j * length, length)
                og[g, strided, :] = od[dense, :]
                lgs[g, strided, :] = ld[dense, :]
            return carry

        lax.fori_loop(0, dilation // n_seg, residues, 0)

    def merge(c, carry):
        r = pl.ds(pl.multiple_of(c * rows, rows), rows)
        l0, l1, l2 = lgs[0, r, :], lgs[1, r, :], lgs[2, r, :]
        m = jnp.maximum(jnp.maximum(l0, l1), l2)
        w0, w1, w2 = jnp.exp(l0 - m), jnp.exp(l1 - m), jnp.exp(l2 - m)
        num = w0 * og[0, r, :] + w1 * og[1, r, :] + w2 * og[2, r, :]
        o_ref[r, :] = (num / (w0 + w1 + w2)).astype(BF16)
        return carry

    lax.fori_loop(0, seq // rows, merge, 0)


def _attention(proj3, bias_tab):
    bsz, seq, _ = proj3.shape
    n_pat = len(DILATION_PATTERNS)
    pairs = ATT_HEADS // 2
    base = 4 * RET_WIDTH // LANES

    def col(section):
        return pl.BlockSpec((None, seq, LANES),
                            lambda b, hp: (b, 0, base + section * (ATT_WIDTH // LANES) + hp))

    return pl.pallas_call(
        _attn_body,
        grid=(bsz, pairs),
        in_specs=[col(0), col(1), col(2),
                  pl.BlockSpec((n_pat, 2, 3, ATT_QBLK, ATT_KWIN), lambda b, hp: (0, hp, 0, 0, 0))],
        out_specs=pl.BlockSpec((None, seq, LANES), lambda b, hp: (b, 0, hp)),
        out_shape=jax.ShapeDtypeStruct((bsz, seq, ATT_WIDTH), BF16),
        scratch_shapes=[pltpu.VMEM((seq, LANES), F32)] * 3
                       + [pltpu.VMEM((seq, LANES), BF16)]
                       + [pltpu.VMEM((seq // 4, LANES), BF16)] * 3
                       + [pltpu.VMEM((seq // 4, LANES), F32)] * 2
                       + [pltpu.VMEM((n_pat, seq, LANES), F32)] * 2,
        compiler_params=_cparams(("arbitrary", "arbitrary"), 56),
        name="attention",
    )(proj3, proj3, proj3, bias_tab)


def _mix_body(ret_ref, att_ref, gr_ref, ga_ref, x_ref, mod_ref, gain_ref,
              wr_ref, wa_ref, wo_ref, wrt_ref, x1_ref, hslab_ref, hrow_ref, logit_ref, slab_scr):
    tm = x_ref.shape[0]
    y_ret = jnp.dot(ret_ref[...], wr_ref[...], preferred_element_type=F32)
    y_att = jnp.dot(att_ref[...], wa_ref[...], preferred_element_type=F32)
    merged = (_sigmoid(gr_ref[...].astype(F32)) * y_ret
              + _sigmoid(ga_ref[...].astype(F32)) * y_att)
    mixed = jnp.dot(merged.astype(BF16), wo_ref[...], preferred_element_type=F32)
    x1 = x_ref[...] + mod_ref[0, 2:3, :] * mixed
    x1_ref[...] = x1
    h = _rms(x1, gain_ref[...]) * (1.0 + mod_ref[0, 4:5, :]) + mod_ref[0, 3:4, :]
    hrow_ref[...] = h.astype(BF16)
    _rows_to_tiles(h, slab_scr, hslab_ref)
    logit_ref[...] = jnp.dot(h, wrt_ref[...], preferred_element_type=F32,
                             precision=lax.Precision.HIGHEST)


def _mix(ret2, att2, proj2, x2, mod3, gain, wr, wa, wo, w_router, seq, tm=512):
    t, d = x2.shape
    gate_base = (4 * RET_WIDTH + 3 * ATT_WIDTH) // d
    row = lambda i: (i, 0)
    const = lambda i: (0, 0)
    return pl.pallas_call(
        _mix_body,
        grid=(t // tm,),
        in_specs=[pl.BlockSpec((tm, d), row), pl.BlockSpec((tm, d), row),
                  pl.BlockSpec((tm, d), lambda i: (i, gate_base)),
                  pl.BlockSpec((tm, d), lambda i: (i, gate_base + 1)),
                  pl.BlockSpec((tm, d), row),
                  pl.BlockSpec((1, 6, d), lambda i: ((i * tm) // seq, 0, 0)),
                  pl.BlockSpec((1, d), const),
                  pl.BlockSpec((d, d), const), pl.BlockSpec((d, d), const),
                  pl.BlockSpec((d, d), const), pl.BlockSpec((d, N_EXPERTS), const)],
        out_specs=[pl.BlockSpec((tm, d), row),
                   pl.BlockSpec((tm, SLABS, LANES), lambda i: (i, 0, 0)),
                   pl.BlockSpec((tm, d), row),
                   pl.BlockSpec((tm, N_EXPERTS), row)],
        out_shape=[jax.ShapeDtypeStruct((t, d), F32),
                   jax.ShapeDtypeStruct((t, SLABS, LANES), BF16),
                   jax.ShapeDtypeStruct((t, d), BF16),
                   jax.ShapeDtypeStruct((t, N_EXPERTS), F32)],
        scratch_shapes=[pltpu.VMEM((tm * SLABS, LANES), F32)],
        compiler_params=_cparams(("arbitrary",), 56),
        name="mix",
    )(ret2, att2, proj2, proj2, x2, mod3, gain, wr, wa, wo, w_router)


def _route_body(logit_ref, bias_ref, eidx_ref, gate_ref, rank_ref, cnt_ref, carry_scr, tri_scr):
    tn = logit_ref.shape[0]
    per_group = N_EXPERTS // N_GROUPS

    @pl.when(pl.program_id(0) == 0)
    def _():
        carry_scr[...] = jnp.zeros_like(carry_scr)
        r = lax.broadcasted_iota(I32, (tn, tn), 0)
        c = lax.broadcasted_iota(I32, (tn, tn), 1)
        tri_scr[...] = jnp.where(r < c, 1.0, 0.0).astype(BF16)

    scores = _sigmoid(logit_ref[...]).T
    choice = scores + jnp.concatenate([bias_ref[...]] * (tn // LANES), axis=1)
    neg_inf = -jnp.inf

    sub = lax.broadcasted_iota(I32, (per_group, tn), 0).astype(F32)
    group_score = []
    for g in range(N_GROUPS):
        cg = choice[g * per_group:(g + 1) * per_group, :]
        m1 = jnp.max(cg, axis=0, keepdims=True)
        first = jnp.min(jnp.where(cg == m1, sub, float(per_group)), axis=0, keepdims=True)
        m2 = jnp.max(jnp.where(sub == first, neg_inf, cg), axis=0, keepdims=True)
        group_score.append(m1 + m2)

    masked = []
    for a in range(N_GROUPS):
        beaten = jnp.zeros((1, tn), F32)
        for b in range(N_GROUPS):
            if b == a:
                continue
            wins = (group_score[b] >= group_score[a]) if b < a else (group_score[b] > group_score[a])
            beaten = beaten + jnp.where(wins, 1.0, 0.0)
        keep = beaten < float(TOPK_GROUPS)
        masked.append(jnp.where(keep, choice[a * per_group:(a + 1) * per_group, :], neg_inf))
    work = jnp.concatenate(masked, axis=0)

    eid = lax.broadcasted_iota(I32, (N_EXPERTS, tn), 0).astype(F32)
    picked = jnp.zeros((N_EXPERTS, tn), F32)
    idx_rows, gate_rows = [], []
    for _ in range(TOP_K):
        m = jnp.max(work, axis=0, keepdims=True)
        idx = jnp.min(jnp.where(work == m, eid, float(N_EXPERTS)), axis=0, keepdims=True)
        sel = eid == idx
        gate_rows.append(jnp.sum(jnp.where(sel, scores, 0.0), axis=0, keepdims=True))
        picked = picked + jnp.where(sel, 1.0, 0.0)
        work = jnp.where(sel, neg_inf, work)
        idx_rows.append(idx)

    before = (jnp.dot(picked.astype(BF16), tri_scr[...], preferred_element_type=F32)
              + carry_scr[:, 0:1])
    rank_rows = [jnp.sum(jnp.where(eid == idx, before, 0.0), axis=0, keepdims=True)
                 for idx in idx_rows]
    carry = carry_scr[...] + jnp.sum(picked, axis=1, keepdims=True)
    carry_scr[...] = carry
    cnt_ref[...] = carry

    gates = jnp.concatenate(gate_rows, axis=0)
    gates = gates / jnp.sum(gates, axis=0, keepdims=True) * ROUTED_SCALE
    eidx_ref[...] = jnp.concatenate(idx_rows, axis=0).astype(I32)
    gate_ref[...] = gates
    rank_ref[...] = jnp.concatenate(rank_rows, axis=0).astype(I32)


def _route(logits, bias_b, tn=512):
    t = logits.shape[0]
    tok = lambda i: (0, i)
    return pl.pallas_call(
        _route_body,
        grid=(t // tn,),
        in_specs=[pl.BlockSpec((tn, N_EXPERTS), lambda i: (i, 0)),
                  pl.BlockSpec((N_EXPERTS, LANES), lambda i: (0, 0))],
        out_specs=[pl.BlockSpec((TOP_K, tn), tok), pl.BlockSpec((TOP_K, tn), tok),
                   pl.BlockSpec((TOP_K, tn), tok),
                   pl.BlockSpec((N_EXPERTS, LANES), lambda i: (0, 0))],
        out_shape=[jax.ShapeDtypeStruct((TOP_K, t), I32), jax.ShapeDtypeStruct((TOP_K, t), F32),
                   jax.ShapeDtypeStruct((TOP_K, t), I32),
                   jax.ShapeDtypeStruct((N_EXPERTS, LANES), F32)],
        scratch_shapes=[pltpu.VMEM((N_EXPERTS, LANES), F32), pltpu.VMEM((tn, tn), BF16)],
        compiler_params=_cparams(("arbitrary",), 48),
        name="route",
    )(logits, bias_b)


def _dest_body(start_ref, eidx_ref, rank_ref, dest_ref):
    e = eidx_ref[...]

    def body(j, acc):
        return jnp.where(e == j, start_ref[j], acc)

    dest_ref[...] = rank_ref[...] + lax.fori_loop(0, N_EXPERTS, body, jnp.zeros_like(e))


def _dest(pad_start, eidx, rank, tn=2048):
    t = eidx.shape[1]
    tok = lambda i, s: (0, i)
    return pl.pallas_call(
        _dest_body,
        grid_spec=pltpu.PrefetchScalarGridSpec(
            num_scalar_prefetch=1,
            grid=(t // tn,),
            in_specs=[pl.BlockSpec((TOP_K, tn), tok), pl.BlockSpec((TOP_K, tn), tok)],
            out_specs=pl.BlockSpec((TOP_K, tn), tok)),
        out_shape=jax.ShapeDtypeStruct((TOP_K, t), I32),
        compiler_params=_cparams(("arbitrary",), 32),
        name="dest",
    )(pad_start, eidx, rank)


def _dispatch_body(dest_hbm, h_ref, xs_hbm, dest_smem, idx_sem, row_sem):
    tt = dest_smem.shape[0] // TOP_K
    idx_copy = pltpu.make_async_copy(dest_hbm.at[pl.program_id(0)], dest_smem, idx_sem)
    idx_copy.start()
    idx_copy.wait()

    def issue(t, carry):
        for k in range(TOP_K):
            pltpu.make_async_copy(h_ref.at[t], xs_hbm.at[dest_smem[t * TOP_K + k]],
                                  row_sem).start(priority=k % 2)
        return carry

    lax.fori_loop(0, tt, issue, 0)

    for k in range(TOP_K):
        pltpu.make_async_copy(h_ref, xs_hbm.at[pl.ds(0, tt)], row_sem).wait()


def _dispatch(dest_tiles, htiles, n_rows):
    n_tiles, width = dest_tiles.shape
    tt = width // TOP_K
    return pl.pallas_call(
        _dispatch_body,
        grid=(n_tiles,),
        in_specs=[pl.BlockSpec(memory_space=pl.ANY),
                  pl.BlockSpec((tt, SLABS, LANES), lambda i: (i, 0, 0))],
        out_specs=pl.BlockSpec(memory_space=pl.ANY),
        out_shape=jax.ShapeDtypeStruct((n_rows, SLABS, LANES), BF16),
        scratch_shapes=[pltpu.SMEM((width,), I32), pltpu.SemaphoreType.DMA, pltpu.SemaphoreType.DMA],
        compiler_params=_cparams(("arbitrary",), 32),
        name="dispatch",
    )(dest_tiles, htiles)


def _experts_body(bexp_ref, bnew_ref, nused_ref, xs_hbm, wg_ref, wu_ref, wd_ref, ys_ref,
                  wgu_s, wd_s, stage_scr, xring, ring_sem):
    i = pl.program_id(0)
    hid = wd_s.shape[0]
    n_used = nused_ref[0]

    def block_copy(j):
        slot = j % EXPERT_RING
        src = xs_hbm.at[pl.ds(pl.multiple_of(j * MOE_BLOCK, MOE_BLOCK), MOE_BLOCK)]
        return pltpu.make_async_copy(src, xring.at[slot], ring_sem.at[slot])

    @pl.when(i == 0)
    def _():
        for j in range(EXPERT_RING - 1):
            @pl.when(j < n_used)
            def _(j=j):
                block_copy(j).start()

    @pl.when(i + EXPERT_RING - 1 < n_used)
    def _():
        block_copy(i + EXPERT_RING - 1).start()

    @pl.when(i < n_used)
    def _():
        @pl.when(bnew_ref[i] == 1)
        def _():
            wgu_s[:, :hid] = wg_ref[...].astype(BF16)
            wgu_s[:, hid:] = wu_ref[...].astype(BF16)
            wd_s[...] = wd_ref[...].astype(BF16)

        block_copy(i).wait()
        x = _tiles_to_rows(xring[i % EXPERT_RING], stage_scr).astype(BF16)
        gu = jnp.dot(x, wgu_s[...], preferred_element_type=F32)
        hg, hu = gu[:, :hid], gu[:, hid:]
        y = jnp.dot((_silu(hg) * hu).astype(BF16), wd_s[...], preferred_element_type=F32)
        _rows_to_tiles(y, stage_scr, ys_ref)


def _experts(block_expert, block_new, n_used, xs, w_gate, w_up, w_down):
    n_blocks = block_expert.shape[0]
    d, hid = w_gate.shape[1], w_gate.shape[2]
    tile_block = (MOE_BLOCK, SLABS, LANES)

    def blk(i, be, bn, nu):
        return (jnp.minimum(i, nu[0] - 1), 0, 0)

    def wsel(i, be, bn, nu):
        return (be[jnp.minimum(i, nu[0] - 1)], 0, 0)

    return pl.pallas_call(
        _experts_body,
        grid_spec=pltpu.PrefetchScalarGridSpec(
            num_scalar_prefetch=3,
            grid=(n_blocks,),
            in_specs=[pl.BlockSpec(memory_space=pl.ANY),
                      pl.BlockSpec((None, d, hid), wsel),
                      pl.BlockSpec((None, d, hid), wsel),
                      pl.BlockSpec((None, hid, d), wsel)],
            out_specs=pl.BlockSpec(tile_block, blk),
            scratch_shapes=[pltpu.VMEM((d, 2 * hid), BF16), pltpu.VMEM((hid, d), BF16),
                            pltpu.VMEM((MOE_BLOCK * SLABS, LANES), F32),
                            pltpu.VMEM((EXPERT_RING,) + tile_block, BF16),
                            pltpu.SemaphoreType.DMA((EXPERT_RING,))]),
        out_shape=jax.ShapeDtypeStruct(xs.shape, BF16),
        compiler_params=_cparams(("arbitrary",), 32),
        name="experts",
    )(block_expert, block_new, n_used, xs, w_gate, w_up, w_down)


def _combine_body(dest_hbm, gate_hbm, ys_hbm, x1_ref, h_ref, wsg_ref, wsu_ref, wsd_ref,
                  mod_ref, gain_ref, o_ref, dest_smem0, dest_smem1, gate_smem, buf, shared_scr,
                  routed_scr, idx_sem, gate_sem, row_sem, *, final_norm):
    tc = x1_ref.shape[0]
    i = pl.program_id(0)
    has_next = i + 1 < pl.num_programs(0)
    cur = i % 2
    dest_smem = (dest_smem0, dest_smem1)

    def index_copy(tile, slot):
        return pltpu.make_async_copy(dest_hbm.at[tile], dest_smem[slot], idx_sem)

    def start_rows(slot):
        def issue(t, carry):
            for k in range(TOP_K):
                pltpu.make_async_copy(ys_hbm.at[dest_smem[slot][t * TOP_K + k]],
                                      buf.at[slot, t * TOP_K + k],
                                      row_sem.at[slot]).start(priority=k % 2)
            return carry

        lax.fori_loop(0, tc, issue, 0)

    gate_copy = pltpu.make_async_copy(gate_hbm.at[i], gate_smem, gate_sem)
    gate_copy.start()

    @pl.when(i == 0)
    def _():
        index_copy(0, 0).start()
        index_copy(0, 0).wait()
        start_rows(0)

    for slot in range(2):
        @pl.when(jnp.logical_and(has_next, cur != slot))
        def _(slot=slot):
            index_copy(i + 1, slot).start()

    h = h_ref[...]
    hid = (_silu(jnp.dot(h, wsg_ref[...], preferred_element_type=F32))
           * jnp.dot(h, wsu_ref[...], preferred_element_type=F32))
    shared_scr[...] = jnp.dot(hid.astype(BF16), wsd_ref[...], preferred_element_type=F32)

    for slot in range(2):
        @pl.when(jnp.logical_and(has_next, cur != slot))
        def _(slot=slot):
            index_copy(i + 1, slot).wait()
            start_rows(slot)

    pltpu.make_async_copy(ys_hbm.at[pl.ds(0, tc * TOP_K)], buf.at[cur], row_sem.at[cur]).wait()
    gate_copy.wait()

    def weigh(g, carry):
        for u in range(COMBINE_UNROLL):
            t = g * COMBINE_UNROLL + u
            acc = gate_smem[t * TOP_K] * buf[cur, t * TOP_K].astype(F32)
            for k in range(1, TOP_K):
                acc = acc + gate_smem[t * TOP_K + k] * buf[cur, t * TOP_K + k].astype(F32)
            routed_scr[pl.ds(pl.multiple_of(t * SLABS, SLABS), SLABS), :] = acc
        return carry

    lax.fori_loop(0, tc // COMBINE_UNROLL, weigh, 0)

    gate_f = mod_ref[0, 5:6, :]
    gain = gain_ref[...]

    def finish(c, carry):
        r0 = pl.multiple_of(c * COMBINE_ROWS, COMBINE_ROWS)
        rows = pl.ds(r0, COMBINE_ROWS)
        routed = jnp.concatenate(
            [routed_scr[pl.ds(r0 * SLABS + s, COMBINE_ROWS, stride=SLABS), :]
             for s in range(SLABS)], axis=1)
        x2 = x1_ref[rows, :] + gate_f * (shared_scr[rows, :] + routed)
        o_ref[rows, :] = _rms(x2, gain) if final_norm else x2
        return carry

    lax.fori_loop(0, tc // COMBINE_ROWS, finish, 0)


def _combine(dest_tiles, gate_tiles, ys, x1, hrow, wsg, wsu, wsd, mod3, gain, seq, tc, final_norm):
    t, d = x1.shape
    hid = wsg.shape[1]
    row = lambda i: (i, 0)
    const = lambda i: (0, 0)
    return pl.pallas_call(
        functools.partial(_combine_body, final_norm=final_norm),
        grid=(t // tc,),
        in_specs=[pl.BlockSpec(memory_space=pl.ANY),
                  pl.BlockSpec(memory_space=pl.ANY),
                  pl.BlockSpec(memory_space=pl.ANY),
                  pl.BlockSpec((tc, d), row), pl.BlockSpec((tc, d), row),
                  pl.BlockSpec((d, hid), const), pl.BlockSpec((d, hid), const),
                  pl.BlockSpec((hid, d), const),
                  pl.BlockSpec((1, 6, d), lambda i: ((i * tc) // seq, 0, 0)),
                  pl.BlockSpec((1, d), const)],
        out_specs=pl.BlockSpec((tc, d), row),
        out_shape=jax.ShapeDtypeStruct((t, d), F32),
        scratch_shapes=[pltpu.SMEM((TOP_K * tc,), I32), pltpu.SMEM((TOP_K * tc,), I32),
                        pltpu.SMEM((TOP_K * tc,), F32),
                        pltpu.VMEM((2, tc * TOP_K, SLABS, LANES), BF16),
                        pltpu.VMEM((tc, d), F32),
                        pltpu.VMEM((tc * SLABS, LANES), F32),
                        pltpu.SemaphoreType.DMA, pltpu.SemaphoreType.DMA,
                        pltpu.SemaphoreType.DMA((2,))],
        compiler_params=_cparams(("arbitrary",), 56),
        name="combine",
    )(dest_tiles, gate_tiles, ys, x1, hrow, wsg, wsu, wsd, mod3, gain)


def _tile_major(a, tile):
    k, t = a.shape
    return a.reshape(k, t // tile, tile).transpose(1, 2, 0).reshape(t // tile, tile * k)


def kernel(x, c, w_ada, b_ada, norm_mix, w_in, ret_decay, t5_bias, w_ret_up, w_att_up, w_o,
           norm_ffn, w_router, router_bias, w_gate, w_up, w_down, ws_gate, ws_up, ws_down, norm_final):
    bsz, seq, d = x.shape
    depth = w_ada.shape[0]
    t = bsz * seq
    assert d == D_MODEL and seq % (ATT_KWIN * DILATION_PATTERNS[-1][1]) == 0

    half = RET_HEAD_DIM // 2
    inv_freq = ROPE_BASE ** (-jnp.arange(half, dtype=F32) / half)
    ang = jnp.arange(seq, dtype=F32)[:, None] * inv_freq[None, :]
    cos, sin = jnp.cos(ang), jnp.sin(ang)
    bias_tab = _attention_bias(t5_bias)

    n_assign = t * TOP_K
    n_blocks = -(-n_assign // MOE_BLOCK) + N_EXPERTS
    n_rows = n_blocks * MOE_BLOCK
    disp_tile = min(1024, t)
    comb_tile = min(256, t)

    x2 = x.reshape(t, d)
    for layer in range(depth):
        mod3 = _adaln(c, w_ada[layer], b_ada[layer]).reshape(bsz, 6, d)
        proj = _inproj(x2, norm_mix[layer].reshape(1, d), mod3, w_in[layer].astype(BF16), seq)
        proj3 = proj.reshape(bsz, seq, PROJ_WIDTH)
        log_gamma = jnp.log1p(-jnp.exp(ret_decay[layer].astype(F32)))
        ret = _retention(proj3, log_gamma, cos, sin)
        att = _attention(proj3, bias_tab)
        x1, hslab, hrow, logits = _mix(
            ret.reshape(t, RET_WIDTH), att.reshape(t, ATT_WIDTH), proj, x2, mod3,
            norm_ffn[layer].reshape(1, d), w_ret_up[layer].astype(BF16),
            w_att_up[layer].astype(BF16), w_o[layer].astype(BF16), w_router[layer], seq)

        bias_b = jnp.broadcast_to(router_bias[layer].astype(F32)[:, None], (N_EXPERTS, LANES))
        eidx, gate, rank, counts = _route(logits, bias_b)

        counts = counts[:, 0].astype(I32)
        padded = (counts + MOE_BLOCK - 1) // MOE_BLOCK * MOE_BLOCK
        pad_end = jnp.cumsum(padded)
        pad_start = pad_end - padded
        block_row = jnp.arange(n_blocks, dtype=I32) * MOE_BLOCK
        block_expert = jnp.minimum(
            jnp.sum((pad_end[None, :] <= block_row[:, None]).astype(I32), axis=1), N_EXPERTS - 1)
        block_new = jnp.concatenate(
            [jnp.ones((1,), I32), (block_expert[1:] != block_expert[:-1]).astype(I32)])
        n_used = (pad_end[-1:] // MOE_BLOCK).astype(I32)

        dest = _dest(pad_start.astype(I32), eidx, rank)
        xs = _dispatch(_tile_major(dest, disp_tile), hslab, n_rows)
        ys = _experts(block_expert, block_new, n_used, xs,
                      w_gate[layer], w_up[layer], w_down[layer])
        x2 = _combine(_tile_major(dest, comb_tile), _tile_major(gate, comb_tile), ys,
                      x1, hrow, ws_gate[layer].astype(BF16), ws_up[layer].astype(BF16),
                      ws_down[layer].astype(BF16), mod3, norm_final.reshape(1, d), seq, comb_tile,
                      final_norm=(layer == depth - 1))
    return x2.reshape(bsz, seq, d)
```

```python
import functools
import math

import jax
import jax.numpy as jnp
import numpy as np
from jax import lax
from jax.experimental import pallas as pl
from jax.experimental.pallas import tpu as pltpu

F32 = jnp.float32
BF16 = jnp.bfloat16
I32 = jnp.int32

D_MODEL = 1024
RET_HEADS = 4
RET_HEAD_DIM = 256
RET_WIDTH = RET_HEADS * RET_HEAD_DIM
RET_CHUNK = 128
ROPE_BASE = 10000.0
ATT_HEADS = 16
ATT_HEAD_DIM = 64
ATT_WIDTH = ATT_HEADS * ATT_HEAD_DIM
DILATION_PATTERNS = ((128, 1), (512, 4), (2048, 16))
REL_BUCKETS = 32
REL_MAX_DISTANCE = 1024
N_EXPERTS = 256
TOP_K = 8
N_GROUPS = 8
TOPK_GROUPS = 4
EXPERT_HIDDEN = 256
ROUTED_SCALE = 2.5
MOE_BLOCK = 256
NORM_EPS = 1e-6
PROJ_WIDTH = 4 * RET_WIDTH + 3 * ATT_WIDTH + 2 * D_MODEL

LANES = 128
SUBLANES = 8
SLABS = D_MODEL // LANES
ATT_RADIUS = 64
ATT_QBLK = 128
ATT_KWIN = 256
MIX_ROWS = 128
RET_UNROLL = 4
EXPERT_RING = 3
COMBINE_ROWS = 32
COMBINE_UNROLL = 8
ATT_UNROLL = 8
NEG_BIG = -1e30
MIB = 1024 * 1024


def _cparams(sem, vmem_mib):
    return pltpu.CompilerParams(dimension_semantics=sem, vmem_limit_bytes=vmem_mib * MIB)


def _sigmoid(x):
    return 1.0 / (1.0 + jnp.exp(-x))


def _silu(x):
    return x * _sigmoid(x)


def _rms(x, gain):
    return x * lax.rsqrt(jnp.mean(x * x, axis=-1, keepdims=True) + NORM_EPS) * gain


def _rows_to_tiles(rows, stage_scr, tiles_ref, first=0):
    n = rows.shape[0]
    base = first * SLABS
    for s in range(SLABS):
        stage_scr[pl.ds(base + s, n, stride=SLABS), :] = rows[:, s * LANES:(s + 1) * LANES]
    staged = stage_scr[pl.ds(base, n * SLABS), :]
    tiles_ref[pl.ds(first, n)] = staged.reshape(n, SLABS, LANES).astype(BF16)


def _tiles_to_rows(tiles, stage_scr):
    n = tiles.shape[0]
    stage_scr[...] = tiles.astype(F32).reshape(n * SLABS, LANES)
    return jnp.concatenate([stage_scr[pl.ds(s, n, stride=SLABS), :] for s in range(SLABS)], axis=1)


def _adaln_body(c_ref, w_ref, b_ref, o_ref):
    cond = _silu(c_ref[...])
    o_ref[...] = jnp.dot(cond, w_ref[...], preferred_element_type=F32,
                         precision=lax.Precision.HIGHEST) + b_ref[...]


def _adaln(c, w, b):
    bsz, d = c.shape
    n = w.shape[1]
    return pl.pallas_call(
        _adaln_body,
        grid=(n // d,),
        in_specs=[pl.BlockSpec((bsz, d), lambda j: (0, 0)),
                  pl.BlockSpec((d, d), lambda j: (0, j)),
                  pl.BlockSpec((1, d), lambda j: (0, j))],
        out_specs=pl.BlockSpec((bsz, d), lambda j: (0, j)),
        out_shape=jax.ShapeDtypeStruct((bsz, n), F32),
        compiler_params=_cparams(("arbitrary",), 32),
        name="adaln",
    )(c, w, b.reshape(1, n))


def _inproj_body(x_ref, gain_ref, mod_ref, w_ref, o_ref, h_scr):
    @pl.when(pl.program_id(1) == 0)
    def _():
        y = _rms(x_ref[...], gain_ref[...])
        h = y * (1.0 + mod_ref[0, 1:2, :]) + mod_ref[0, 0:1, :]
        h_scr[...] = h.astype(BF16)

    o_ref[...] = jnp.dot(h_scr[...], w_ref[...], preferred_element_type=F32).astype(BF16)


def _inproj(x2, gain, mod3, w_bf, seq, tm=1024, tn=3072):
    t, d = x2.shape
    n = w_bf.shape[1]
    return pl.pallas_call(
        _inproj_body,
        grid=(t // tm, n // tn),
        in_specs=[pl.BlockSpec((tm, d), lambda i, j: (i, 0)),
                  pl.BlockSpec((1, d), lambda i, j: (0, 0)),
                  pl.BlockSpec((1, 6, d), lambda i, j: ((i * tm) // seq, 0, 0)),
                  pl.BlockSpec((d, tn), lambda i, j: (0, j))],
        out_specs=pl.BlockSpec((tm, tn), lambda i, j: (i, j)),
        out_shape=jax.ShapeDtypeStruct((t, n), BF16),
        scratch_shapes=[pltpu.VMEM((tm, d), BF16)],
        compiler_params=_cparams(("arbitrary", "arbitrary"), 48),
        name="inproj",
    )(x2, gain, mod3, w_bf)


def _ret_body(lg_ref, q_ref, k_ref, v_ref, g_ref, cos_ref, sin_ref, o_ref,
              qr_scr, kr_scr, accf_scr, accb_scr, stf_scr, stb_scr):
    head = pl.program_id(1)
    seq = q_ref.shape[0]
    n_chunks = seq // RET_CHUNK
    half = RET_HEAD_DIM // 2
    rot_rows = 256

    def rot_step(c, carry):
        r = pl.ds(pl.multiple_of(c * rot_rows, rot_rows), rot_rows)
        cs = cos_ref[r, :]
        sn = sin_ref[r, :]
        for src, dst, scale in ((q_ref, qr_scr, 1.0), (k_ref, kr_scr, RET_HEAD_DIM ** -0.5)):
            t = src[r, :].astype(F32)
            t1, t2 = t[:, :half], t[:, half:]
            dst[r, :half] = ((t1 * cs - t2 * sn) * scale).astype(BF16)
            dst[r, half:] = ((t1 * sn + t2 * cs) * scale).astype(BF16)
        return carry

    lax.fori_loop(0, seq // rot_rows, rot_step, 0)

    ri = lax.broadcasted_iota(I32, (RET_CHUNK, RET_CHUNK), 0)
    ci = lax.broadcasted_iota(I32, (RET_CHUNK, RET_CHUNK), 1)
    rowpos = lax.broadcasted_iota(I32, (RET_CHUNK, RET_HEAD_DIM), 0).astype(F32)

    def decay_tables(lg, forward):
        if forward:
            diff = (ri - ci).astype(F32)
            allowed = ri >= ci
            q_scale = jnp.exp(lg * (rowpos + 1.0))
            k_scale = jnp.exp(lg * (RET_CHUNK - 1.0 - rowpos))
        else:
            diff = (ci - ri).astype(F32)
            allowed = ci > ri
            q_scale = jnp.exp(lg * (RET_CHUNK - rowpos))
            k_scale = jnp.exp(lg * rowpos)
        intra = jnp.where(allowed, jnp.exp(lg * jnp.where(allowed, diff, 0.0)), 0.0)
        chunk_decay = jnp.exp(lg * jnp.full((1, RET_HEAD_DIM), float(RET_CHUNK), F32))
        return intra, q_scale, k_scale, chunk_decay

    def chunk_update(c, tables, st_scr, acc_scr):
        intra, q_scale, k_scale, chunk_decay = tables
        r = pl.ds(pl.multiple_of(c * RET_CHUNK, RET_CHUNK), RET_CHUNK)
        q = qr_scr[r, :]
        k = kr_scr[r, :]
        v = v_ref[r, :]
        scores = lax.dot_general(q, k, (((1,), (1,)), ((), ())),
                                 preferred_element_type=F32) * intra
        state = st_scr[...]
        acc_scr[r, :] = (jnp.dot(scores.astype(BF16), v, preferred_element_type=F32)
                         + jnp.dot((q.astype(F32) * q_scale).astype(BF16), state.astype(BF16),
                                   preferred_element_type=F32))
        k_t = (k.astype(F32) * k_scale).T.astype(BF16)
        st_scr[...] = state * chunk_decay + jnp.dot(k_t, v, preferred_element_type=F32)

    fwd_tables = decay_tables(lg_ref[0, head], True)
    bwd_tables = decay_tables(lg_ref[1, head], False)
    stf_scr[...] = jnp.zeros_like(stf_scr)
    stb_scr[...] = jnp.zeros_like(stb_scr)

    def step(trip, carry):
        for u in range(RET_UNROLL):
            i = trip * RET_UNROLL + u
            chunk_update(i, fwd_tables, stf_scr, accf_scr)
            chunk_update(n_chunks - 1 - i, bwd_tables, stb_scr, accb_scr)
        return carry

    lax.fori_loop(0, n_chunks // RET_UNROLL, step, 0)

    def finish(c, carry):
        r = pl.ds(pl.multiple_of(c * rot_rows, rot_rows), rot_rows)
        o = accf_scr[r, :] + accb_scr[r, :]
        mu = jnp.mean(o, axis=-1, keepdims=True)
        oc = o - mu
        var = jnp.mean(oc * oc, axis=-1, keepdims=True)
        o = oc * lax.rsqrt(var + NORM_EPS)
        o_ref[r, :] = (o * _silu(g_ref[r, :].astype(F32))).astype(BF16)
        return carry

    lax.fori_loop(0, seq // rot_rows, finish, 0)


def _retention(proj3, log_gamma, cos, sin):
    bsz, seq, _ = proj3.shape
    hd = RET_HEAD_DIM

    def col(section):
        return pl.BlockSpec((None, seq, hd), lambda b, h, lg: (b, 0, section * RET_HEADS + h))

    return pl.pallas_call(
        _ret_body,
        grid_spec=pltpu.PrefetchScalarGridSpec(
            num_scalar_prefetch=1,
            grid=(bsz, RET_HEADS),
            in_specs=[col(0), col(1), col(2), col(3),
                      pl.BlockSpec((seq, hd // 2), lambda b, h, lg: (0, 0)),
                      pl.BlockSpec((seq, hd // 2), lambda b, h, lg: (0, 0))],
            out_specs=pl.BlockSpec((None, seq, hd), lambda b, h, lg: (b, 0, h)),
            scratch_shapes=[pltpu.VMEM((seq, hd), BF16), pltpu.VMEM((seq, hd), BF16),
                            pltpu.VMEM((seq, hd), F32), pltpu.VMEM((seq, hd), F32),
                            pltpu.VMEM((hd, hd), F32), pltpu.VMEM((hd, hd), F32)]),
        out_shape=jax.ShapeDtypeStruct((bsz, seq, RET_WIDTH), BF16),
        compiler_params=_cparams(("arbitrary", "arbitrary"), 56),
        name="retention",
    )(log_gamma, proj3, proj3, proj3, proj3, cos, sin)


def _t5_bucket(rel):
    half = REL_BUCKETS // 2
    max_exact = half // 2
    n = jnp.abs(rel)
    large = max_exact + (jnp.log(jnp.maximum(n, 1).astype(F32) / max_exact)
                         / math.log(REL_MAX_DISTANCE / max_exact) * (half - max_exact)).astype(I32)
    large = jnp.minimum(large, half - 1)
    return jnp.where(rel > 0, half, 0) + jnp.where(n < max_exact, n, large)


def _band_buckets():
    qi = jnp.arange(ATT_QBLK, dtype=I32)[:, None]
    kj = jnp.arange(ATT_KWIN, dtype=I32)[None, :]
    tables = []
    for _, dilation in DILATION_PATTERNS:
        cases = []
        for offset in (0, -ATT_RADIUS, ATT_QBLK - ATT_KWIN):
            rel = kj + offset - qi
            cases.append(jnp.where(jnp.abs(rel) <= ATT_RADIUS, _t5_bucket(rel * dilation), -1))
        tables.append(jnp.stack(cases, axis=0))
    return jnp.stack(tables, axis=0)


def _bias_body(t5_ref, bucket_ref, o_ref):
    bucket = bucket_ref[...]

    def head(h, carry):
        acc = jnp.full(bucket.shape, NEG_BIG, F32)
        for b in range(REL_BUCKETS):
            acc = jnp.where(bucket == b, t5_ref[b, h], acc)
        o_ref[h] = acc
        return carry

    lax.fori_loop(0, ATT_HEADS, head, 0)


def _attention_bias(t5_bias):
    buckets = _band_buckets()
    n_pat, n_case = buckets.shape[:2]
    return pl.pallas_call(
        _bias_body,
        grid_spec=pltpu.PrefetchScalarGridSpec(
            num_scalar_prefetch=1,
            grid=(n_pat, n_case),
            in_specs=[pl.BlockSpec((None, None, ATT_QBLK, ATT_KWIN), lambda p, c, t5: (p, c, 0, 0))],
            out_specs=pl.BlockSpec((None, ATT_HEADS, None, ATT_QBLK, ATT_KWIN),
                                   lambda p, c, t5: (p, 0, c, 0, 0))),
        out_shape=jax.ShapeDtypeStruct((n_pat, ATT_HEADS, n_case, ATT_QBLK, ATT_KWIN), F32),
        compiler_params=_cparams(("arbitrary", "arbitrary"), 32),
        name="attn_bias",
    )(t5_bias.astype(F32), buckets)


def _attn_body(q_ref, k_ref, v_ref, bias_ref, o_ref,
               qf, kf, vf, qb, qd, kd, vd, od, ld, og, lgs):
    seq = q_ref.shape[0]
    rows = 256
    lane = lax.broadcasted_iota(I32, (ATT_QBLK, LANES), 1)
    head0 = lane < ATT_HEAD_DIM

    def to_f32(c, carry):
        r = pl.ds(pl.multiple_of(c * rows, rows), rows)
        q = q_ref[r, :].astype(F32) * (ATT_HEAD_DIM ** -0.5)
        qf[r, :] = q
        qb[r, :] = q.astype(BF16)
        kf[r, :] = k_ref[r, :].astype(F32)
        vf[r, :] = v_ref[r, :].astype(F32)
        return carry

    lax.fori_loop(0, seq // rows, to_f32, 0)

    def band_blocks(g, length, n_seg, q_src, k_src, v_src, o_dst, l_dst):
        n_qb = length // ATT_QBLK
        n_blocks = n_seg * n_qb
        unroll = min(ATT_UNROLL, n_blocks)

        def qgroup(it, carry):
            for u in range(unroll):
                qblock(it * unroll + u)
            return carry

        def qblock(b):
            qi = b % n_qb
            base = (b // n_qb) * length
            qs = pl.multiple_of(base + qi * ATT_QBLK, ATT_QBLK)
            ws = pl.multiple_of(
                base + jnp.clip(qi * ATT_QBLK - ATT_RADIUS, 0, length - ATT_KWIN), ATT_RADIUS)
            case = jnp.where(qi == 0, 0, jnp.where(qi == n_qb - 1, 2, 1))
            q = q_src[pl.ds(qs, ATT_QBLK), :]
            k = k_src[pl.ds(ws, ATT_KWIN), :]
            v = v_src[pl.ds(ws, ATT_KWIN), :]
            outs, lses = [], []
            for hh in range(2):
                mask = head0 if hh == 0 else jnp.logical_not(head0)
                qm = jnp.where(mask, q, jnp.zeros_like(q))
                s = lax.dot_general(qm, k, (((1,), (1,)), ((), ())), preferred_element_type=F32)
                s = s + bias_ref[g, hh, case]
                m = jnp.max(s, axis=-1, keepdims=True)
                p = jnp.exp(s - m)
                l = jnp.sum(p, axis=-1, keepdims=True)
                outs.append(jnp.dot(p.astype(BF16), v, preferred_element_type=F32) / l)
                lses.append(m + jnp.log(l))
            o_dst[pl.ds(qs, ATT_QBLK), :] = jnp.where(head0, outs[0], outs[1])
            l_dst[pl.ds(qs, ATT_QBLK), :] = jnp.where(head0, lses[0], lses[1])

        lax.fori_loop(0, n_blocks // unroll, qgroup, 0)

    for g, (_, dilation) in enumerate(DILATION_PATTERNS):
        length = seq // dilation
        if dilation == 1:
            band_blocks(g, length, 1, qb, k_ref, v_ref, og.at[g], lgs.at[g])
            continue
        n_seg = max(1, min(dilation, ATT_UNROLL * ATT_QBLK // length))

        def residues(it, carry, g=g, dilation=dilation, length=length, n_seg=n_seg):
            for j in range(n_seg):
                strided = pl.ds(it * n_seg + j, length, stride=dilation)
                dense = pl.ds(j * length, length)
                qd[dense, :] = qf[strided, :].astype(BF16)
                kd[dense, :] = kf[strided, :].astype(BF16)
                vd[dense, :] = vf[strided, :].astype(BF16)
            band_blocks(g, length, n_seg, qd, kd, vd, od, ld)
            for j in range(n_seg):
                strided = pl.ds(it * n_seg + j, length, stride=dilation)
                dense = pl.ds(j * length, length)
                og[g, strided, :] = od[dense, :]
                lgs[g, strided, :] = ld[dense, :]
            return carry

        lax.fori_loop(0, dilation // n_seg, residues, 0)

    def merge(c, carry):
        r = pl.ds(pl.multiple_of(c * rows, rows), rows)
        l0, l1, l2 = lgs[0, r, :], lgs[1, r, :], lgs[2, r, :]
        m = jnp.maximum(jnp.maximum(l0, l1), l2)
        w0, w1, w2 = jnp.exp(l0 - m), jnp.exp(l1 - m), jnp.exp(l2 - m)
        num = w0 * og[0, r, :] + w1 * og[1, r, :] + w2 * og[2, r, :]
        o_ref[r, :] = (num / (w0 + w1 + w2)).astype(BF16)
        return carry

    lax.fori_loop(0, seq // rows, merge, 0)


def _attention(proj3, bias_tab):
    bsz, seq, _ = proj3.shape
    n_pat = len(DILATION_PATTERNS)
    pairs = ATT_HEADS // 2
    base = 4 * RET_WIDTH // LANES

    def col(section):
        return pl.BlockSpec((None, seq, LANES),
                            lambda b, hp: (b, 0, base + section * (ATT_WIDTH // LANES) + hp))

    return pl.pallas_call(
        _attn_body,
        grid=(bsz, pairs),
        in_specs=[col(0), col(1), col(2),
                  pl.BlockSpec((n_pat, 2, 3, ATT_QBLK, ATT_KWIN), lambda b, hp: (0, hp, 0, 0, 0))],
        out_specs=pl.BlockSpec((None, seq, LANES), lambda b, hp: (b, 0, hp)),
        out_shape=jax.ShapeDtypeStruct((bsz, seq, ATT_WIDTH), BF16),
        scratch_shapes=[pltpu.VMEM((seq, LANES), F32)] * 3
                       + [pltpu.VMEM((seq, LANES), BF16)]
                       + [pltpu.VMEM((seq // 4, LANES), BF16)] * 3
                       + [pltpu.VMEM((seq // 4, LANES), F32)] * 2
                       + [pltpu.VMEM((n_pat, seq, LANES), F32)] * 2,
        compiler_params=_cparams(("arbitrary", "arbitrary"), 56),
        name="attention",
    )(proj3, proj3, proj3, bias_tab)


def _mix_body(ret_ref, att_ref, gr_ref, ga_ref, x_ref, mod_ref, gain_ref,
              wr_ref, wa_ref, wo_ref, wrt_ref, x1_ref, hslab_ref, hrow_ref, logit_ref, slab_scr):
    tm = x_ref.shape[0]
    shift_f, scale_f, gate_m = mod_ref[0, 3:4, :], mod_ref[0, 4:5, :], mod_ref[0, 2:3, :]
    for r in range(tm // MIX_ROWS):
        rows = pl.ds(r * MIX_ROWS, MIX_ROWS)
        y_ret = jnp.dot(ret_ref[rows, :], wr_ref[...], preferred_element_type=F32)
        y_att = jnp.dot(att_ref[rows, :], wa_ref[...], preferred_element_type=F32)
        merged = (_sigmoid(gr_ref[rows, :].astype(F32)) * y_ret
                  + _sigmoid(ga_ref[rows, :].astype(F32)) * y_att)
        mixed = jnp.dot(merged.astype(BF16), wo_ref[...], preferred_element_type=F32)
        x1 = x_ref[rows, :] + gate_m * mixed
        x1_ref[rows, :] = x1
        h = _rms(x1, gain_ref[...]) * (1.0 + scale_f) + shift_f
        hrow_ref[rows, :] = h.astype(BF16)
        _rows_to_tiles(h, slab_scr, hslab_ref, first=r * MIX_ROWS)
        logit_ref[rows, :] = jnp.dot(h, wrt_ref[...], preferred_element_type=F32,
                                     precision=lax.Precision.HIGHEST)


def _mix(ret2, att2, proj2, x2, mod3, gain, wr, wa, wo, w_router, seq, tm=512):
    t, d = x2.shape
    gate_base = (4 * RET_WIDTH + 3 * ATT_WIDTH) // d
    row = lambda i: (i, 0)
    const = lambda i: (0, 0)
    return pl.pallas_call(
        _mix_body,
        grid=(t // tm,),
        in_specs=[pl.BlockSpec((tm, d), row), pl.BlockSpec((tm, d), row),
                  pl.BlockSpec((tm, d), lambda i: (i, gate_base)),
                  pl.BlockSpec((tm, d), lambda i: (i, gate_base + 1)),
                  pl.BlockSpec((tm, d), row),
                  pl.BlockSpec((1, 6, d), lambda i: ((i * tm) // seq, 0, 0)),
                  pl.BlockSpec((1, d), const),
                  pl.BlockSpec((d, d), const), pl.BlockSpec((d, d), const),
                  pl.BlockSpec((d, d), const), pl.BlockSpec((d, N_EXPERTS), const)],
        out_specs=[pl.BlockSpec((tm, d), row),
                   pl.BlockSpec((tm, SLABS, LANES), lambda i: (i, 0, 0)),
                   pl.BlockSpec((tm, d), row),
                   pl.BlockSpec((tm, N_EXPERTS), row)],
        out_shape=[jax.ShapeDtypeStruct((t, d), F32),
                   jax.ShapeDtypeStruct((t, SLABS, LANES), BF16),
                   jax.ShapeDtypeStruct((t, d), BF16),
                   jax.ShapeDtypeStruct((t, N_EXPERTS), F32)],
        scratch_shapes=[pltpu.VMEM((tm * SLABS, LANES), F32)],
        compiler_params=_cparams(("arbitrary",), 56),
        name="mix",
    )(ret2, att2, proj2, proj2, x2, mod3, gain, wr, wa, wo, w_router)


def _route_body(logit_ref, bias_ref, eidx_ref, gate_ref, rank_ref, cnt_ref, carry_scr, tri_scr):
    tn = logit_ref.shape[0]
    per_group = N_EXPERTS // N_GROUPS

    @pl.when(pl.program_id(0) == 0)
    def _():
        carry_scr[...] = jnp.zeros_like(carry_scr)
        r = lax.broadcasted_iota(I32, (tn, tn), 0)
        c = lax.broadcasted_iota(I32, (tn, tn), 1)
        tri_scr[...] = jnp.where(r < c, 1.0, 0.0).astype(BF16)

    scores = _sigmoid(logit_ref[...]).T
    choice = scores + jnp.concatenate([bias_ref[...]] * (tn // LANES), axis=1)
    neg_inf = -jnp.inf

    sub = lax.broadcasted_iota(I32, (per_group, tn), 0).astype(F32)
    group_score = []
    for g in range(N_GROUPS):
        cg = choice[g * per_group:(g + 1) * per_group, :]
        m1 = jnp.max(cg, axis=0, keepdims=True)
        first = jnp.min(jnp.where(cg == m1, sub, float(per_group)), axis=0, keepdims=True)
        m2 = jnp.max(jnp.where(sub == first, neg_inf, cg), axis=0, keepdims=True)
        group_score.append(m1 + m2)

    masked = []
    for a in range(N_GROUPS):
        beaten = jnp.zeros((1, tn), F32)
        for b in range(N_GROUPS):
            if b == a:
                continue
            wins = (group_score[b] >= group_score[a]) if b < a else (group_score[b] > group_score[a])
            beaten = beaten + jnp.where(wins, 1.0, 0.0)
        keep = beaten < float(TOPK_GROUPS)
        masked.append(jnp.where(keep, choice[a * per_group:(a + 1) * per_group, :], neg_inf))
    work = jnp.concatenate(masked, axis=0)

    eid = lax.broadcasted_iota(I32, (N_EXPERTS, tn), 0).astype(F32)
    picked = jnp.zeros((N_EXPERTS, tn), F32)
    idx_rows, gate_rows = [], []
    for _ in range(TOP_K):
        m = jnp.max(work, axis=0, keepdims=True)
        idx = jnp.min(jnp.where(work == m, eid, float(N_EXPERTS)), axis=0, keepdims=True)
        sel = eid == idx
        gate_rows.append(jnp.sum(jnp.where(sel, scores, 0.0), axis=0, keepdims=True))
        picked = picked + jnp.where(sel, 1.0, 0.0)
        work = jnp.where(sel, neg_inf, work)
        idx_rows.append(idx)

    before = (jnp.dot(picked.astype(BF16), tri_scr[...], preferred_element_type=F32)
              + carry_scr[:, 0:1])
    rank_rows = [jnp.sum(jnp.where(eid == idx, before, 0.0), axis=0, keepdims=True)
                 for idx in idx_rows]
    carry = carry_scr[...] + jnp.sum(picked, axis=1, keepdims=True)
    carry_scr[...] = carry
    cnt_ref[...] = carry

    gates = jnp.concatenate(gate_rows, axis=0)
    gates = gates / jnp.sum(gates, axis=0, keepdims=True) * ROUTED_SCALE
    eidx_ref[...] = jnp.concatenate(idx_rows, axis=0).astype(I32)
    gate_ref[...] = gates
    rank_ref[...] = jnp.concatenate(rank_rows, axis=0).astype(I32)


def _route(logits, bias_b, tn=512):
    t = logits.shape[0]
    tok = lambda i: (0, i)
    return pl.pallas_call(
        _route_body,
        grid=(t // tn,),
        in_specs=[pl.BlockSpec((tn, N_EXPERTS), lambda i: (i, 0)),
                  pl.BlockSpec((N_EXPERTS, LANES), lambda i: (0, 0))],
        out_specs=[pl.BlockSpec((TOP_K, tn), tok), pl.BlockSpec((TOP_K, tn), tok),
                   pl.BlockSpec((TOP_K, tn), tok),
                   pl.BlockSpec((N_EXPERTS, LANES), lambda i: (0, 0))],
        out_shape=[jax.ShapeDtypeStruct((TOP_K, t), I32), jax.ShapeDtypeStruct((TOP_K, t), F32),
                   jax.ShapeDtypeStruct((TOP_K, t), I32),
                   jax.ShapeDtypeStruct((N_EXPERTS, LANES), F32)],
        scratch_shapes=[pltpu.VMEM((N_EXPERTS, LANES), F32), pltpu.VMEM((tn, tn), BF16)],
        compiler_params=_cparams(("arbitrary",), 48),
        name="route",
    )(logits, bias_b)


def _dest_body(start_ref, eidx_ref, rank_ref, dest_ref):
    e = eidx_ref[...]

    def body(j, acc):
        return jnp.where(e == j, start_ref[j], acc)

    dest_ref[...] = rank_ref[...] + lax.fori_loop(0, N_EXPERTS, body, jnp.zeros_like(e))


def _dest(pad_start, eidx, rank, tn=2048):
    t = eidx.shape[1]
    tok = lambda i, s: (0, i)
    return pl.pallas_call(
        _dest_body,
        grid_spec=pltpu.PrefetchScalarGridSpec(
            num_scalar_prefetch=1,
            grid=(t // tn,),
            in_specs=[pl.BlockSpec((TOP_K, tn), tok), pl.BlockSpec((TOP_K, tn), tok)],
            out_specs=pl.BlockSpec((TOP_K, tn), tok)),
        out_shape=jax.ShapeDtypeStruct((TOP_K, t), I32),
        compiler_params=_cparams(("arbitrary",), 32),
        name="dest",
    )(pad_start, eidx, rank)


def _dispatch_body(dest_hbm, h_ref, xs_hbm, dest_smem, idx_sem, row_sem):
    tt = dest_smem.shape[0] // TOP_K
    idx_copy = pltpu.make_async_copy(dest_hbm.at[pl.program_id(0)], dest_smem, idx_sem)
    idx_copy.start()
    idx_copy.wait()

    def issue(t, carry):
        for k in range(TOP_K):
            pltpu.make_async_copy(h_ref.at[t], xs_hbm.at[dest_smem[t * TOP_K + k]],
                                  row_sem).start(priority=k % 2)
        return carry

    lax.fori_loop(0, tt, issue, 0)

    for k in range(TOP_K):
        pltpu.make_async_copy(h_ref, xs_hbm.at[pl.ds(0, tt)], row_sem).wait()


def _dispatch(dest_tiles, htiles, n_rows):
    n_tiles, width = dest_tiles.shape
    tt = width // TOP_K
    return pl.pallas_call(
        _dispatch_body,
        grid=(n_tiles,),
        in_specs=[pl.BlockSpec(memory_space=pl.ANY),
                  pl.BlockSpec((tt, SLABS, LANES), lambda i: (i, 0, 0))],
        out_specs=pl.BlockSpec(memory_space=pl.ANY),
        out_shape=jax.ShapeDtypeStruct((n_rows, SLABS, LANES), BF16),
        scratch_shapes=[pltpu.SMEM((width,), I32), pltpu.SemaphoreType.DMA, pltpu.SemaphoreType.DMA],
        compiler_params=_cparams(("arbitrary",), 32),
        name="dispatch",
    )(dest_tiles, htiles)


def _experts_body(bexp_ref, bnew_ref, nused_ref, bnext_ref, bslot_ref, xs_hbm, wg_hbm, wu_hbm, wd_hbm,
                  ys_ref, wgu_s, wd_s, stage_scr, xring, wg_buf, wu_buf, wd_buf, ring_sem, w_sem):
    i = pl.program_id(0)
    hid = wd_s.shape[0]
    n_used = nused_ref[0]

    def block_copy(j):
        slot = j % EXPERT_RING
        src = xs_hbm.at[pl.ds(pl.multiple_of(j * MOE_BLOCK, MOE_BLOCK), MOE_BLOCK)]
        return pltpu.make_async_copy(src, xring.at[slot], ring_sem.at[slot])

    def weight_copies(expert, slot):
        return [pltpu.make_async_copy(src.at[expert], dst.at[slot], w_sem.at[slot])
                for src, dst in ((wg_hbm, wg_buf), (wu_hbm, wu_buf), (wd_hbm, wd_buf))]

    @pl.when(i == 0)
    def _():
        for copy in weight_copies(bexp_ref[0], 0):
            copy.start()
        for j in range(EXPERT_RING - 1):
            @pl.when(j < n_used)
            def _(j=j):
                block_copy(j).start()

    @pl.when(i + EXPERT_RING - 1 < n_used)
    def _():
        block_copy(i + EXPERT_RING - 1).start()

    @pl.when(i < n_used)
    def _():
        @pl.when(bnew_ref[i] == 1)
        def _():
            slot = bslot_ref[i]
            for copy in weight_copies(bexp_ref[i], slot):
                copy.wait()
            nxt = bnext_ref[i]

            @pl.when(nxt >= 0)
            def _():
                for copy in weight_copies(nxt, 1 - slot):
                    copy.start()

            wgu_s[:, :hid] = wg_buf[slot].astype(BF16)
            wgu_s[:, hid:] = wu_buf[slot].astype(BF16)
            wd_s[...] = wd_buf[slot].astype(BF16)

        block_copy(i).wait()
        x = _tiles_to_rows(xring[i % EXPERT_RING], stage_scr).astype(BF16)
        gu = jnp.dot(x, wgu_s[...], preferred_element_type=F32)
        hg, hu = gu[:, :hid], gu[:, hid:]
        y = jnp.dot((_silu(hg) * hu).astype(BF16), wd_s[...], preferred_element_type=F32)
        _rows_to_tiles(y, stage_scr, ys_ref)


def _experts(block_expert, block_new, n_used, block_next, block_slot, xs, w_gate, w_up, w_down):
    n_blocks = block_expert.shape[0]
    d, hid = w_gate.shape[1], w_gate.shape[2]
    tile_block = (MOE_BLOCK, SLABS, LANES)
    hbm = pl.BlockSpec(memory_space=pl.ANY)

    def blk(i, be, bn, nu, bx, bs):
        return (jnp.minimum(i, nu[0] - 1), 0, 0)

    return pl.pallas_call(
        _experts_body,
        grid_spec=pltpu.PrefetchScalarGridSpec(
            num_scalar_prefetch=5,
            grid=(n_blocks,),
            in_specs=[hbm, hbm, hbm, hbm],
            out_specs=pl.BlockSpec(tile_block, blk),
            scratch_shapes=[pltpu.VMEM((d, 2 * hid), BF16), pltpu.VMEM((hid, d), BF16),
                            pltpu.VMEM((MOE_BLOCK * SLABS, LANES), F32),
                            pltpu.VMEM((EXPERT_RING,) + tile_block, BF16),
                            pltpu.VMEM((2, d, hid), F32), pltpu.VMEM((2, d, hid), F32),
                            pltpu.VMEM((2, hid, d), F32),
                            pltpu.SemaphoreType.DMA((EXPERT_RING,)),
                            pltpu.SemaphoreType.DMA((2,))]),
        out_shape=jax.ShapeDtypeStruct(xs.shape, BF16),
        compiler_params=_cparams(("arbitrary",), 32),
        name="experts",
    )(block_expert, block_new, n_used, block_next, block_slot, xs, w_gate, w_up, w_down)


def _combine_body(dest_hbm, gate_hbm, ys_hbm, x1_ref, h_ref, wsg_ref, wsu_ref, wsd_ref,
                  mod_ref, gain_ref, o_ref, dest_smem0, dest_smem1, gate_smem, buf, shared_scr,
                  routed_scr, idx_sem, gate_sem, row_sem, *, final_norm):
    tc = x1_ref.shape[0]
    i = pl.program_id(0)
    has_next = i + 1 < pl.num_programs(0)
    cur = i % 2
    dest_smem = (dest_smem0, dest_smem1)

    def index_copy(tile, slot):
        return pltpu.make_async_copy(dest_hbm.at[tile], dest_smem[slot], idx_sem)

    def start_rows(slot):
        def issue(t, carry):
            for k in range(TOP_K):
                pltpu.make_async_copy(ys_hbm.at[dest_smem[slot][t * TOP_K + k]],
                                      buf.at[slot, t * TOP_K + k],
                                      row_sem.at[slot]).start(priority=k % 2)
            return carry

        lax.fori_loop(0, tc, issue, 0)

    gate_copy = pltpu.make_async_copy(gate_hbm.at[i], gate_smem, gate_sem)
    gate_copy.start()

    @pl.when(i == 0)
    def _():
        index_copy(0, 0).start()
        index_copy(0, 0).wait()
        start_rows(0)

    for slot in range(2):
        @pl.when(jnp.logical_and(has_next, cur != slot))
        def _(slot=slot):
            index_copy(i + 1, slot).start()

    h = h_ref[...]
    hid = (_silu(jnp.dot(h, wsg_ref[...], preferred_element_type=F32))
           * jnp.dot(h, wsu_ref[...], preferred_element_type=F32))
    shared_scr[...] = jnp.dot(hid.astype(BF16), wsd_ref[...], preferred_element_type=F32)

    for slot in range(2):
        @pl.when(jnp.logical_and(has_next, cur != slot))
        def _(slot=slot):
            index_copy(i + 1, slot).wait()
            start_rows(slot)

    pltpu.make_async_copy(ys_hbm.at[pl.ds(0, tc * TOP_K)], buf.at[cur], row_sem.at[cur]).wait()
    gate_copy.wait()

    def weigh(g, carry):
        for u in range(COMBINE_UNROLL):
            t = g * COMBINE_UNROLL + u
            acc = gate_smem[t * TOP_K] * buf[cur, t * TOP_K].astype(F32)
            for k in range(1, TOP_K):
                acc = acc + gate_smem[t * TOP_K + k] * buf[cur, t * TOP_K + k].astype(F32)
            routed_scr[pl.ds(pl.multiple_of(t * SLABS, SLABS), SLABS), :] = acc
        return carry

    lax.fori_loop(0, tc // COMBINE_UNROLL, weigh, 0)

    gate_f = mod_ref[0, 5:6, :]
    gain = gain_ref[...]

    def finish(c, carry):
        r0 = pl.multiple_of(c * COMBINE_ROWS, COMBINE_ROWS)
        rows = pl.ds(r0, COMBINE_ROWS)
        routed = jnp.concatenate(
            [routed_scr[pl.ds(r0 * SLABS + s, COMBINE_ROWS, stride=SLABS), :]
             for s in range(SLABS)], axis=1)
        x2 = x1_ref[rows, :] + gate_f * (shared_scr[rows, :] + routed)
        o_ref[rows, :] = _rms(x2, gain) if final_norm else x2
        return carry

    lax.fori_loop(0, tc // COMBINE_ROWS, finish, 0)


def _combine(dest_tiles, gate_tiles, ys, x1, hrow, wsg, wsu, wsd, mod3, gain, seq, tc, final_norm):
    t, d = x1.shape
    hid = wsg.shape[1]
    row = lambda i: (i, 0)
    const = lambda i: (0, 0)
    return pl.pallas_call(
        functools.partial(_combine_body, final_norm=final_norm),
        grid=(t // tc,),
        in_specs=[pl.BlockSpec(memory_space=pl.ANY),
                  pl.BlockSpec(memory_space=pl.ANY),
                  pl.BlockSpec(memory_space=pl.ANY),
                  pl.BlockSpec((tc, d), row), pl.BlockSpec((tc, d), row),
                  pl.BlockSpec((d, hid), const), pl.BlockSpec((d, hid), const),
                  pl.BlockSpec((hid, d), const),
                  pl.BlockSpec((1, 6, d), lambda i: ((i * tc) // seq, 0, 0)),
                  pl.BlockSpec((1, d), const)],
        out_specs=pl.BlockSpec((tc, d), row),
        out_shape=jax.ShapeDtypeStruct((t, d), F32),
        scratch_shapes=[pltpu.SMEM((TOP_K * tc,), I32), pltpu.SMEM((TOP_K * tc,), I32),
                        pltpu.SMEM((TOP_K * tc,), F32),
                        pltpu.VMEM((2, tc * TOP_K, SLABS, LANES), BF16),
                        pltpu.VMEM((tc, d), F32),
                        pltpu.VMEM((tc * SLABS, LANES), F32),
                        pltpu.SemaphoreType.DMA, pltpu.SemaphoreType.DMA,
                        pltpu.SemaphoreType.DMA((2,))],
        compiler_params=_cparams(("arbitrary",), 56),
        name="combine",
    )(dest_tiles, gate_tiles, ys, x1, hrow, wsg, wsu, wsd, mod3, gain)


def _tile_major(a, tile):
    k, t = a.shape
    return a.reshape(k, t // tile, tile).transpose(1, 2, 0).reshape(t // tile, tile * k)


def kernel(x, c, w_ada, b_ada, norm_mix, w_in, ret_decay, t5_bias, w_ret_up, w_att_up, w_o,
           norm_ffn, w_router, router_bias, w_gate, w_up, w_down, ws_gate, ws_up, ws_down, norm_final):
    bsz, seq, d = x.shape
    depth = w_ada.shape[0]
    t = bsz * seq
    assert d == D_MODEL and seq % (ATT_KWIN * DILATION_PATTERNS[-1][1]) == 0

    half = RET_HEAD_DIM // 2
    inv_freq = ROPE_BASE ** (-jnp.arange(half, dtype=F32) / half)
    ang = jnp.arange(seq, dtype=F32)[:, None] * inv_freq[None, :]
    cos, sin = jnp.cos(ang), jnp.sin(ang)
    bias_tab = _attention_bias(t5_bias)

    n_assign = t * TOP_K
    n_blocks = -(-n_assign // MOE_BLOCK) + N_EXPERTS
    n_rows = n_blocks * MOE_BLOCK
    disp_tile = min(1024, t)
    comb_tile = min(256, t)

    x2 = x.reshape(t, d)
    for layer in range(depth):
        mod3 = _adaln(c, w_ada[layer], b_ada[layer]).reshape(bsz, 6, d)
        proj = _inproj(x2, norm_mix[layer].reshape(1, d), mod3, w_in[layer].astype(BF16), seq)
        proj3 = proj.reshape(bsz, seq, PROJ_WIDTH)
        log_gamma = jnp.log1p(-jnp.exp(ret_decay[layer].astype(F32)))
        ret = _retention(proj3, log_gamma, cos, sin)
        att = _attention(proj3, bias_tab)
        x1, hslab, hrow, logits = _mix(
            ret.reshape(t, RET_WIDTH), att.reshape(t, ATT_WIDTH), proj, x2, mod3,
            norm_ffn[layer].reshape(1, d), w_ret_up[layer].astype(BF16),
            w_att_up[layer].astype(BF16), w_o[layer].astype(BF16), w_router[layer], seq)

        bias_b = jnp.broadcast_to(router_bias[layer].astype(F32)[:, None], (N_EXPERTS, LANES))
        eidx, gate, rank, counts = _route(logits, bias_b)

        counts = counts[:, 0].astype(I32)
        padded = (counts + MOE_BLOCK - 1) // MOE_BLOCK * MOE_BLOCK
        pad_end = jnp.cumsum(padded)
        pad_start = pad_end - padded
        block_row = jnp.arange(n_blocks, dtype=I32) * MOE_BLOCK
        block_expert = jnp.minimum(
            jnp.sum((pad_end[None, :] <= block_row[:, None]).astype(I32), axis=1), N_EXPERTS - 1)
        block_new = jnp.concatenate(
            [jnp.ones((1,), I32), (block_expert[1:] != block_expert[:-1]).astype(I32)])
        n_used = (pad_end[-1:] // MOE_BLOCK).astype(I32)

        dest = _dest(pad_start.astype(I32), eidx, rank)
        xs = _dispatch(_tile_major(dest, disp_tile), hslab, n_rows)
        expert_ids = jnp.arange(N_EXPERTS, dtype=I32)
        later = jnp.logical_and(expert_ids[None, :] > block_expert[:, None], (padded > 0)[None, :])
        block_next = jnp.min(jnp.where(later, expert_ids[None, :], N_EXPERTS), axis=1)
        block_next = jnp.where(block_next == N_EXPERTS, -1, block_next).astype(I32)
        block_slot = ((jnp.cumsum(block_new) - 1) % 2).astype(I32)
        ys = _experts(block_expert, block_new, n_used, block_next, block_slot, xs,
                      w_gate[layer], w_up[layer], w_down[layer])
        x2 = _combine(_tile_major(dest, comb_tile), _tile_major(gate, comb_tile), ys,
                      x1, hrow, ws_gate[layer].astype(BF16), ws_up[layer].astype(BF16),
                      ws_down[layer].astype(BF16), mod3, norm_final.reshape(1, d), seq, comb_tile,
                      final_norm=(layer == depth - 1))
    return x2.reshape(bsz, seq, d)
```

```python
import functools
import math

import jax
import jax.numpy as jnp
import numpy as np
from jax import lax
from jax.experimental import pallas as pl
from jax.experimental.pallas import tpu as pltpu

F32 = jnp.float32
BF16 = jnp.bfloat16
I32 = jnp.int32

D_MODEL = 1024
RET_HEADS = 4
RET_HEAD_DIM = 256
RET_WIDTH = RET_HEADS * RET_HEAD_DIM
RET_CHUNK = 128
ROPE_BASE = 10000.0
ATT_HEADS = 16
ATT_HEAD_DIM = 64
ATT_WIDTH = ATT_HEADS * ATT_HEAD_DIM
DILATION_PATTERNS = ((128, 1), (512, 4), (2048, 16))
REL_BUCKETS = 32
REL_MAX_DISTANCE = 1024
N_EXPERTS = 256
TOP_K = 8
N_GROUPS = 8
TOPK_GROUPS = 4
EXPERT_HIDDEN = 256
ROUTED_SCALE = 2.5
MOE_BLOCK = 256
NORM_EPS = 1e-6
PROJ_WIDTH = 4 * RET_WIDTH + 3 * ATT_WIDTH + 2 * D_MODEL

LANES = 128
SUBLANES = 8
SLABS = D_MODEL // LANES
ATT_RADIUS = 64
ATT_QBLK = 128
ATT_KWIN = 256
MIX_ROWS = 128
RET_UNROLL = 4
EXPERT_RING = 3
COMBINE_ROWS = 32
COMBINE_UNROLL = 8
ATT_COARSE = 4
ATT_UNROLL = 8
NEG_BIG = -1e30
MIB = 1024 * 1024


def _cparams(sem, vmem_mib):
    return pltpu.CompilerParams(dimension_semantics=sem, vmem_limit_bytes=vmem_mib * MIB)


def _sigmoid(x):
    return 1.0 / (1.0 + jnp.exp(-x))


def _silu(x):
    return x * _sigmoid(x)


def _rms(x, gain):
    return x * lax.rsqrt(jnp.mean(x * x, axis=-1, keepdims=True) + NORM_EPS) * gain


def _rows_to_tiles(rows, stage_scr, tiles_ref, first=0):
    n = rows.shape[0]
    base = first * SLABS
    for s in range(SLABS):
        stage_scr[pl.ds(base + s, n, stride=SLABS), :] = rows[:, s * LANES:(s + 1) * LANES]
    staged = stage_scr[pl.ds(base, n * SLABS), :]
    tiles_ref[pl.ds(first, n)] = staged.reshape(n, SLABS, LANES).astype(BF16)


def _tiles_to_rows(tiles, stage_scr):
    n = tiles.shape[0]
    stage_scr[...] = tiles.astype(F32).reshape(n * SLABS, LANES)
    return jnp.concatenate([stage_scr[pl.ds(s, n, stride=SLABS), :] for s in range(SLABS)], axis=1)


def _adaln_body(c_ref, w_ref, b_ref, o_ref):
    cond = _silu(c_ref[...])
    o_ref[...] = jnp.dot(cond, w_ref[...], preferred_element_type=F32,
                         precision=lax.Precision.HIGHEST) + b_ref[...]


def _adaln(c, w, b):
    bsz, d = c.shape
    n = w.shape[1]
    return pl.pallas_call(
        _adaln_body,
        grid=(n // d,),
        in_specs=[pl.BlockSpec((bsz, d), lambda j: (0, 0)),
                  pl.BlockSpec((d, d), lambda j: (0, j)),
                  pl.BlockSpec((1, d), lambda j: (0, j))],
        out_specs=pl.BlockSpec((bsz, d), lambda j: (0, j)),
        out_shape=jax.ShapeDtypeStruct((bsz, n), F32),
        compiler_params=_cparams(("arbitrary",), 32),
        name="adaln",
    )(c, w, b.reshape(1, n))


def _inproj_body(x_ref, gain_ref, mod_ref, w_ref, o_ref, h_scr):
    @pl.when(pl.program_id(1) == 0)
    def _():
        y = _rms(x_ref[...], gain_ref[...])
        h = y * (1.0 + mod_ref[0, 1:2, :]) + mod_ref[0, 0:1, :]
        h_scr[...] = h.astype(BF16)

    o_ref[...] = jnp.dot(h_scr[...], w_ref[...], preferred_element_type=F32).astype(BF16)


def _inproj(x2, gain, mod3, w_bf, seq, tm=1024, tn=3072):
    t, d = x2.shape
    n = w_bf.shape[1]
    return pl.pallas_call(
        _inproj_body,
        grid=(t // tm, n // tn),
        in_specs=[pl.BlockSpec((tm, d), lambda i, j: (i, 0)),
                  pl.BlockSpec((1, d), lambda i, j: (0, 0)),
                  pl.BlockSpec((1, 6, d), lambda i, j: ((i * tm) // seq, 0, 0)),
                  pl.BlockSpec((d, tn), lambda i, j: (0, j))],
        out_specs=pl.BlockSpec((tm, tn), lambda i, j: (i, j)),
        out_shape=jax.ShapeDtypeStruct((t, n), BF16),
        scratch_shapes=[pltpu.VMEM((tm, d), BF16)],
        compiler_params=_cparams(("arbitrary", "arbitrary"), 48),
        name="inproj",
    )(x2, gain, mod3, w_bf)


def _ret_body(lg_ref, q_ref, k_ref, v_ref, g_ref, cos_ref, sin_ref, o_ref,
              qr_scr, kr_scr, accf_scr, accb_scr, stf_scr, stb_scr):
    head = pl.program_id(1)
    seq = q_ref.shape[0]
    n_chunks = seq // RET_CHUNK
    half = RET_HEAD_DIM // 2
    rot_rows = 256

    def rot_step(c, carry):
        r = pl.ds(pl.multiple_of(c * rot_rows, rot_rows), rot_rows)
        cs = cos_ref[r, :]
        sn = sin_ref[r, :]
        for src, dst, scale in ((q_ref, qr_scr, 1.0), (k_ref, kr_scr, RET_HEAD_DIM ** -0.5)):
            t = src[r, :].astype(F32)
            t1, t2 = t[:, :half], t[:, half:]
            dst[r, :half] = ((t1 * cs - t2 * sn) * scale).astype(BF16)
            dst[r, half:] = ((t1 * sn + t2 * cs) * scale).astype(BF16)
        return carry

    lax.fori_loop(0, seq // rot_rows, rot_step, 0)

    ri = lax.broadcasted_iota(I32, (RET_CHUNK, RET_CHUNK), 0)
    ci = lax.broadcasted_iota(I32, (RET_CHUNK, RET_CHUNK), 1)
    rowpos = lax.broadcasted_iota(I32, (RET_CHUNK, RET_HEAD_DIM), 0).astype(F32)

    def decay_tables(lg, forward):
        if forward:
            diff = (ri - ci).astype(F32)
            allowed = ri >= ci
            q_scale = jnp.exp(lg * (rowpos + 1.0))
            k_scale = jnp.exp(lg * (RET_CHUNK - 1.0 - rowpos))
        else:
            diff = (ci - ri).astype(F32)
            allowed = ci > ri
            q_scale = jnp.exp(lg * (RET_CHUNK - rowpos))
            k_scale = jnp.exp(lg * rowpos)
        intra = jnp.where(allowed, jnp.exp(lg * jnp.where(allowed, diff, 0.0)), 0.0)
        chunk_decay = jnp.exp(lg * jnp.full((1, RET_HEAD_DIM), float(RET_CHUNK), F32))
        return intra, q_scale, k_scale, chunk_decay

    def chunk_update(c, tables, st_scr, acc_scr):
        intra, q_scale, k_scale, chunk_decay = tables
        r = pl.ds(pl.multiple_of(c * RET_CHUNK, RET_CHUNK), RET_CHUNK)
        q = qr_scr[r, :]
        k = kr_scr[r, :]
        v = v_ref[r, :]
        scores = lax.dot_general(q, k, (((1,), (1,)), ((), ())),
                                 preferred_element_type=F32) * intra
        state = st_scr[...]
        acc_scr[r, :] = (jnp.dot(scores.astype(BF16), v, preferred_element_type=F32)
                         + jnp.dot((q.astype(F32) * q_scale).astype(BF16), state.astype(BF16),
                                   preferred_element_type=F32))
        k_t = (k.astype(F32) * k_scale).T.astype(BF16)
        st_scr[...] = state * chunk_decay + jnp.dot(k_t, v, preferred_element_type=F32)

    fwd_tables = decay_tables(lg_ref[0, head], True)
    bwd_tables = decay_tables(lg_ref[1, head], False)
    stf_scr[...] = jnp.zeros_like(stf_scr)
    stb_scr[...] = jnp.zeros_like(stb_scr)

    def step(trip, carry):
        for u in range(RET_UNROLL):
            i = trip * RET_UNROLL + u
            chunk_update(i, fwd_tables, stf_scr, accf_scr)
            chunk_update(n_chunks - 1 - i, bwd_tables, stb_scr, accb_scr)
        return carry

    lax.fori_loop(0, n_chunks // RET_UNROLL, step, 0)

    def finish(c, carry):
        r = pl.ds(pl.multiple_of(c * rot_rows, rot_rows), rot_rows)
        o = accf_scr[r, :] + accb_scr[r, :]
        mu = jnp.mean(o, axis=-1, keepdims=True)
        oc = o - mu
        var = jnp.mean(oc * oc, axis=-1, keepdims=True)
        o = oc * lax.rsqrt(var + NORM_EPS)
        o_ref[r, :] = (o * _silu(g_ref[r, :].astype(F32))).astype(BF16)
        return carry

    lax.fori_loop(0, seq // rot_rows, finish, 0)


def _retention(proj3, log_gamma, cos, sin):
    bsz, seq, _ = proj3.shape
    hd = RET_HEAD_DIM

    def col(section):
        return pl.BlockSpec((None, seq, hd), lambda b, h, lg: (b, 0, section * RET_HEADS + h))

    return pl.pallas_call(
        _ret_body,
        grid_spec=pltpu.PrefetchScalarGridSpec(
            num_scalar_prefetch=1,
            grid=(bsz, RET_HEADS),
            in_specs=[col(0), col(1), col(2), col(3),
                      pl.BlockSpec((seq, hd // 2), lambda b, h, lg: (0, 0)),
                      pl.BlockSpec((seq, hd // 2), lambda b, h, lg: (0, 0))],
            out_specs=pl.BlockSpec((None, seq, hd), lambda b, h, lg: (b, 0, h)),
            scratch_shapes=[pltpu.VMEM((seq, hd), BF16), pltpu.VMEM((seq, hd), BF16),
                            pltpu.VMEM((seq, hd), F32), pltpu.VMEM((seq, hd), F32),
                            pltpu.VMEM((hd, hd), F32), pltpu.VMEM((hd, hd), F32)]),
        out_shape=jax.ShapeDtypeStruct((bsz, seq, RET_WIDTH), BF16),
        compiler_params=_cparams(("arbitrary", "arbitrary"), 56),
        name="retention",
    )(log_gamma, proj3, proj3, proj3, proj3, cos, sin)


def _t5_bucket(rel):
    half = REL_BUCKETS // 2
    max_exact = half // 2
    n = jnp.abs(rel)
    large = max_exact + (jnp.log(jnp.maximum(n, 1).astype(F32) / max_exact)
                         / math.log(REL_MAX_DISTANCE / max_exact) * (half - max_exact)).astype(I32)
    large = jnp.minimum(large, half - 1)
    return jnp.where(rel > 0, half, 0) + jnp.where(n < max_exact, n, large)


def _band_buckets():
    qi = jnp.arange(ATT_QBLK, dtype=I32)[:, None]
    kj = jnp.arange(ATT_KWIN, dtype=I32)[None, :]
    tables = []
    for _, dilation in DILATION_PATTERNS:
        cases = []
        for offset in (0, -ATT_RADIUS, ATT_QBLK - ATT_KWIN):
            rel = kj + offset - qi
            cases.append(jnp.where(jnp.abs(rel) <= ATT_RADIUS, _t5_bucket(rel * dilation), -1))
        tables.append(jnp.stack(cases, axis=0))
    return jnp.stack(tables, axis=0)


def _bias_body(t5_ref, bucket_ref, o_ref):
    bucket = bucket_ref[...]

    def head(h, carry):
        acc = jnp.full(bucket.shape, NEG_BIG, F32)
        for b in range(REL_BUCKETS):
            acc = jnp.where(bucket == b, t5_ref[b, h], acc)
        o_ref[h] = acc
        return carry

    lax.fori_loop(0, ATT_HEADS, head, 0)


def _attention_bias(t5_bias):
    buckets = _band_buckets()
    n_pat, n_case = buckets.shape[:2]
    return pl.pallas_call(
        _bias_body,
        grid_spec=pltpu.PrefetchScalarGridSpec(
            num_scalar_prefetch=1,
            grid=(n_pat, n_case),
            in_specs=[pl.BlockSpec((None, None, ATT_QBLK, ATT_KWIN), lambda p, c, t5: (p, c, 0, 0))],
            out_specs=pl.BlockSpec((None, ATT_HEADS, None, ATT_QBLK, ATT_KWIN),
                                   lambda p, c, t5: (p, 0, c, 0, 0))),
        out_shape=jax.ShapeDtypeStruct((n_pat, ATT_HEADS, n_case, ATT_QBLK, ATT_KWIN), F32),
        compiler_params=_cparams(("arbitrary", "arbitrary"), 32),
        name="attn_bias",
    )(t5_bias.astype(F32), buckets)


def _attn_body(q_ref, k_ref, v_ref, bias_ref, o_ref,
               qf, kf, vf, qb, q4, k4, v4, qd, kd, vd, od, ld, o4, l4, og, lgs):
    seq = q_ref.shape[0]
    rows = 256
    lane = lax.broadcasted_iota(I32, (ATT_QBLK, LANES), 1)
    head0 = lane < ATT_HEAD_DIM

    def to_f32(c, carry):
        r = pl.ds(pl.multiple_of(c * rows, rows), rows)
        q = q_ref[r, :].astype(F32) * (ATT_HEAD_DIM ** -0.5)
        qf[r, :] = q
        qb[r, :] = q.astype(BF16)
        kf[r, :] = k_ref[r, :].astype(F32)
        vf[r, :] = v_ref[r, :].astype(F32)
        return carry

    lax.fori_loop(0, seq // rows, to_f32, 0)

    def band_blocks(g, length, n_seg, q_src, k_src, v_src, o_dst, l_dst):
        n_qb = length // ATT_QBLK
        n_blocks = n_seg * n_qb
        unroll = min(ATT_UNROLL, n_blocks)

        def qgroup(it, carry):
            for u in range(unroll):
                qblock(it * unroll + u)
            return carry

        def qblock(b):
            qi = b % n_qb
            base = (b // n_qb) * length
            qs = pl.multiple_of(base + qi * ATT_QBLK, ATT_QBLK)
            ws = pl.multiple_of(
                base + jnp.clip(qi * ATT_QBLK - ATT_RADIUS, 0, length - ATT_KWIN), ATT_RADIUS)
            case = jnp.where(qi == 0, 0, jnp.where(qi == n_qb - 1, 2, 1))
            q = q_src[pl.ds(qs, ATT_QBLK), :]
            k = k_src[pl.ds(ws, ATT_KWIN), :]
            v = v_src[pl.ds(ws, ATT_KWIN), :]
            v_ones = jnp.concatenate([v, jnp.ones_like(v)], axis=1)
            outs, lses = [], []
            for hh in range(2):
                mask = head0 if hh == 0 else jnp.logical_not(head0)
                qm = jnp.where(mask, q, jnp.zeros_like(q))
                s = lax.dot_general(qm, k, (((1,), (1,)), ((), ())), preferred_element_type=F32)
                s = s + bias_ref[g, hh, case]
                m = jnp.max(s, axis=-1, keepdims=True)
                p = jnp.exp(s - m).astype(BF16)
                ol = jnp.dot(p, v_ones, preferred_element_type=F32)
                l = ol[:, LANES:]
                outs.append(ol[:, :LANES] / l)
                lses.append(m + jnp.log(l))
            o_dst[pl.ds(qs, ATT_QBLK), :] = jnp.where(head0, outs[0], outs[1])
            l_dst[pl.ds(qs, ATT_QBLK), :] = jnp.where(head0, lses[0], lses[1])

        lax.fori_loop(0, n_blocks // unroll, qgroup, 0)

    coarse_len = seq // ATT_COARSE

    def split(c, carry):
        strided = pl.ds(c, coarse_len, stride=ATT_COARSE)
        q4[c] = qf[strided, :]
        k4[c] = kf[strided, :]
        v4[c] = vf[strided, :]
        return carry

    lax.fori_loop(0, ATT_COARSE, split, 0)

    for g, (_, dilation) in enumerate(DILATION_PATTERNS):
        length = seq // dilation
        if dilation == 1:
            band_blocks(g, length, 1, qb, k_ref, v_ref, og.at[g], lgs.at[g])
            continue
        assert dilation % ATT_COARSE == 0
        fine = dilation // ATT_COARSE
        assert fine * length == coarse_len and fine * (length // ATT_QBLK) >= min(ATT_UNROLL, fine)

        def coarse_class(c, carry, g=g, length=length, fine=fine):
            segs = [(pl.ds(j, length, stride=fine) if fine > 1 else pl.ds(0, length),
                     pl.ds(j * length, length)) for j in range(fine)]
            for src, dense in segs:
                qd[dense, :] = q4[c, src, :].astype(BF16)
                kd[dense, :] = k4[c, src, :].astype(BF16)
                vd[dense, :] = v4[c, src, :].astype(BF16)
            band_blocks(g, length, fine, qd, kd, vd, od, ld)
            whole = pl.ds(c, coarse_len, stride=ATT_COARSE)
            if fine == 1:
                og[g, whole, :] = od[...]
                lgs[g, whole, :] = ld[...]
            else:
                for src, dense in segs:
                    o4[src, :] = od[dense, :]
                    l4[src, :] = ld[dense, :]
                og[g, whole, :] = o4[...]
                lgs[g, whole, :] = l4[...]
            return carry

        lax.fori_loop(0, ATT_COARSE, coarse_class, 0)

    def merge(c, carry):
        r = pl.ds(pl.multiple_of(c * rows, rows), rows)
        l0, l1, l2 = lgs[0, r, :], lgs[1, r, :], lgs[2, r, :]
        m = jnp.maximum(jnp.maximum(l0, l1), l2)
        w0, w1, w2 = jnp.exp(l0 - m), jnp.exp(l1 - m), jnp.exp(l2 - m)
        num = w0 * og[0, r, :] + w1 * og[1, r, :] + w2 * og[2, r, :]
        o_ref[r, :] = (num / (w0 + w1 + w2)).astype(BF16)
        return carry

    lax.fori_loop(0, seq // rows, merge, 0)


def _attention(proj3, bias_tab):
    bsz, seq, _ = proj3.shape
    n_pat = len(DILATION_PATTERNS)
    pairs = ATT_HEADS // 2
    base = 4 * RET_WIDTH // LANES

    def col(section):
        return pl.BlockSpec((None, seq, LANES),
                            lambda b, hp: (b, 0, base + section * (ATT_WIDTH // LANES) + hp))

    return pl.pallas_call(
        _attn_body,
        grid=(bsz, pairs),
        in_specs=[col(0), col(1), col(2),
                  pl.BlockSpec((n_pat, 2, 3, ATT_QBLK, ATT_KWIN), lambda b, hp: (0, hp, 0, 0, 0))],
        out_specs=pl.BlockSpec((None, seq, LANES), lambda b, hp: (b, 0, hp)),
        out_shape=jax.ShapeDtypeStruct((bsz, seq, ATT_WIDTH), BF16),
        scratch_shapes=[pltpu.VMEM((seq, LANES), F32)] * 3
                       + [pltpu.VMEM((seq, LANES), BF16)]
                       + [pltpu.VMEM((ATT_COARSE, seq // ATT_COARSE, LANES), F32)] * 3
                       + [pltpu.VMEM((seq // ATT_COARSE, LANES), BF16)] * 3
                       + [pltpu.VMEM((seq // ATT_COARSE, LANES), F32)] * 4
                       + [pltpu.VMEM((n_pat, seq, LANES), F32)] * 2,
        compiler_params=_cparams(("arbitrary", "arbitrary"), 56),
        name="attention",
    )(proj3, proj3, proj3, bias_tab)


def _mix_body(ret_ref, att_ref, gr_ref, ga_ref, x_ref, mod_ref, gain_ref,
              wr_ref, wa_ref, wo_ref, wrt_ref, x1_ref, hslab_ref, hrow_ref, logit_ref, slab_scr):
    tm = x_ref.shape[0]
    shift_f, scale_f, gate_m = mod_ref[0, 3:4, :], mod_ref[0, 4:5, :], mod_ref[0, 2:3, :]
    for r in range(tm // MIX_ROWS):
        rows = pl.ds(r * MIX_ROWS, MIX_ROWS)
        y_ret = jnp.dot(ret_ref[rows, :], wr_ref[...], preferred_element_type=F32)
        y_att = jnp.dot(att_ref[rows, :], wa_ref[...], preferred_element_type=F32)
        merged = (_sigmoid(gr_ref[rows, :].astype(F32)) * y_ret
                  + _sigmoid(ga_ref[rows, :].astype(F32)) * y_att)
        mixed = jnp.dot(merged.astype(BF16), wo_ref[...], preferred_element_type=F32)
        x1 = x_ref[rows, :] + gate_m * mixed
        x1_ref[rows, :] = x1
        h = _rms(x1, gain_ref[...]) * (1.0 + scale_f) + shift_f
        hrow_ref[rows, :] = h.astype(BF16)
        _rows_to_tiles(h, slab_scr, hslab_ref, first=r * MIX_ROWS)
        logit_ref[rows, :] = jnp.dot(h, wrt_ref[...], preferred_element_type=F32,
                                     precision=lax.Precision.HIGHEST)


def _mix(ret2, att2, proj2, x2, mod3, gain, wr, wa, wo, w_router, seq, tm=512):
    t, d = x2.shape
    gate_base = (4 * RET_WIDTH + 3 * ATT_WIDTH) // d
    row = lambda i: (i, 0)
    const = lambda i: (0, 0)
    return pl.pallas_call(
        _mix_body,
        grid=(t // tm,),
        in_specs=[pl.BlockSpec((tm, d), row), pl.BlockSpec((tm, d), row),
                  pl.BlockSpec((tm, d), lambda i: (i, gate_base)),
                  pl.BlockSpec((tm, d), lambda i: (i, gate_base + 1)),
                  pl.BlockSpec((tm, d), row),
                  pl.BlockSpec((1, 6, d), lambda i: ((i * tm) // seq, 0, 0)),
                  pl.BlockSpec((1, d), const),
                  pl.BlockSpec((d, d), const), pl.BlockSpec((d, d), const),
                  pl.BlockSpec((d, d), const), pl.BlockSpec((d, N_EXPERTS), const)],
        out_specs=[pl.BlockSpec((tm, d), row),
                   pl.BlockSpec((tm, SLABS, LANES), lambda i: (i, 0, 0)),
                   pl.BlockSpec((tm, d), row),
                   pl.BlockSpec((tm, N_EXPERTS), row)],
        out_shape=[jax.ShapeDtypeStruct((t, d), F32),
                   jax.ShapeDtypeStruct((t, SLABS, LANES), BF16),
                   jax.ShapeDtypeStruct((t, d), BF16),
                   jax.ShapeDtypeStruct((t, N_EXPERTS), F32)],
        scratch_shapes=[pltpu.VMEM((tm * SLABS, LANES), F32)],
        compiler_params=_cparams(("arbitrary",), 56),
        name="mix",
    )(ret2, att2, proj2, proj2, x2, mod3, gain, wr, wa, wo, w_router)


def _route_body(logit_ref, bias_ref, eidx_ref, gate_ref, rank_ref, cnt_ref, carry_scr, tri_scr):
    tn = logit_ref.shape[0]
    per_group = N_EXPERTS // N_GROUPS

    @pl.when(pl.program_id(0) == 0)
    def _():
        carry_scr[...] = jnp.zeros_like(carry_scr)
        r = lax.broadcasted_iota(I32, (tn, tn), 0)
        c = lax.broadcasted_iota(I32, (tn, tn), 1)
        tri_scr[...] = jnp.where(r < c, 1.0, 0.0).astype(BF16)

    scores = _sigmoid(logit_ref[...]).T
    choice = scores + jnp.concatenate([bias_ref[...]] * (tn // LANES), axis=1)
    neg_inf = -jnp.inf

    sub = lax.broadcasted_iota(I32, (per_group, tn), 0).astype(F32)
    group_score = []
    for g in range(N_GROUPS):
        cg = choice[g * per_group:(g + 1) * per_group, :]
        m1 = jnp.max(cg, axis=0, keepdims=True)
        first = jnp.min(jnp.where(cg == m1, sub, float(per_group)), axis=0, keepdims=True)
        m2 = jnp.max(jnp.where(sub == first, neg_inf, cg), axis=0, keepdims=True)
        group_score.append(m1 + m2)

    masked = []
    for a in range(N_GROUPS):
        beaten = jnp.zeros((1, tn), F32)
        for b in range(N_GROUPS):
            if b == a:
                continue
            wins = (group_score[b] >= group_score[a]) if b < a else (group_score[b] > group_score[a])
            beaten = beaten + jnp.where(wins, 1.0, 0.0)
        keep = beaten < float(TOPK_GROUPS)
        masked.append(jnp.where(keep, choice[a * per_group:(a + 1) * per_group, :], neg_inf))
    work = jnp.concatenate(masked, axis=0)

    eid = lax.broadcasted_iota(I32, (N_EXPERTS, tn), 0).astype(F32)
    picked = jnp.zeros((N_EXPERTS, tn), F32)
    idx_rows, gate_rows = [], []
    for _ in range(TOP_K):
        m = jnp.max(work, axis=0, keepdims=True)
        idx = jnp.min(jnp.where(work == m, eid, float(N_EXPERTS)), axis=0, keepdims=True)
        sel = eid == idx
        gate_rows.append(jnp.sum(jnp.where(sel, scores, 0.0), axis=0, keepdims=True))
        picked = picked + jnp.where(sel, 1.0, 0.0)
        work = jnp.where(sel, neg_inf, work)
        idx_rows.append(idx)

    before = (jnp.dot(picked.astype(BF16), tri_scr[...], preferred_element_type=F32)
              + carry_scr[:, 0:1])
    rank_rows = [jnp.sum(jnp.where(eid == idx, before, 0.0), axis=0, keepdims=True)
                 for idx in idx_rows]
    carry = carry_scr[...] + jnp.sum(picked, axis=1, keepdims=True)
    carry_scr[...] = carry
    cnt_ref[...] = carry

    gates = jnp.concatenate(gate_rows, axis=0)
    gates = gates / jnp.sum(gates, axis=0, keepdims=True) * ROUTED_SCALE
    eidx_ref[...] = jnp.concatenate(idx_rows, axis=0).astype(I32)
    gate_ref[...] = gates
    rank_ref[...] = jnp.concatenate(rank_rows, axis=0).astype(I32)


def _route(logits, bias_b, tn=512):
    t = logits.shape[0]
    tok = lambda i: (0, i)
    return pl.pallas_call(
        _route_body,
        grid=(t // tn,),
        in_specs=[pl.BlockSpec((tn, N_EXPERTS), lambda i: (i, 0)),
                  pl.BlockSpec((N_EXPERTS, LANES), lambda i: (0, 0))],
        out_specs=[pl.BlockSpec((TOP_K, tn), tok), pl.BlockSpec((TOP_K, tn), tok),
                   pl.BlockSpec((TOP_K, tn), tok),
                   pl.BlockSpec((N_EXPERTS, LANES), lambda i: (0, 0))],
        out_shape=[jax.ShapeDtypeStruct((TOP_K, t), I32), jax.ShapeDtypeStruct((TOP_K, t), F32),
                   jax.ShapeDtypeStruct((TOP_K, t), I32),
                   jax.ShapeDtypeStruct((N_EXPERTS, LANES), F32)],
        scratch_shapes=[pltpu.VMEM((N_EXPERTS, LANES), F32), pltpu.VMEM((tn, tn), BF16)],
        compiler_params=_cparams(("arbitrary",), 48),
        name="route",
    )(logits, bias_b)


def _dest_body(start_ref, eidx_ref, rank_ref, dest_ref):
    e = eidx_ref[...]

    def body(j, acc):
        return jnp.where(e == j, start_ref[j], acc)

    dest_ref[...] = rank_ref[...] + lax.fori_loop(0, N_EXPERTS, body, jnp.zeros_like(e))


def _dest(pad_start, eidx, rank, tn=2048):
    t = eidx.shape[1]
    tok = lambda i, s: (0, i)
    return pl.pallas_call(
        _dest_body,
        grid_spec=pltpu.PrefetchScalarGridSpec(
            num_scalar_prefetch=1,
            grid=(t // tn,),
            in_specs=[pl.BlockSpec((TOP_K, tn), tok), pl.BlockSpec((TOP_K, tn), tok)],
            out_specs=pl.BlockSpec((TOP_K, tn), tok)),
        out_shape=jax.ShapeDtypeStruct((TOP_K, t), I32),
        compiler_params=_cparams(("arbitrary",), 32),
        name="dest",
    )(pad_start, eidx, rank)


def _dispatch_body(dest_hbm, h_ref, xs_hbm, dest_smem, idx_sem, row_sem):
    tt = dest_smem.shape[0] // TOP_K
    idx_copy = pltpu.make_async_copy(dest_hbm.at[pl.program_id(0)], dest_smem, idx_sem)
    idx_copy.start()
    idx_copy.wait()

    def issue(t, carry):
        for k in range(TOP_K):
            pltpu.make_async_copy(h_ref.at[t], xs_hbm.at[dest_smem[t * TOP_K + k]],
                                  row_sem).start(priority=k % 2)
        return carry

    lax.fori_loop(0, tt, issue, 0)

    for k in range(TOP_K):
        pltpu.make_async_copy(h_ref, xs_hbm.at[pl.ds(0, tt)], row_sem).wait()


def _dispatch(dest_tiles, htiles, n_rows):
    n_tiles, width = dest_tiles.shape
    tt = width // TOP_K
    return pl.pallas_call(
        _dispatch_body,
        grid=(n_tiles,),
        in_specs=[pl.BlockSpec(memory_space=pl.ANY),
                  pl.BlockSpec((tt, SLABS, LANES), lambda i: (i, 0, 0))],
        out_specs=pl.BlockSpec(memory_space=pl.ANY),
        out_shape=jax.ShapeDtypeStruct((n_rows, SLABS, LANES), BF16),
        scratch_shapes=[pltpu.SMEM((width,), I32), pltpu.SemaphoreType.DMA, pltpu.SemaphoreType.DMA],
        compiler_params=_cparams(("arbitrary",), 32),
        name="dispatch",
    )(dest_tiles, htiles)


def _experts_body(bexp_ref, bnew_ref, nused_ref, bnext_ref, bslot_ref, xs_hbm, wg_hbm, wu_hbm, wd_hbm,
                  ys_ref, wgu_s, wd_s, stage_scr, xring, wg_buf, wu_buf, wd_buf, ring_sem, w_sem):
    i = pl.program_id(0)
    hid = wd_s.shape[0]
    n_used = nused_ref[0]

    def block_copy(j):
        slot = j % EXPERT_RING
        src = xs_hbm.at[pl.ds(pl.multiple_of(j * MOE_BLOCK, MOE_BLOCK), MOE_BLOCK)]
        return pltpu.make_async_copy(src, xring.at[slot], ring_sem.at[slot])

    def weight_copies(expert, slot):
        return [pltpu.make_async_copy(src.at[expert], dst.at[slot], w_sem.at[slot])
                for src, dst in ((wg_hbm, wg_buf), (wu_hbm, wu_buf), (wd_hbm, wd_buf))]

    @pl.when(i == 0)
    def _():
        for copy in weight_copies(bexp_ref[0], 0):
            copy.start()
        for j in range(EXPERT_RING - 1):
            @pl.when(j < n_used)
            def _(j=j):
                block_copy(j).start()

    @pl.when(i + EXPERT_RING - 1 < n_used)
    def _():
        block_copy(i + EXPERT_RING - 1).start()

    @pl.when(i < n_used)
    def _():
        @pl.when(bnew_ref[i] == 1)
        def _():
            slot = bslot_ref[i]
            for copy in weight_copies(bexp_ref[i], slot):
                copy.wait()
            nxt = bnext_ref[i]

            @pl.when(nxt >= 0)
            def _():
                for copy in weight_copies(nxt, 1 - slot):
                    copy.start()

            wgu_s[:, :hid] = wg_buf[slot].astype(BF16)
            wgu_s[:, hid:] = wu_buf[slot].astype(BF16)
            wd_s[...] = wd_buf[slot].astype(BF16)

        block_copy(i).wait()
        x = _tiles_to_rows(xring[i % EXPERT_RING], stage_scr).astype(BF16)
        gu = jnp.dot(x, wgu_s[...], preferred_element_type=F32)
        hg, hu = gu[:, :hid], gu[:, hid:]
        y = jnp.dot((_silu(hg) * hu).astype(BF16), wd_s[...], preferred_element_type=F32)
        _rows_to_tiles(y, stage_scr, ys_ref)


def _experts(block_expert, block_new, n_used, block_next, block_slot, xs, w_gate, w_up, w_down):
    n_blocks = block_expert.shape[0]
    d, hid = w_gate.shape[1], w_gate.shape[2]
    tile_block = (MOE_BLOCK, SLABS, LANES)
    hbm = pl.BlockSpec(memory_space=pl.ANY)

    def blk(i, be, bn, nu, bx, bs):
        return (jnp.minimum(i, nu[0] - 1), 0, 0)

    return pl.pallas_call(
        _experts_body,
        grid_spec=pltpu.PrefetchScalarGridSpec(
            num_scalar_prefetch=5,
            grid=(n_blocks,),
            in_specs=[hbm, hbm, hbm, hbm],
            out_specs=pl.BlockSpec(tile_block, blk),
            scratch_shapes=[pltpu.VMEM((d, 2 * hid), BF16), pltpu.VMEM((hid, d), BF16),
                            pltpu.VMEM((MOE_BLOCK * SLABS, LANES), F32),
                            pltpu.VMEM((EXPERT_RING,) + tile_block, BF16),
                            pltpu.VMEM((2, d, hid), F32), pltpu.VMEM((2, d, hid), F32),
                            pltpu.VMEM((2, hid, d), F32),
                            pltpu.SemaphoreType.DMA((EXPERT_RING,)),
                            pltpu.SemaphoreType.DMA((2,))]),
        out_shape=jax.ShapeDtypeStruct(xs.shape, BF16),
        compiler_params=_cparams(("arbitrary",), 32),
        name="experts",
    )(block_expert, block_new, n_used, block_next, block_slot, xs, w_gate, w_up, w_down)


def _combine_body(dest_hbm, gate_hbm, ys_hbm, x1_ref, h_ref, wsg_ref, wsu_ref, wsd_ref,
                  mod_ref, gain_ref, o_ref, dest_smem0, dest_smem1, gate_smem, buf, shared_scr,
                  routed_scr, idx_sem, gate_sem, row_sem, *, final_norm):
    tc = x1_ref.shape[0]
    i = pl.program_id(0)
    has_next = i + 1 < pl.num_programs(0)
    cur = i % 2
    dest_smem = (dest_smem0, dest_smem1)

    def index_copy(tile, slot):
        return pltpu.make_async_copy(dest_hbm.at[tile], dest_smem[slot], idx_sem)

    def start_rows(slot):
        def issue(t, carry):
            for k in range(TOP_K):
                pltpu.make_async_copy(ys_hbm.at[dest_smem[slot][t * TOP_K + k]],
                                      buf.at[slot, t * TOP_K + k],
                                      row_sem.at[slot]).start(priority=k % 2)
            return carry

        lax.fori_loop(0, tc, issue, 0)

    gate_copy = pltpu.make_async_copy(gate_hbm.at[i], gate_smem, gate_sem)
    gate_copy.start()

    @pl.when(i == 0)
    def _():
        index_copy(0, 0).start()
        index_copy(0, 0).wait()
        start_rows(0)

    for slot in range(2):
        @pl.when(jnp.logical_and(has_next, cur != slot))
        def _(slot=slot):
            index_copy(i + 1, slot).start()

    h = h_ref[...]
    hid = (_silu(jnp.dot(h, wsg_ref[...], preferred_element_type=F32))
           * jnp.dot(h, wsu_ref[...], preferred_element_type=F32))
    shared_scr[...] = jnp.dot(hid.astype(BF16), wsd_ref[...], preferred_element_type=F32)

    for slot in range(2):
        @pl.when(jnp.logical_and(has_next, cur != slot))
        def _(slot=slot):
            index_copy(i + 1, slot).wait()
            start_rows(slot)

    pltpu.make_async_copy(ys_hbm.at[pl.ds(0, tc * TOP_K)], buf.at[cur], row_sem.at[cur]).wait()
    gate_copy.wait()

    def weigh(g, carry):
        for u in range(COMBINE_UNROLL):
            t = g * COMBINE_UNROLL + u
            acc = gate_smem[t * TOP_K] * buf[cur, t * TOP_K].astype(F32)
            for k in range(1, TOP_K):
                acc = acc + gate_smem[t * TOP_K + k] * buf[cur, t * TOP_K + k].astype(F32)
            routed_scr[pl.ds(pl.multiple_of(t * SLABS, SLABS), SLABS), :] = acc
        return carry

    lax.fori_loop(0, tc // COMBINE_UNROLL, weigh, 0)

    gate_f = mod_ref[0, 5:6, :]
    gain = gain_ref[...]

    def finish(c, carry):
        r0 = pl.multiple_of(c * COMBINE_ROWS, COMBINE_ROWS)
        rows = pl.ds(r0, COMBINE_ROWS)
        routed = jnp.concatenate(
            [routed_scr[pl.ds(r0 * SLABS + s, COMBINE_ROWS, stride=SLABS), :]
             for s in range(SLABS)], axis=1)
        x2 = x1_ref[rows, :] + gate_f * (shared_scr[rows, :] + routed)
        o_ref[rows, :] = _rms(x2, gain) if final_norm else x2
        return carry

    lax.fori_loop(0, tc // COMBINE_ROWS, finish, 0)


def _combine(dest_tiles, gate_tiles, ys, x1, hrow, wsg, wsu, wsd, mod3, gain, seq, tc, final_norm):
    t, d = x1.shape
    hid = wsg.shape[1]
    row = lambda i: (i, 0)
    const = lambda i: (0, 0)
    return pl.pallas_call(
        functools.partial(_combine_body, final_norm=final_norm),
        grid=(t // tc,),
        in_specs=[pl.BlockSpec(memory_space=pl.ANY),
                  pl.BlockSpec(memory_space=pl.ANY),
                  pl.BlockSpec(memory_space=pl.ANY),
                  pl.BlockSpec((tc, d), row), pl.BlockSpec((tc, d), row),
                  pl.BlockSpec((d, hid), const), pl.BlockSpec((d, hid), const),
                  pl.BlockSpec((hid, d), const),
                  pl.BlockSpec((1, 6, d), lambda i: ((i * tc) // seq, 0, 0)),
                  pl.BlockSpec((1, d), const)],
        out_specs=pl.BlockSpec((tc, d), row),
        out_shape=jax.ShapeDtypeStruct((t, d), F32),
        scratch_shapes=[pltpu.SMEM((TOP_K * tc,), I32), pltpu.SMEM((TOP_K * tc,), I32),
                        pltpu.SMEM((TOP_K * tc,), F32),
                        pltpu.VMEM((2, tc * TOP_K, SLABS, LANES), BF16),
                        pltpu.VMEM((tc, d), F32),
                        pltpu.VMEM((tc * SLABS, LANES), F32),
                        pltpu.SemaphoreType.DMA, pltpu.SemaphoreType.DMA,
                        pltpu.SemaphoreType.DMA((2,))],
        compiler_params=_cparams(("arbitrary",), 56),
        name="combine",
    )(dest_tiles, gate_tiles, ys, x1, hrow, wsg, wsu, wsd, mod3, gain)


def _tile_major(a, tile):
    k, t = a.shape
    return a.reshape(k, t // tile, tile).transpose(1, 2, 0).reshape(t // tile, tile * k)


def kernel(x, c, w_ada, b_ada, norm_mix, w_in, ret_decay, t5_bias, w_ret_up, w_att_up, w_o,
           norm_ffn, w_router, router_bias, w_gate, w_up, w_down, ws_gate, ws_up, ws_down, norm_final):
    bsz, seq, d = x.shape
    depth = w_ada.shape[0]
    t = bsz * seq
    assert d == D_MODEL and seq % (ATT_KWIN * DILATION_PATTERNS[-1][1]) == 0

    half = RET_HEAD_DIM // 2
    inv_freq = ROPE_BASE ** (-jnp.arange(half, dtype=F32) / half)
    ang = jnp.arange(seq, dtype=F32)[:, None] * inv_freq[None, :]
    cos, sin = jnp.cos(ang), jnp.sin(ang)
    bias_tab = _attention_bias(t5_bias)

    n_assign = t * TOP_K
    n_blocks = -(-n_assign // MOE_BLOCK) + N_EXPERTS
    n_rows = n_blocks * MOE_BLOCK
    disp_tile = min(1024, t)
    comb_tile = min(256, t)

    x2 = x.reshape(t, d)
    for layer in range(depth):
        mod3 = _adaln(c, w_ada[layer], b_ada[layer]).reshape(bsz, 6, d)
        proj = _inproj(x2, norm_mix[layer].reshape(1, d), mod3, w_in[layer].astype(BF16), seq)
        proj3 = proj.reshape(bsz, seq, PROJ_WIDTH)
        log_gamma = jnp.log1p(-jnp.exp(ret_decay[layer].astype(F32)))
        ret = _retention(proj3, log_gamma, cos, sin)
        att = _attention(proj3, bias_tab)
        x1, hslab, hrow, logits = _mix(
            ret.reshape(t, RET_WIDTH), att.reshape(t, ATT_WIDTH), proj, x2, mod3,
            norm_ffn[layer].reshape(1, d), w_ret_up[layer].astype(BF16),
            w_att_up[layer].astype(BF16), w_o[layer].astype(BF16), w_router[layer], seq)

        bias_b = jnp.broadcast_to(router_bias[layer].astype(F32)[:, None], (N_EXPERTS, LANES))
        eidx, gate, rank, counts = _route(logits, bias_b)

        counts = counts[:, 0].astype(I32)
        padded = (counts + MOE_BLOCK - 1) // MOE_BLOCK * MOE_BLOCK
        pad_end = jnp.cumsum(padded)
        pad_start = pad_end - padded
        block_row = jnp.arange(n_blocks, dtype=I32) * MOE_BLOCK
        block_expert = jnp.minimum(
            jnp.sum((pad_end[None, :] <= block_row[:, None]).astype(I32), axis=1), N_EXPERTS - 1)
        block_new = jnp.concatenate(
            [jnp.ones((1,), I32), (block_expert[1:] != block_expert[:-1]).astype(I32)])
        n_used = (pad_end[-1:] // MOE_BLOCK).astype(I32)

        dest = _dest(pad_start.astype(I32), eidx, rank)
        xs = _dispatch(_tile_major(dest, disp_tile), hslab, n_rows)
        expert_ids = jnp.arange(N_EXPERTS, dtype=I32)
        later = jnp.logical_and(expert_ids[None, :] > block_expert[:, None], (padded > 0)[None, :])
        block_next = jnp.min(jnp.where(later, expert_ids[None, :], N_EXPERTS), axis=1)
        block_next = jnp.where(block_next == N_EXPERTS, -1, block_next).astype(I32)
        block_slot = ((jnp.cumsum(block_new) - 1) % 2).astype(I32)
        ys = _experts(block_expert, block_new, n_used, block_next, block_slot, xs,
                      w_gate[layer], w_up[layer], w_down[layer])
        x2 = _combine(_tile_major(dest, comb_tile), _tile_major(gate, comb_tile), ys,
                      x1, hrow, ws_gate[layer].astype(BF16), ws_up[layer].astype(BF16),
                      ws_down[layer].astype(BF16), mod3, norm_final.reshape(1, d), seq, comb_tile,
                      final_norm=(layer == depth - 1))
    return x2.reshape(bsz, seq, d)
```

```python
import functools
import math

import jax
import jax.numpy as jnp
import numpy as np
from jax import lax
from jax.experimental import pallas as pl
from jax.experimental.pallas import tpu as pltpu

F32 = jnp.float32
BF16 = jnp.bfloat16
I32 = jnp.int32

D_MODEL = 1024
RET_HEADS = 4
RET_HEAD_DIM = 256
RET_WIDTH = RET_HEADS * RET_HEAD_DIM
RET_CHUNK = 128
ROPE_BASE = 10000.0
ATT_HEADS = 16
ATT_HEAD_DIM = 64
ATT_WIDTH = ATT_HEADS * ATT_HEAD_DIM
DILATION_PATTERNS = ((128, 1), (512, 4), (2048, 16))
REL_BUCKETS = 32
REL_MAX_DISTANCE = 1024
N_EXPERTS = 256
TOP_K = 8
N_GROUPS = 8
TOPK_GROUPS = 4
EXPERT_HIDDEN = 256
ROUTED_SCALE = 2.5
MOE_BLOCK = 256
NORM_EPS = 1e-6
PROJ_WIDTH = 4 * RET_WIDTH + 3 * ATT_WIDTH + 2 * D_MODEL

LANES = 128
SUBLANES = 8
SLABS = D_MODEL // LANES
ATT_RADIUS = 64
ATT_QBLK = 128
ATT_KWIN = 256
MIX_ROWS = 128
RET_UNROLL = 4
EXPERT_RING = 3
COMBINE_ROWS = 32
COMBINE_UNROLL = 8
ATT_COARSE = 4
ATT_UNROLL = 8
NEG_BIG = -1e30
MIB = 1024 * 1024


def _cparams(sem, vmem_mib):
    return pltpu.CompilerParams(dimension_semantics=sem, vmem_limit_bytes=vmem_mib * MIB)


def _sigmoid(x):
    return 1.0 / (1.0 + jnp.exp(-x))


def _silu(x):
    return x * _sigmoid(x)


def _rms(x, gain):
    return x * lax.rsqrt(jnp.mean(x * x, axis=-1, keepdims=True) + NORM_EPS) * gain


def _rows_to_tiles(rows, stage_scr, tiles_ref, first=0):
    n = rows.shape[0]
    base = first * SLABS
    for s in range(SLABS):
        stage_scr[pl.ds(base + s, n, stride=SLABS), :] = rows[:, s * LANES:(s + 1) * LANES]
    staged = stage_scr[pl.ds(base, n * SLABS), :]
    tiles_ref[pl.ds(first, n)] = staged.reshape(n, SLABS, LANES).astype(BF16)


def _tiles_to_rows(tiles, stage_scr):
    n = tiles.shape[0]
    stage_scr[...] = tiles.astype(F32).reshape(n * SLABS, LANES)
    return jnp.concatenate([stage_scr[pl.ds(s, n, stride=SLABS), :] for s in range(SLABS)], axis=1)


def _adaln_body(c_ref, w_ref, b_ref, o_ref):
    cond = _silu(c_ref[...])
    o_ref[...] = jnp.dot(cond, w_ref[...], preferred_element_type=F32,
                         precision=lax.Precision.HIGHEST) + b_ref[...]


def _adaln(c, w, b):
    bsz, d = c.shape
    n = w.shape[1]
    return pl.pallas_call(
        _adaln_body,
        grid=(n // d,),
        in_specs=[pl.BlockSpec((bsz, d), lambda j: (0, 0)),
                  pl.BlockSpec((d, d), lambda j: (0, j)),
                  pl.BlockSpec((1, d), lambda j: (0, j))],
        out_specs=pl.BlockSpec((bsz, d), lambda j: (0, j)),
        out_shape=jax.ShapeDtypeStruct((bsz, n), F32),
        compiler_params=_cparams(("arbitrary",), 32),
        name="adaln",
    )(c, w, b.reshape(1, n))


def _inproj_body(x_ref, gain_ref, mod_ref, w_ref, o_ref, h_scr):
    @pl.when(pl.program_id(1) == 0)
    def _():
        y = _rms(x_ref[...], gain_ref[...])
        h = y * (1.0 + mod_ref[0, 1:2, :]) + mod_ref[0, 0:1, :]
        h_scr[...] = h.astype(BF16)

    o_ref[...] = jnp.dot(h_scr[...], w_ref[...], preferred_element_type=F32).astype(BF16)


def _inproj(x2, gain, mod3, w_bf, seq, tm=1024, tn=4608):
    t, d = x2.shape
    n = w_bf.shape[1]
    return pl.pallas_call(
        _inproj_body,
        grid=(t // tm, n // tn),
        in_specs=[pl.BlockSpec((tm, d), lambda i, j: (i, 0)),
                  pl.BlockSpec((1, d), lambda i, j: (0, 0)),
                  pl.BlockSpec((1, 6, d), lambda i, j: ((i * tm) // seq, 0, 0)),
                  pl.BlockSpec((d, tn), lambda i, j: (0, j))],
        out_specs=pl.BlockSpec((tm, tn), lambda i, j: (i, j)),
        out_shape=jax.ShapeDtypeStruct((t, n), BF16),
        scratch_shapes=[pltpu.VMEM((tm, d), BF16)],
        compiler_params=_cparams(("arbitrary", "arbitrary"), 56),
        name="inproj",
    )(x2, gain, mod3, w_bf)


def _ret_body(lg_ref, q_ref, k_ref, v_ref, g_ref, cos_ref, sin_ref, o_ref,
              qr_scr, kr_scr, accf_scr, accb_scr, stf_scr, stb_scr):
    head = pl.program_id(1)
    seq = q_ref.shape[0]
    n_chunks = seq // RET_CHUNK
    half = RET_HEAD_DIM // 2
    rot_rows = 256

    def rot_step(c, carry):
        r = pl.ds(pl.multiple_of(c * rot_rows, rot_rows), rot_rows)
        cs = cos_ref[r, :]
        sn = sin_ref[r, :]
        for src, dst, scale in ((q_ref, qr_scr, 1.0), (k_ref, kr_scr, RET_HEAD_DIM ** -0.5)):
            t = src[r, :].astype(F32)
            t1, t2 = t[:, :half], t[:, half:]
            dst[r, :half] = ((t1 * cs - t2 * sn) * scale).astype(BF16)
            dst[r, half:] = ((t1 * sn + t2 * cs) * scale).astype(BF16)
        return carry

    lax.fori_loop(0, seq // rot_rows, rot_step, 0)

    ri = lax.broadcasted_iota(I32, (RET_CHUNK, RET_CHUNK), 0)
    ci = lax.broadcasted_iota(I32, (RET_CHUNK, RET_CHUNK), 1)
    rowpos = lax.broadcasted_iota(I32, (RET_CHUNK, RET_HEAD_DIM), 0).astype(F32)

    def decay_tables(lg, forward):
        if forward:
            diff = (ri - ci).astype(F32)
            allowed = ri >= ci
            q_scale = jnp.exp(lg * (rowpos + 1.0))
            k_scale = jnp.exp(lg * (RET_CHUNK - 1.0 - rowpos))
        else:
            diff = (ci - ri).astype(F32)
            allowed = ci > ri
            q_scale = jnp.exp(lg * (RET_CHUNK - rowpos))
            k_scale = jnp.exp(lg * rowpos)
        intra = jnp.where(allowed, jnp.exp(lg * jnp.where(allowed, diff, 0.0)), 0.0)
        chunk_decay = jnp.exp(lg * jnp.full((1, RET_HEAD_DIM), float(RET_CHUNK), F32))
        return intra, q_scale, k_scale, chunk_decay

    def chunk_update(c, tables, st_scr, acc_scr):
        intra, q_scale, k_scale, chunk_decay = tables
        r = pl.ds(pl.multiple_of(c * RET_CHUNK, RET_CHUNK), RET_CHUNK)
        q = qr_scr[r, :]
        k = kr_scr[r, :]
        v = v_ref[r, :]
        scores = lax.dot_general(q, k, (((1,), (1,)), ((), ())),
                                 preferred_element_type=F32) * intra
        state = st_scr[...]
        acc_scr[r, :] = (jnp.dot(scores.astype(BF16), v, preferred_element_type=F32)
                         + jnp.dot((q.astype(F32) * q_scale).astype(BF16), state.astype(BF16),
                                   preferred_element_type=F32))
        k_t = (k.astype(F32) * k_scale).T.astype(BF16)
        st_scr[...] = state * chunk_decay + jnp.dot(k_t, v, preferred_element_type=F32)

    fwd_tables = decay_tables(lg_ref[0, head], True)
    bwd_tables = decay_tables(lg_ref[1, head], False)
    stf_scr[...] = jnp.zeros_like(stf_scr)
    stb_scr[...] = jnp.zeros_like(stb_scr)

    def step(trip, carry):
        for u in range(RET_UNROLL):
            i = trip * RET_UNROLL + u
            chunk_update(i, fwd_tables, stf_scr, accf_scr)
            chunk_update(n_chunks - 1 - i, bwd_tables, stb_scr, accb_scr)
        return carry

    lax.fori_loop(0, n_chunks // RET_UNROLL, step, 0)

    def finish(c, carry):
        r = pl.ds(pl.multiple_of(c * rot_rows, rot_rows), rot_rows)
        o = accf_scr[r, :] + accb_scr[r, :]
        mu = jnp.mean(o, axis=-1, keepdims=True)
        oc = o - mu
        var = jnp.mean(oc * oc, axis=-1, keepdims=True)
        o = oc * lax.rsqrt(var + NORM_EPS)
        o_ref[r, :] = (o * _silu(g_ref[r, :].astype(F32))).astype(BF16)
        return carry

    lax.fori_loop(0, seq // rot_rows, finish, 0)


def _retention(proj3, log_gamma, cos, sin):
    bsz, seq, _ = proj3.shape
    hd = RET_HEAD_DIM

    def col(section):
        return pl.BlockSpec((None, seq, hd), lambda b, h, lg: (b, 0, section * RET_HEADS + h))

    return pl.pallas_call(
        _ret_body,
        grid_spec=pltpu.PrefetchScalarGridSpec(
            num_scalar_prefetch=1,
            grid=(bsz, RET_HEADS),
            in_specs=[col(0), col(1), col(2), col(3),
                      pl.BlockSpec((seq, hd // 2), lambda b, h, lg: (0, 0)),
                      pl.BlockSpec((seq, hd // 2), lambda b, h, lg: (0, 0))],
            out_specs=pl.BlockSpec((None, seq, hd), lambda b, h, lg: (b, 0, h)),
            scratch_shapes=[pltpu.VMEM((seq, hd), BF16), pltpu.VMEM((seq, hd), BF16),
                            pltpu.VMEM((seq, hd), F32), pltpu.VMEM((seq, hd), F32),
                            pltpu.VMEM((hd, hd), F32), pltpu.VMEM((hd, hd), F32)]),
        out_shape=jax.ShapeDtypeStruct((bsz, seq, RET_WIDTH), BF16),
        compiler_params=_cparams(("arbitrary", "arbitrary"), 56),
        name="retention",
    )(log_gamma, proj3, proj3, proj3, proj3, cos, sin)


def _t5_bucket(rel):
    half = REL_BUCKETS // 2
    max_exact = half // 2
    n = jnp.abs(rel)
    large = max_exact + (jnp.log(jnp.maximum(n, 1).astype(F32) / max_exact)
                         / math.log(REL_MAX_DISTANCE / max_exact) * (half - max_exact)).astype(I32)
    large = jnp.minimum(large, half - 1)
    return jnp.where(rel > 0, half, 0) + jnp.where(n < max_exact, n, large)


def _band_buckets():
    qi = jnp.arange(ATT_QBLK, dtype=I32)[:, None]
    kj = jnp.arange(ATT_KWIN, dtype=I32)[None, :]
    tables = []
    for _, dilation in DILATION_PATTERNS:
        cases = []
        for offset in (0, -ATT_RADIUS, ATT_QBLK - ATT_KWIN):
            rel = kj + offset - qi
            cases.append(jnp.where(jnp.abs(rel) <= ATT_RADIUS, _t5_bucket(rel * dilation), -1))
        tables.append(jnp.stack(cases, axis=0))
    return jnp.stack(tables, axis=0)


def _bias_body(t5_ref, bucket_ref, o_ref):
    bucket = bucket_ref[...]

    def head(h, carry):
        acc = jnp.full(bucket.shape, NEG_BIG, F32)
        for b in range(REL_BUCKETS):
            acc = jnp.where(bucket == b, t5_ref[b, h], acc)
        o_ref[h] = acc
        return carry

    lax.fori_loop(0, ATT_HEADS, head, 0)


def _attention_bias(t5_bias):
    buckets = _band_buckets()
    n_pat, n_case = buckets.shape[:2]
    return pl.pallas_call(
        _bias_body,
        grid_spec=pltpu.PrefetchScalarGridSpec(
            num_scalar_prefetch=1,
            grid=(n_pat, n_case),
            in_specs=[pl.BlockSpec((None, None, ATT_QBLK, ATT_KWIN), lambda p, c, t5: (p, c, 0, 0))],
            out_specs=pl.BlockSpec((None, ATT_HEADS, None, ATT_QBLK, ATT_KWIN),
                                   lambda p, c, t5: (p, 0, c, 0, 0))),
        out_shape=jax.ShapeDtypeStruct((n_pat, ATT_HEADS, n_case, ATT_QBLK, ATT_KWIN), F32),
        compiler_params=_cparams(("arbitrary", "arbitrary"), 32),
        name="attn_bias",
    )(t5_bias.astype(F32), buckets)


def _attn_body(q_ref, k_ref, v_ref, bias_ref, o_ref,
               qf, kf, vf, qb, q4, k4, v4, qd, kd, vd, od, ld, o4, l4, og, lgs):
    seq = q_ref.shape[0]
    rows = 256
    lane = lax.broadcasted_iota(I32, (ATT_QBLK, LANES), 1)
    head0 = lane < ATT_HEAD_DIM

    def to_f32(c, carry):
        r = pl.ds(pl.multiple_of(c * rows, rows), rows)
        q = q_ref[r, :].astype(F32) * (ATT_HEAD_DIM ** -0.5)
        qf[r, :] = q
        qb[r, :] = q.astype(BF16)
        kf[r, :] = k_ref[r, :].astype(F32)
        vf[r, :] = v_ref[r, :].astype(F32)
        return carry

    lax.fori_loop(0, seq // rows, to_f32, 0)

    def band_blocks(g, length, n_seg, q_src, k_src, v_src, o_dst, l_dst):
        n_qb = length // ATT_QBLK
        n_blocks = n_seg * n_qb
        unroll = min(ATT_UNROLL, n_blocks)

        def qgroup(it, carry):
            for u in range(unroll):
                qblock(it * unroll + u)
            return carry

        def qblock(b):
            qi = b % n_qb
            base = (b // n_qb) * length
            qs = pl.multiple_of(base + qi * ATT_QBLK, ATT_QBLK)
            ws = pl.multiple_of(
                base + jnp.clip(qi * ATT_QBLK - ATT_RADIUS, 0, length - ATT_KWIN), ATT_RADIUS)
            case = jnp.where(qi == 0, 0, jnp.where(qi == n_qb - 1, 2, 1))
            q = q_src[pl.ds(qs, ATT_QBLK), :]
            k = k_src[pl.ds(ws, ATT_KWIN), :]
            v = v_src[pl.ds(ws, ATT_KWIN), :]
            v_ones = jnp.concatenate([v, jnp.ones_like(v)], axis=1)
            outs, lses = [], []
            for hh in range(2):
                mask = head0 if hh == 0 else jnp.logical_not(head0)
                qm = jnp.where(mask, q, jnp.zeros_like(q))
                s = lax.dot_general(qm, k, (((1,), (1,)), ((), ())), preferred_element_type=F32)
                s = s + bias_ref[g, hh, case]
                m = jnp.max(s, axis=-1, keepdims=True)
                p = jnp.exp(s - m).astype(BF16)
                ol = jnp.dot(p, v_ones, preferred_element_type=F32)
                l = ol[:, LANES:]
                outs.append(ol[:, :LANES] / l)
                lses.append(m + jnp.log(l))
            o_dst[pl.ds(qs, ATT_QBLK), :] = jnp.where(head0, outs[0], outs[1])
            l_dst[pl.ds(qs, ATT_QBLK), :] = jnp.where(head0, lses[0], lses[1])

        lax.fori_loop(0, n_blocks // unroll, qgroup, 0)

    coarse_len = seq // ATT_COARSE

    def split(c, carry):
        strided = pl.ds(c, coarse_len, stride=ATT_COARSE)
        q4[c] = qf[strided, :]
        k4[c] = kf[strided, :]
        v4[c] = vf[strided, :]
        return carry

    lax.fori_loop(0, ATT_COARSE, split, 0)

    for g, (_, dilation) in enumerate(DILATION_PATTERNS):
        length = seq // dilation
        if dilation == 1:
            band_blocks(g, length, 1, qb, k_ref, v_ref, og.at[g], lgs.at[g])
            continue
        assert dilation % ATT_COARSE == 0
        fine = dilation // ATT_COARSE
        assert fine * length == coarse_len and fine * (length // ATT_QBLK) >= min(ATT_UNROLL, fine)

        def coarse_class(c, carry, g=g, length=length, fine=fine):
            segs = [(pl.ds(j, length, stride=fine) if fine > 1 else pl.ds(0, length),
                     pl.ds(j * length, length)) for j in range(fine)]
            for src, dense in segs:
                qd[dense, :] = q4[c, src, :].astype(BF16)
                kd[dense, :] = k4[c, src, :].astype(BF16)
                vd[dense, :] = v4[c, src, :].astype(BF16)
            band_blocks(g, length, fine, qd, kd, vd, od, ld)
            whole = pl.ds(c, coarse_len, stride=ATT_COARSE)
            if fine == 1:
                og[g, whole, :] = od[...]
                lgs[g, whole, :] = ld[...]
            else:
                for src, dense in segs:
                    o4[src, :] = od[dense, :]
                    l4[src, :] = ld[dense, :]
                og[g, whole, :] = o4[...]
                lgs[g, whole, :] = l4[...]
            return carry

        lax.fori_loop(0, ATT_COARSE, coarse_class, 0)

    def merge(c, carry):
        r = pl.ds(pl.multiple_of(c * rows, rows), rows)
        l0, l1, l2 = lgs[0, r, :], lgs[1, r, :], lgs[2, r, :]
        m = jnp.maximum(jnp.maximum(l0, l1), l2)
        w0, w1, w2 = jnp.exp(l0 - m), jnp.exp(l1 - m), jnp.exp(l2 - m)
        num = w0 * og[0, r, :] + w1 * og[1, r, :] + w2 * og[2, r, :]
        o_ref[r, :] = (num / (w0 + w1 + w2)).astype(BF16)
        return carry

    lax.fori_loop(0, seq // rows, merge, 0)


def _attention(proj3, bias_tab):
    bsz, seq, _ = proj3.shape
    n_pat = len(DILATION_PATTERNS)
    pairs = ATT_HEADS // 2
    base = 4 * RET_WIDTH // LANES

    def col(section):
        return pl.BlockSpec((None, seq, LANES),
                            lambda b, hp: (b, 0, base + section * (ATT_WIDTH // LANES) + hp))

    return pl.pallas_call(
        _attn_body,
        grid=(bsz, pairs),
        in_specs=[col(0), col(1), col(2),
                  pl.BlockSpec((n_pat, 2, 3, ATT_QBLK, ATT_KWIN), lambda b, hp: (0, hp, 0, 0, 0))],
        out_specs=pl.BlockSpec((None, seq, LANES), lambda b, hp: (b, 0, hp)),
        out_shape=jax.ShapeDtypeStruct((bsz, seq, ATT_WIDTH), BF16),
        scratch_shapes=[pltpu.VMEM((seq, LANES), F32)] * 3
                       + [pltpu.VMEM((seq, LANES), BF16)]
                       + [pltpu.VMEM((ATT_COARSE, seq // ATT_COARSE, LANES), F32)] * 3
                       + [pltpu.VMEM((seq // ATT_COARSE, LANES), BF16)] * 3
                       + [pltpu.VMEM((seq // ATT_COARSE, LANES), F32)] * 4
                       + [pltpu.VMEM((n_pat, seq, LANES), F32)] * 2,
        compiler_params=_cparams(("arbitrary", "arbitrary"), 56),
        name="attention",
    )(proj3, proj3, proj3, bias_tab)


def _mix_body(ret_ref, att_ref, gr_ref, ga_ref, x_ref, mod_ref, gain_ref,
              wr_ref, wa_ref, wo_ref, wrt_ref, x1_ref, hslab_ref, hrow_ref, logit_ref, slab_scr):
    tm = x_ref.shape[0]
    shift_f, scale_f, gate_m = mod_ref[0, 3:4, :], mod_ref[0, 4:5, :], mod_ref[0, 2:3, :]
    for r in range(tm // MIX_ROWS):
        rows = pl.ds(r * MIX_ROWS, MIX_ROWS)
        y_ret = jnp.dot(ret_ref[rows, :], wr_ref[...], preferred_element_type=F32)
        y_att = jnp.dot(att_ref[rows, :], wa_ref[...], preferred_element_type=F32)
        merged = (_sigmoid(gr_ref[rows, :].astype(F32)) * y_ret
                  + _sigmoid(ga_ref[rows, :].astype(F32)) * y_att)
        mixed = jnp.dot(merged.astype(BF16), wo_ref[...], preferred_element_type=F32)
        x1 = x_ref[rows, :] + gate_m * mixed
        x1_ref[rows, :] = x1
        h = _rms(x1, gain_ref[...]) * (1.0 + scale_f) + shift_f
        hrow_ref[rows, :] = h.astype(BF16)
        _rows_to_tiles(h, slab_scr, hslab_ref, first=r * MIX_ROWS)
        h_hi = h.astype(BF16)
        h_lo = (h - h_hi.astype(F32)).astype(BF16)
        logit_ref[rows, :] = (jnp.dot(h_hi, wrt_ref[0], preferred_element_type=F32)
                              + (jnp.dot(h_lo, wrt_ref[0], preferred_element_type=F32)
                                 + jnp.dot(h_hi, wrt_ref[1], preferred_element_type=F32)))


def _mix(ret2, att2, proj2, x2, mod3, gain, wr, wa, wo, w_router, seq, tm=512):
    t, d = x2.shape
    gate_base = (4 * RET_WIDTH + 3 * ATT_WIDTH) // d
    row = lambda i: (i, 0)
    const = lambda i: (0, 0)
    return pl.pallas_call(
        _mix_body,
        grid=(t // tm,),
        in_specs=[pl.BlockSpec((tm, d), row), pl.BlockSpec((tm, d), row),
                  pl.BlockSpec((tm, d), lambda i: (i, gate_base)),
                  pl.BlockSpec((tm, d), lambda i: (i, gate_base + 1)),
                  pl.BlockSpec((tm, d), row),
                  pl.BlockSpec((1, 6, d), lambda i: ((i * tm) // seq, 0, 0)),
                  pl.BlockSpec((1, d), const),
                  pl.BlockSpec((d, d), const), pl.BlockSpec((d, d), const),
                  pl.BlockSpec((d, d), const),
                  pl.BlockSpec((2, d, N_EXPERTS), lambda i: (0, 0, 0))],
        out_specs=[pl.BlockSpec((tm, d), row),
                   pl.BlockSpec((tm, SLABS, LANES), lambda i: (i, 0, 0)),
                   pl.BlockSpec((tm, d), row),
                   pl.BlockSpec((tm, N_EXPERTS), row)],
        out_shape=[jax.ShapeDtypeStruct((t, d), F32),
                   jax.ShapeDtypeStruct((t, SLABS, LANES), BF16),
                   jax.ShapeDtypeStruct((t, d), BF16),
                   jax.ShapeDtypeStruct((t, N_EXPERTS), F32)],
        scratch_shapes=[pltpu.VMEM((tm * SLABS, LANES), F32)],
        compiler_params=_cparams(("arbitrary",), 56),
        name="mix",
    )(ret2, att2, proj2, proj2, x2, mod3, gain, wr, wa, wo, w_router)


def _route_body(logit_ref, bias_ref, eidx_ref, gate_ref, rank_ref, cnt_ref, carry_scr, tri_scr):
    tn = logit_ref.shape[0]
    per_group = N_EXPERTS // N_GROUPS

    @pl.when(pl.program_id(0) == 0)
    def _():
        carry_scr[...] = jnp.zeros_like(carry_scr)
        r = lax.broadcasted_iota(I32, (tn, tn), 0)
        c = lax.broadcasted_iota(I32, (tn, tn), 1)
        tri_scr[...] = jnp.where(r < c, 1.0, 0.0).astype(BF16)

    scores = _sigmoid(logit_ref[...]).T
    choice = scores + jnp.concatenate([bias_ref[...]] * (tn // LANES), axis=1)
    neg_inf = -jnp.inf

    sub = lax.broadcasted_iota(I32, (per_group, tn), 0).astype(F32)
    group_score = []
    for g in range(N_GROUPS):
        cg = choice[g * per_group:(g + 1) * per_group, :]
        m1 = jnp.max(cg, axis=0, keepdims=True)
        first = jnp.min(jnp.where(cg == m1, sub, float(per_group)), axis=0, keepdims=True)
        m2 = jnp.max(jnp.where(sub == first, neg_inf, cg), axis=0, keepdims=True)
        group_score.append(m1 + m2)

    masked = []
    for a in range(N_GROUPS):
        beaten = jnp.zeros((1, tn), F32)
        for b in range(N_GROUPS):
            if b == a:
                continue
            wins = (group_score[b] >= group_score[a]) if b < a else (group_score[b] > group_score[a])
            beaten = beaten + jnp.where(wins, 1.0, 0.0)
        keep = beaten < float(TOPK_GROUPS)
        masked.append(jnp.where(keep, choice[a * per_group:(a + 1) * per_group, :], neg_inf))
    work = jnp.concatenate(masked, axis=0)

    eid = lax.broadcasted_iota(I32, (N_EXPERTS, tn), 0).astype(F32)
    picked = jnp.zeros((N_EXPERTS, tn), F32)
    idx_rows, gate_rows = [], []
    for _ in range(TOP_K):
        m = jnp.max(work, axis=0, keepdims=True)
        idx = jnp.min(jnp.where(work == m, eid, float(N_EXPERTS)), axis=0, keepdims=True)
        sel = eid == idx
        gate_rows.append(jnp.sum(jnp.where(sel, scores, 0.0), axis=0, keepdims=True))
        picked = picked + jnp.where(sel, 1.0, 0.0)
        work = jnp.where(sel, neg_inf, work)
        idx_rows.append(idx)

    before = (jnp.dot(picked.astype(BF16), tri_scr[...], preferred_element_type=F32)
              + carry_scr[:, 0:1])
    rank_rows = [jnp.sum(jnp.where(eid == idx, before, 0.0), axis=0, keepdims=True)
                 for idx in idx_rows]
    carry = carry_scr[...] + jnp.sum(picked, axis=1, keepdims=True)
    carry_scr[...] = carry
    cnt_ref[...] = carry

    gates = jnp.concatenate(gate_rows, axis=0)
    gates = gates / jnp.sum(gates, axis=0, keepdims=True) * ROUTED_SCALE
    eidx_ref[...] = jnp.concatenate(idx_rows, axis=0).astype(I32)
    gate_ref[...] = gates
    rank_ref[...] = jnp.concatenate(rank_rows, axis=0).astype(I32)


def _route(logits, bias_b, tn=512):
    t = logits.shape[0]
    tok = lambda i: (0, i)
    return pl.pallas_call(
        _route_body,
        grid=(t // tn,),
        in_specs=[pl.BlockSpec((tn, N_EXPERTS), lambda i: (i, 0)),
                  pl.BlockSpec((N_EXPERTS, LANES), lambda i: (0, 0))],
        out_specs=[pl.BlockSpec((TOP_K, tn), tok), pl.BlockSpec((TOP_K, tn), tok),
                   pl.BlockSpec((TOP_K, tn), tok),
                   pl.BlockSpec((N_EXPERTS, LANES), lambda i: (0, 0))],
        out_shape=[jax.ShapeDtypeStruct((TOP_K, t), I32), jax.ShapeDtypeStruct((TOP_K, t), F32),
                   jax.ShapeDtypeStruct((TOP_K, t), I32),
                   jax.ShapeDtypeStruct((N_EXPERTS, LANES), F32)],
        scratch_shapes=[pltpu.VMEM((N_EXPERTS, LANES), F32), pltpu.VMEM((tn, tn), BF16)],
        compiler_params=_cparams(("arbitrary",), 48),
        name="route",
    )(logits, bias_b)


def _dest_body(start_ref, eidx_ref, rank_ref, dest_ref):
    e = eidx_ref[...]

    def body(j, acc):
        return jnp.where(e == j, start_ref[j], acc)

    dest_ref[...] = rank_ref[...] + lax.fori_loop(0, N_EXPERTS, body, jnp.zeros_like(e))


def _dest(pad_start, eidx, rank, tn=2048):
    t = eidx.shape[1]
    tok = lambda i, s: (0, i)
    return pl.pallas_call(
        _dest_body,
        grid_spec=pltpu.PrefetchScalarGridSpec(
            num_scalar_prefetch=1,
            grid=(t // tn,),
            in_specs=[pl.BlockSpec((TOP_K, tn), tok), pl.BlockSpec((TOP_K, tn), tok)],
            out_specs=pl.BlockSpec((TOP_K, tn), tok)),
        out_shape=jax.ShapeDtypeStruct((TOP_K, t), I32),
        compiler_params=_cparams(("arbitrary",), 32),
        name="dest",
    )(pad_start, eidx, rank)


def _dispatch_body(dest_hbm, h_ref, xs_hbm, dest_smem, idx_sem, row_sem):
    tt = dest_smem.shape[0] // TOP_K
    idx_copy = pltpu.make_async_copy(dest_hbm.at[pl.program_id(0)], dest_smem, idx_sem)
    idx_copy.start()
    idx_copy.wait()

    def issue(t, carry):
        for k in range(TOP_K):
            pltpu.make_async_copy(h_ref.at[t], xs_hbm.at[dest_smem[t * TOP_K + k]],
                                  row_sem).start(priority=k % 2)
        return carry

    lax.fori_loop(0, tt, issue, 0)

    for k in range(TOP_K):
        pltpu.make_async_copy(h_ref, xs_hbm.at[pl.ds(0, tt)], row_sem).wait()


def _dispatch(dest_tiles, htiles, n_rows):
    n_tiles, width = dest_tiles.shape
    tt = width // TOP_K
    return pl.pallas_call(
        _dispatch_body,
        grid=(n_tiles,),
        in_specs=[pl.BlockSpec(memory_space=pl.ANY),
                  pl.BlockSpec((tt, SLABS, LANES), lambda i: (i, 0, 0))],
        out_specs=pl.BlockSpec(memory_space=pl.ANY),
        out_shape=jax.ShapeDtypeStruct((n_rows, SLABS, LANES), BF16),
        scratch_shapes=[pltpu.SMEM((width,), I32), pltpu.SemaphoreType.DMA, pltpu.SemaphoreType.DMA],
        compiler_params=_cparams(("arbitrary",), 32),
        name="dispatch",
    )(dest_tiles, htiles)


def _experts_body(bexp_ref, bnew_ref, nused_ref, bnext_ref, bslot_ref, xs_hbm, wg_hbm, wu_hbm, wd_hbm,
                  ys_ref, wgu_s, wd_s, stage_scr, xring, wg_buf, wu_buf, wd_buf, ring_sem, w_sem):
    i = pl.program_id(0)
    hid = wd_s.shape[0]
    n_used = nused_ref[0]

    def block_copy(j):
        slot = j % EXPERT_RING
        src = xs_hbm.at[pl.ds(pl.multiple_of(j * MOE_BLOCK, MOE_BLOCK), MOE_BLOCK)]
        return pltpu.make_async_copy(src, xring.at[slot], ring_sem.at[slot])

    def weight_copies(expert, slot):
        return [pltpu.make_async_copy(src.at[expert], dst.at[slot], w_sem.at[slot])
                for src, dst in ((wg_hbm, wg_buf), (wu_hbm, wu_buf), (wd_hbm, wd_buf))]

    @pl.when(i == 0)
    def _():
        for copy in weight_copies(bexp_ref[0], 0):
            copy.start()
        for j in range(EXPERT_RING - 1):
            @pl.when(j < n_used)
            def _(j=j):
                block_copy(j).start()

    @pl.when(i + EXPERT_RING - 1 < n_used)
    def _():
        block_copy(i + EXPERT_RING - 1).start()

    @pl.when(i < n_used)
    def _():
        @pl.when(bnew_ref[i] == 1)
        def _():
            slot = bslot_ref[i]
            for copy in weight_copies(bexp_ref[i], slot):
                copy.wait()
            nxt = bnext_ref[i]

            @pl.when(nxt >= 0)
            def _():
                for copy in weight_copies(nxt, 1 - slot):
                    copy.start()

            wgu_s[:, :hid] = wg_buf[slot].astype(BF16)
            wgu_s[:, hid:] = wu_buf[slot].astype(BF16)
            wd_s[...] = wd_buf[slot].astype(BF16)

        block_copy(i).wait()
        x = _tiles_to_rows(xring[i % EXPERT_RING], stage_scr).astype(BF16)
        gu = jnp.dot(x, wgu_s[...], preferred_element_type=F32)
        hg, hu = gu[:, :hid], gu[:, hid:]
        y = jnp.dot((_silu(hg) * hu).astype(BF16), wd_s[...], preferred_element_type=F32)
        _rows_to_tiles(y, stage_scr, ys_ref)


def _experts(block_expert, block_new, n_used, block_next, block_slot, xs, w_gate, w_up, w_down):
    n_blocks = block_expert.shape[0]
    d, hid = w_gate.shape[1], w_gate.shape[2]
    tile_block = (MOE_BLOCK, SLABS, LANES)
    hbm = pl.BlockSpec(memory_space=pl.ANY)

    def blk(i, be, bn, nu, bx, bs):
        return (jnp.minimum(i, nu[0] - 1), 0, 0)

    return pl.pallas_call(
        _experts_body,
        grid_spec=pltpu.PrefetchScalarGridSpec(
            num_scalar_prefetch=5,
            grid=(n_blocks,),
            in_specs=[hbm, hbm, hbm, hbm],
            out_specs=pl.BlockSpec(tile_block, blk),
            scratch_shapes=[pltpu.VMEM((d, 2 * hid), BF16), pltpu.VMEM((hid, d), BF16),
                            pltpu.VMEM((MOE_BLOCK * SLABS, LANES), F32),
                            pltpu.VMEM((EXPERT_RING,) + tile_block, BF16),
                            pltpu.VMEM((2, d, hid), F32), pltpu.VMEM((2, d, hid), F32),
                            pltpu.VMEM((2, hid, d), F32),
                            pltpu.SemaphoreType.DMA((EXPERT_RING,)),
                            pltpu.SemaphoreType.DMA((2,))]),
        out_shape=jax.ShapeDtypeStruct(xs.shape, BF16),
        compiler_params=_cparams(("arbitrary",), 32),
        name="experts",
    )(block_expert, block_new, n_used, block_next, block_slot, xs, w_gate, w_up, w_down)


def _combine_body(dest_hbm, gate_hbm, ys_hbm, x1_ref, h_ref, wsg_ref, wsu_ref, wsd_ref,
                  mod_ref, gain_ref, o_ref, dest_smem0, dest_smem1, gate_smem, buf, shared_scr,
                  routed_scr, idx_sem, gate_sem, row_sem, *, final_norm):
    tc = x1_ref.shape[0]
    i = pl.program_id(0)
    has_next = i + 1 < pl.num_programs(0)
    cur = i % 2
    dest_smem = (dest_smem0, dest_smem1)

    def index_copy(tile, slot):
        return pltpu.make_async_copy(dest_hbm.at[tile], dest_smem[slot], idx_sem)

    def start_rows(slot):
        def issue(t, carry):
            for k in range(TOP_K):
                pltpu.make_async_copy(ys_hbm.at[dest_smem[slot][t * TOP_K + k]],
                                      buf.at[slot, t * TOP_K + k],
                                      row_sem.at[slot]).start(priority=k % 2)
            return carry

        lax.fori_loop(0, tc, issue, 0)

    gate_copy = pltpu.make_async_copy(gate_hbm.at[i], gate_smem, gate_sem)
    gate_copy.start()

    @pl.when(i == 0)
    def _():
        index_copy(0, 0).start()
        index_copy(0, 0).wait()
        start_rows(0)

    for slot in range(2):
        @pl.when(jnp.logical_and(has_next, cur != slot))
        def _(slot=slot):
            index_copy(i + 1, slot).start()

    h = h_ref[...]
    hid = (_silu(jnp.dot(h, wsg_ref[...], preferred_element_type=F32))
           * jnp.dot(h, wsu_ref[...], preferred_element_type=F32))
    shared_scr[...] = jnp.dot(hid.astype(BF16), wsd_ref[...], preferred_element_type=F32)

    for slot in range(2):
        @pl.when(jnp.logical_and(has_next, cur != slot))
        def _(slot=slot):
            index_copy(i + 1, slot).wait()
            start_rows(slot)

    pltpu.make_async_copy(ys_hbm.at[pl.ds(0, tc * TOP_K)], buf.at[cur], row_sem.at[cur]).wait()
    gate_copy.wait()

    def weigh(g, carry):
        for u in range(COMBINE_UNROLL):
            t = g * COMBINE_UNROLL + u
            acc = gate_smem[t * TOP_K] * buf[cur, t * TOP_K].astype(F32)
            for k in range(1, TOP_K):
                acc = acc + gate_smem[t * TOP_K + k] * buf[cur, t * TOP_K + k].astype(F32)
            routed_scr[pl.ds(pl.multiple_of(t * SLABS, SLABS), SLABS), :] = acc
        return carry

    lax.fori_loop(0, tc // COMBINE_UNROLL, weigh, 0)

    gate_f = mod_ref[0, 5:6, :]
    gain = gain_ref[...]

    def finish(c, carry):
        r0 = pl.multiple_of(c * COMBINE_ROWS, COMBINE_ROWS)
        rows = pl.ds(r0, COMBINE_ROWS)
        routed = jnp.concatenate(
            [routed_scr[pl.ds(r0 * SLABS + s, COMBINE_ROWS, stride=SLABS), :]
             for s in range(SLABS)], axis=1)
        x2 = x1_ref[rows, :] + gate_f * (shared_scr[rows, :] + routed)
        o_ref[rows, :] = _rms(x2, gain) if final_norm else x2
        return carry

    lax.fori_loop(0, tc // COMBINE_ROWS, finish, 0)


def _combine(dest_tiles, gate_tiles, ys, x1, hrow, wsg, wsu, wsd, mod3, gain, seq, tc, final_norm):
    t, d = x1.shape
    hid = wsg.shape[1]
    row = lambda i: (i, 0)
    const = lambda i: (0, 0)
    return pl.pallas_call(
        functools.partial(_combine_body, final_norm=final_norm),
        grid=(t // tc,),
        in_specs=[pl.BlockSpec(memory_space=pl.ANY),
                  pl.BlockSpec(memory_space=pl.ANY),
                  pl.BlockSpec(memory_space=pl.ANY),
                  pl.BlockSpec((tc, d), row), pl.BlockSpec((tc, d), row),
                  pl.BlockSpec((d, hid), const), pl.BlockSpec((d, hid), const),
                  pl.BlockSpec((hid, d), const),
                  pl.BlockSpec((1, 6, d), lambda i: ((i * tc) // seq, 0, 0)),
                  pl.BlockSpec((1, d), const)],
        out_specs=pl.BlockSpec((tc, d), row),
        out_shape=jax.ShapeDtypeStruct((t, d), F32),
        scratch_shapes=[pltpu.SMEM((TOP_K * tc,), I32), pltpu.SMEM((TOP_K * tc,), I32),
                        pltpu.SMEM((TOP_K * tc,), F32),
                        pltpu.VMEM((2, tc * TOP_K, SLABS, LANES), BF16),
                        pltpu.VMEM((tc, d), F32),
                        pltpu.VMEM((tc * SLABS, LANES), F32),
                        pltpu.SemaphoreType.DMA, pltpu.SemaphoreType.DMA,
                        pltpu.SemaphoreType.DMA((2,))],
        compiler_params=_cparams(("arbitrary",), 56),
        name="combine",
    )(dest_tiles, gate_tiles, ys, x1, hrow, wsg, wsu, wsd, mod3, gain)


def _split_bf16(w):
    hi = w.astype(BF16)
    lo = (w.astype(F32) - hi.astype(F32)).astype(BF16)
    return jnp.stack([hi, lo], axis=0)


def _tile_major(a, tile):
    k, t = a.shape
    return a.reshape(k, t // tile, tile).transpose(1, 2, 0).reshape(t // tile, tile * k)


def kernel(x, c, w_ada, b_ada, norm_mix, w_in, ret_decay, t5_bias, w_ret_up, w_att_up, w_o,
           norm_ffn, w_router, router_bias, w_gate, w_up, w_down, ws_gate, ws_up, ws_down, norm_final):
    bsz, seq, d = x.shape
    depth = w_ada.shape[0]
    t = bsz * seq
    assert d == D_MODEL and seq % (ATT_KWIN * DILATION_PATTERNS[-1][1]) == 0

    half = RET_HEAD_DIM // 2
    inv_freq = ROPE_BASE ** (-jnp.arange(half, dtype=F32) / half)
    ang = jnp.arange(seq, dtype=F32)[:, None] * inv_freq[None, :]
    cos, sin = jnp.cos(ang), jnp.sin(ang)
    bias_tab = _attention_bias(t5_bias)

    n_assign = t * TOP_K
    n_blocks = -(-n_assign // MOE_BLOCK) + N_EXPERTS
    n_rows = n_blocks * MOE_BLOCK
    disp_tile = min(1024, t)
    comb_tile = min(256, t)

    x2 = x.reshape(t, d)
    for layer in range(depth):
        mod3 = _adaln(c, w_ada[layer], b_ada[layer]).reshape(bsz, 6, d)
        proj = _inproj(x2, norm_mix[layer].reshape(1, d), mod3, w_in[layer].astype(BF16), seq)
        proj3 = proj.reshape(bsz, seq, PROJ_WIDTH)
        log_gamma = jnp.log1p(-jnp.exp(ret_decay[layer].astype(F32)))
        ret = _retention(proj3, log_gamma, cos, sin)
        att = _attention(proj3, bias_tab)
        x1, hslab, hrow, logits = _mix(
            ret.reshape(t, RET_WIDTH), att.reshape(t, ATT_WIDTH), proj, x2, mod3,
            norm_ffn[layer].reshape(1, d), w_ret_up[layer].astype(BF16),
            w_att_up[layer].astype(BF16), w_o[layer].astype(BF16), _split_bf16(w_router[layer]),
            seq)

        bias_b = jnp.broadcast_to(router_bias[layer].astype(F32)[:, None], (N_EXPERTS, LANES))
        eidx, gate, rank, counts = _route(logits, bias_b)

        counts = counts[:, 0].astype(I32)
        padded = (counts + MOE_BLOCK - 1) // MOE_BLOCK * MOE_BLOCK
        pad_end = jnp.cumsum(padded)
        pad_start = pad_end - padded
        block_row = jnp.arange(n_blocks, dtype=I32) * MOE_BLOCK
        block_expert = jnp.minimum(
            jnp.sum((pad_end[None, :] <= block_row[:, None]).astype(I32), axis=1), N_EXPERTS - 1)
        block_new = jnp.concatenate(
            [jnp.ones((1,), I32), (block_expert[1:] != block_expert[:-1]).astype(I32)])
        n_used = (pad_end[-1:] // MOE_BLOCK).astype(I32)

        dest = _dest(pad_start.astype(I32), eidx, rank)
        xs = _dispatch(_tile_major(dest, disp_tile), hslab, n_rows)
        expert_ids = jnp.arange(N_EXPERTS, dtype=I32)
        later = jnp.logical_and(expert_ids[None, :] > block_expert[:, None], (padded > 0)[None, :])
        block_next = jnp.min(jnp.where(later, expert_ids[None, :], N_EXPERTS), axis=1)
        block_next = jnp.where(block_next == N_EXPERTS, -1, block_next).astype(I32)
        block_slot = ((jnp.cumsum(block_new) - 1) % 2).astype(I32)
        ys = _experts(block_expert, block_new, n_used, block_next, block_slot, xs,
                      w_gate[layer], w_up[layer], w_down[layer])
        x2 = _combine(_tile_major(dest, comb_tile), _tile_major(gate, comb_tile), ys,
                      x1, hrow, ws_gate[layer].astype(BF16), ws_up[layer].astype(BF16),
                      ws_down[layer].astype(BF16), mod3, norm_final.reshape(1, d), seq, comb_tile,
                      final_norm=(layer == depth - 1))
    return x2.reshape(bsz, seq, d)
```

```python
import functools
import math

import jax
import jax.numpy as jnp
import numpy as np
from jax import lax
from jax.experimental import pallas as pl
from jax.experimental.pallas import tpu as pltpu

F32 = jnp.float32
BF16 = jnp.bfloat16
I32 = jnp.int32

D_MODEL = 1024
RET_HEADS = 4
RET_HEAD_DIM = 256
RET_WIDTH = RET_HEADS * RET_HEAD_DIM
RET_CHUNK = 128
ROPE_BASE = 10000.0
ATT_HEADS = 16
ATT_HEAD_DIM = 64
ATT_WIDTH = ATT_HEADS * ATT_HEAD_DIM
DILATION_PATTERNS = ((128, 1), (512, 4), (2048, 16))
REL_BUCKETS = 32
REL_MAX_DISTANCE = 1024
N_EXPERTS = 256
TOP_K = 8
N_GROUPS = 8
TOPK_GROUPS = 4
EXPERT_HIDDEN = 256
ROUTED_SCALE = 2.5
MOE_BLOCK = 256
NORM_EPS = 1e-6
PROJ_WIDTH = 4 * RET_WIDTH + 3 * ATT_WIDTH + 2 * D_MODEL

LANES = 128
SUBLANES = 8
SLABS = D_MODEL // LANES
ATT_RADIUS = 64
ATT_QBLK = 128
ATT_KWIN = 256
MIX_ROWS = 128
RET_UNROLL = 8
EXPERT_RING = 3
COMBINE_ROWS = 32
COMBINE_UNROLL = 8
ATT_COARSE = 4
ATT_CLASSES = 2
ATT_UNROLL = 16
NEG_BIG = -1e30
MIB = 1024 * 1024


def _cparams(sem, vmem_mib):
    return pltpu.CompilerParams(dimension_semantics=sem, vmem_limit_bytes=vmem_mib * MIB)


def _sigmoid(x):
    return 1.0 / (1.0 + jnp.exp(-x))


def _silu(x):
    return x * _sigmoid(x)


def _rms(x, gain):
    return x * lax.rsqrt(jnp.mean(x * x, axis=-1, keepdims=True) + NORM_EPS) * gain


def _rows_to_tiles(rows, stage_scr, tiles_ref, first=0):
    n = rows.shape[0]
    base = first * SLABS
    for s in range(SLABS):
        stage_scr[pl.ds(base + s, n, stride=SLABS), :] = rows[:, s * LANES:(s + 1) * LANES]
    staged = stage_scr[pl.ds(base, n * SLABS), :]
    tiles_ref[pl.ds(first, n)] = staged.reshape(n, SLABS, LANES).astype(BF16)


def _tiles_to_rows(tiles, stage_scr):
    n = tiles.shape[0]
    stage_scr[...] = tiles.astype(F32).reshape(n * SLABS, LANES)
    return jnp.concatenate([stage_scr[pl.ds(s, n, stride=SLABS), :] for s in range(SLABS)], axis=1)


def _adaln_body(c_ref, w_ref, b_ref, o_ref):
    cond = _silu(c_ref[...])
    o_ref[...] = jnp.dot(cond, w_ref[...], preferred_element_type=F32,
                         precision=lax.Precision.HIGHEST) + b_ref[...]


def _adaln(c, w, b):
    bsz, d = c.shape
    n = w.shape[1]
    return pl.pallas_call(
        _adaln_body,
        grid=(n // d,),
        in_specs=[pl.BlockSpec((bsz, d), lambda j: (0, 0)),
                  pl.BlockSpec((d, d), lambda j: (0, j)),
                  pl.BlockSpec((1, d), lambda j: (0, j))],
        out_specs=pl.BlockSpec((bsz, d), lambda j: (0, j)),
        out_shape=jax.ShapeDtypeStruct((bsz, n), F32),
        compiler_params=_cparams(("arbitrary",), 32),
        name="adaln",
    )(c, w, b.reshape(1, n))


def _inproj_body(x_ref, gain_ref, mod_ref, w_ref, o_ref, h_scr):
    @pl.when(pl.program_id(1) == 0)
    def _():
        y = _rms(x_ref[...], gain_ref[...])
        h = y * (1.0 + mod_ref[0, 1:2, :]) + mod_ref[0, 0:1, :]
        h_scr[...] = h.astype(BF16)

    o_ref[...] = jnp.dot(h_scr[...], w_ref[...], preferred_element_type=F32).astype(BF16)


def _inproj(x2, gain, mod3, w_bf, seq, tm=1024, tn=4608):
    t, d = x2.shape
    n = w_bf.shape[1]
    return pl.pallas_call(
        _inproj_body,
        grid=(t // tm, n // tn),
        in_specs=[pl.BlockSpec((tm, d), lambda i, j: (i, 0)),
                  pl.BlockSpec((1, d), lambda i, j: (0, 0)),
                  pl.BlockSpec((1, 6, d), lambda i, j: ((i * tm) // seq, 0, 0)),
                  pl.BlockSpec((d, tn), lambda i, j: (0, j))],
        out_specs=pl.BlockSpec((tm, tn), lambda i, j: (i, j)),
        out_shape=jax.ShapeDtypeStruct((t, n), BF16),
        scratch_shapes=[pltpu.VMEM((tm, d), BF16)],
        compiler_params=_cparams(("arbitrary", "arbitrary"), 56),
        name="inproj",
    )(x2, gain, mod3, w_bf)


def _ret_body(lg_ref, q_ref, k_ref, v_ref, g_ref, cos_ref, sin_ref, o_ref,
              qr_scr, kr_scr, accf_scr, accb_scr, stf_scr, stb_scr):
    head = pl.program_id(1)
    seq = q_ref.shape[0]
    n_chunks = seq // RET_CHUNK
    half = RET_HEAD_DIM // 2
    rot_rows = 256

    def rot_step(c, carry):
        r = pl.ds(pl.multiple_of(c * rot_rows, rot_rows), rot_rows)
        cs = cos_ref[r, :]
        sn = sin_ref[r, :]
        for src, dst, scale in ((q_ref, qr_scr, 1.0), (k_ref, kr_scr, RET_HEAD_DIM ** -0.5)):
            t = src[r, :].astype(F32)
            t1, t2 = t[:, :half], t[:, half:]
            dst[r, :half] = ((t1 * cs - t2 * sn) * scale).astype(BF16)
            dst[r, half:] = ((t1 * sn + t2 * cs) * scale).astype(BF16)
        return carry

    lax.fori_loop(0, seq // rot_rows, rot_step, 0)

    ri = lax.broadcasted_iota(I32, (RET_CHUNK, RET_CHUNK), 0)
    ci = lax.broadcasted_iota(I32, (RET_CHUNK, RET_CHUNK), 1)
    rowpos = lax.broadcasted_iota(I32, (RET_CHUNK, RET_HEAD_DIM), 0).astype(F32)

    def decay_tables(lg, forward):
        if forward:
            diff = (ri - ci).astype(F32)
            allowed = ri >= ci
            q_scale = jnp.exp(lg * (rowpos + 1.0))
            k_scale = jnp.exp(lg * (RET_CHUNK - 1.0 - rowpos))
        else:
            diff = (ci - ri).astype(F32)
            allowed = ci > ri
            q_scale = jnp.exp(lg * (RET_CHUNK - rowpos))
            k_scale = jnp.exp(lg * rowpos)
        intra = jnp.where(allowed, jnp.exp(lg * jnp.where(allowed, diff, 0.0)), 0.0)
        chunk_decay = jnp.exp(lg * jnp.full((1, RET_HEAD_DIM), float(RET_CHUNK), F32))
        return intra, q_scale, k_scale, chunk_decay

    def chunk_update(c, tables, st_scr, acc_scr):
        intra, q_scale, k_scale, chunk_decay = tables
        r = pl.ds(pl.multiple_of(c * RET_CHUNK, RET_CHUNK), RET_CHUNK)
        q = qr_scr[r, :]
        k = kr_scr[r, :]
        v = v_ref[r, :]
        scores = lax.dot_general(q, k, (((1,), (1,)), ((), ())),
                                 preferred_element_type=F32) * intra
        state = st_scr[...]
        acc_scr[r, :] = (jnp.dot(scores.astype(BF16), v, preferred_element_type=F32)
                         + jnp.dot((q.astype(F32) * q_scale).astype(BF16), state.astype(BF16),
                                   preferred_element_type=F32))
        k_t = (k.astype(F32) * k_scale).T.astype(BF16)
        st_scr[...] = state * chunk_decay + jnp.dot(k_t, v, preferred_element_type=F32)

    fwd_tables = decay_tables(lg_ref[0, head], True)
    bwd_tables = decay_tables(lg_ref[1, head], False)
    stf_scr[...] = jnp.zeros_like(stf_scr)
    stb_scr[...] = jnp.zeros_like(stb_scr)

    def step(trip, carry):
        for u in range(RET_UNROLL):
            i = trip * RET_UNROLL + u
            chunk_update(i, fwd_tables, stf_scr, accf_scr)
            chunk_update(n_chunks - 1 - i, bwd_tables, stb_scr, accb_scr)
        return carry

    lax.fori_loop(0, n_chunks // RET_UNROLL, step, 0)

    def finish(c, carry):
        r = pl.ds(pl.multiple_of(c * rot_rows, rot_rows), rot_rows)
        o = accf_scr[r, :] + accb_scr[r, :]
        mu = jnp.mean(o, axis=-1, keepdims=True)
        oc = o - mu
        var = jnp.mean(oc * oc, axis=-1, keepdims=True)
        o = oc * lax.rsqrt(var + NORM_EPS)
        o_ref[r, :] = (o * _silu(g_ref[r, :].astype(F32))).astype(BF16)
        return carry

    lax.fori_loop(0, seq // rot_rows, finish, 0)


def _retention(proj3, log_gamma, cos, sin):
    bsz, seq, _ = proj3.shape
    hd = RET_HEAD_DIM

    def col(section):
        return pl.BlockSpec((None, seq, hd), lambda b, h, lg: (b, 0, section * RET_HEADS + h))

    return pl.pallas_call(
        _ret_body,
        grid_spec=pltpu.PrefetchScalarGridSpec(
            num_scalar_prefetch=1,
            grid=(bsz, RET_HEADS),
            in_specs=[col(0), col(1), col(2), col(3),
                      pl.BlockSpec((seq, hd // 2), lambda b, h, lg: (0, 0)),
                      pl.BlockSpec((seq, hd // 2), lambda b, h, lg: (0, 0))],
            out_specs=pl.BlockSpec((None, seq, hd), lambda b, h, lg: (b, 0, h)),
            scratch_shapes=[pltpu.VMEM((seq, hd), BF16), pltpu.VMEM((seq, hd), BF16),
                            pltpu.VMEM((seq, hd), F32), pltpu.VMEM((seq, hd), F32),
                            pltpu.VMEM((hd, hd), F32), pltpu.VMEM((hd, hd), F32)]),
        out_shape=jax.ShapeDtypeStruct((bsz, seq, RET_WIDTH), BF16),
        compiler_params=_cparams(("arbitrary", "arbitrary"), 56),
        name="retention",
    )(log_gamma, proj3, proj3, proj3, proj3, cos, sin)


def _t5_bucket(rel):
    half = REL_BUCKETS // 2
    max_exact = half // 2
    n = jnp.abs(rel)
    large = max_exact + (jnp.log(jnp.maximum(n, 1).astype(F32) / max_exact)
                         / math.log(REL_MAX_DISTANCE / max_exact) * (half - max_exact)).astype(I32)
    large = jnp.minimum(large, half - 1)
    return jnp.where(rel > 0, half, 0) + jnp.where(n < max_exact, n, large)


def _band_buckets():
    qi = jnp.arange(ATT_QBLK, dtype=I32)[:, None]
    kj = jnp.arange(ATT_KWIN, dtype=I32)[None, :]
    tables = []
    for _, dilation in DILATION_PATTERNS:
        cases = []
        for offset in (0, -ATT_RADIUS, ATT_QBLK - ATT_KWIN):
            rel = kj + offset - qi
            cases.append(jnp.where(jnp.abs(rel) <= ATT_RADIUS, _t5_bucket(rel * dilation), -1))
        tables.append(jnp.stack(cases, axis=0))
    return jnp.stack(tables, axis=0)


def _bias_body(t5_ref, bucket_ref, o_ref):
    bucket = bucket_ref[...]

    def head(h, carry):
        acc = jnp.full(bucket.shape, NEG_BIG, F32)
        for b in range(REL_BUCKETS):
            acc = jnp.where(bucket == b, t5_ref[b, h], acc)
        o_ref[h] = acc
        return carry

    lax.fori_loop(0, ATT_HEADS, head, 0)


def _attention_bias(t5_bias):
    buckets = _band_buckets()
    n_pat, n_case = buckets.shape[:2]
    return pl.pallas_call(
        _bias_body,
        grid_spec=pltpu.PrefetchScalarGridSpec(
            num_scalar_prefetch=1,
            grid=(n_pat, n_case),
            in_specs=[pl.BlockSpec((None, None, ATT_QBLK, ATT_KWIN), lambda p, c, t5: (p, c, 0, 0))],
            out_specs=pl.BlockSpec((None, ATT_HEADS, None, ATT_QBLK, ATT_KWIN),
                                   lambda p, c, t5: (p, 0, c, 0, 0))),
        out_shape=jax.ShapeDtypeStruct((n_pat, ATT_HEADS, n_case, ATT_QBLK, ATT_KWIN), F32),
        compiler_params=_cparams(("arbitrary", "arbitrary"), 32),
        name="attn_bias",
    )(t5_bias.astype(F32), buckets)


def _attn_body(q_ref, k_ref, v_ref, bias_ref, o_ref,
               qf, kf, vf, qb, q4, k4, v4, qd, kd, vd, od, ld, o4, l4, og, lgs):
    seq = q_ref.shape[0]
    rows = 256
    lane = lax.broadcasted_iota(I32, (ATT_QBLK, LANES), 1)
    head0 = lane < ATT_HEAD_DIM

    def to_f32(c, carry):
        r = pl.ds(pl.multiple_of(c * rows, rows), rows)
        q = q_ref[r, :].astype(F32) * (ATT_HEAD_DIM ** -0.5)
        qf[r, :] = q
        qb[r, :] = q.astype(BF16)
        kf[r, :] = k_ref[r, :].astype(F32)
        vf[r, :] = v_ref[r, :].astype(F32)
        return carry

    lax.fori_loop(0, seq // rows, to_f32, 0)

    def band_blocks(g, length, n_seg, q_src, k_src, v_src, o_dst, l_dst):
        n_qb = length // ATT_QBLK
        n_blocks = n_seg * n_qb
        unroll = min(ATT_UNROLL, n_blocks)

        def qgroup(it, carry):
            for u in range(unroll):
                qblock(it * unroll + u)
            return carry

        def qblock(b):
            qi = b % n_qb
            base = (b // n_qb) * length
            qs = pl.multiple_of(base + qi * ATT_QBLK, ATT_QBLK)
            ws = pl.multiple_of(
                base + jnp.clip(qi * ATT_QBLK - ATT_RADIUS, 0, length - ATT_KWIN), ATT_RADIUS)
            case = jnp.where(qi == 0, 0, jnp.where(qi == n_qb - 1, 2, 1))
            q = q_src[pl.ds(qs, ATT_QBLK), :]
            k = k_src[pl.ds(ws, ATT_KWIN), :]
            v = v_src[pl.ds(ws, ATT_KWIN), :]
            v_ones = jnp.concatenate([v, jnp.ones_like(v)], axis=1)
            outs, lses = [], []
            for hh in range(2):
                mask = head0 if hh == 0 else jnp.logical_not(head0)
                qm = jnp.where(mask, q, jnp.zeros_like(q))
                s = lax.dot_general(qm, k, (((1,), (1,)), ((), ())), preferred_element_type=F32)
                s = s + bias_ref[g, hh, case]
                m = jnp.max(s, axis=-1, keepdims=True)
                p = jnp.exp(s - m).astype(BF16)
                ol = jnp.dot(p, v_ones, preferred_element_type=F32)
                l = ol[:, LANES:]
                outs.append(ol[:, :LANES] / l)
                lses.append(m + jnp.log(l))
            o_dst[pl.ds(qs, ATT_QBLK), :] = jnp.where(head0, outs[0], outs[1])
            l_dst[pl.ds(qs, ATT_QBLK), :] = jnp.where(head0, lses[0], lses[1])

        lax.fori_loop(0, n_blocks // unroll, qgroup, 0)

    coarse_len = seq // ATT_COARSE

    def split(c, carry):
        strided = pl.ds(c, coarse_len, stride=ATT_COARSE)
        q4[c] = qf[strided, :]
        k4[c] = kf[strided, :]
        v4[c] = vf[strided, :]
        return carry

    lax.fori_loop(0, ATT_COARSE, split, 0)

    for g, (_, dilation) in enumerate(DILATION_PATTERNS):
        length = seq // dilation
        if dilation == 1:
            band_blocks(g, length, 1, qb, k_ref, v_ref, og.at[g], lgs.at[g])
            continue
        assert dilation % ATT_COARSE == 0
        fine = dilation // ATT_COARSE
        assert fine * length == coarse_len and fine * (length // ATT_QBLK) >= min(ATT_UNROLL, fine)

        def coarse_classes(trip, carry, g=g, length=length, fine=fine):
            classes = [(trip * ATT_CLASSES + u, u * coarse_len) for u in range(ATT_CLASSES)]
            segs = [(j, pl.ds(j, length, stride=fine) if fine > 1 else pl.ds(0, length), j * length)
                    for j in range(fine)]
            for c, base in classes:
                for _, src, off in segs:
                    dense = pl.ds(base + off, length)
                    qd[dense, :] = q4[c, src, :].astype(BF16)
                    kd[dense, :] = k4[c, src, :].astype(BF16)
                    vd[dense, :] = v4[c, src, :].astype(BF16)
            band_blocks(g, length, ATT_CLASSES * fine, qd, kd, vd, od, ld)
            for c, base in classes:
                whole = pl.ds(c, coarse_len, stride=ATT_COARSE)
                staged = pl.ds(base, coarse_len)
                if fine == 1:
                    og[g, whole, :] = od[staged, :]
                    lgs[g, whole, :] = ld[staged, :]
                else:
                    for j, _, off in segs:
                        interleaved = pl.ds(base + j, length, stride=fine)
                        o4[interleaved, :] = od[pl.ds(base + off, length), :]
                        l4[interleaved, :] = ld[pl.ds(base + off, length), :]
                    og[g, whole, :] = o4[staged, :]
                    lgs[g, whole, :] = l4[staged, :]
            return carry

        lax.fori_loop(0, ATT_COARSE // ATT_CLASSES, coarse_classes, 0)

    def merge(c, carry):
        r = pl.ds(pl.multiple_of(c * rows, rows), rows)
        l0, l1, l2 = lgs[0, r, :], lgs[1, r, :], lgs[2, r, :]
        m = jnp.maximum(jnp.maximum(l0, l1), l2)
        w0, w1, w2 = jnp.exp(l0 - m), jnp.exp(l1 - m), jnp.exp(l2 - m)
        num = w0 * og[0, r, :] + w1 * og[1, r, :] + w2 * og[2, r, :]
        o_ref[r, :] = (num / (w0 + w1 + w2)).astype(BF16)
        return carry

    lax.fori_loop(0, seq // rows, merge, 0)


def _attention(proj3, bias_tab):
    bsz, seq, _ = proj3.shape
    n_pat = len(DILATION_PATTERNS)
    pairs = ATT_HEADS // 2
    base = 4 * RET_WIDTH // LANES

    def col(section):
        return pl.BlockSpec((None, seq, LANES),
                            lambda b, hp: (b, 0, base + section * (ATT_WIDTH // LANES) + hp))

    return pl.pallas_call(
        _attn_body,
        grid=(bsz, pairs),
        in_specs=[col(0), col(1), col(2),
                  pl.BlockSpec((n_pat, 2, 3, ATT_QBLK, ATT_KWIN), lambda b, hp: (0, hp, 0, 0, 0))],
        out_specs=pl.BlockSpec((None, seq, LANES), lambda b, hp: (b, 0, hp)),
        out_shape=jax.ShapeDtypeStruct((bsz, seq, ATT_WIDTH), BF16),
        scratch_shapes=[pltpu.VMEM((seq, LANES), F32)] * 3
                       + [pltpu.VMEM((seq, LANES), BF16)]
                       + [pltpu.VMEM((ATT_COARSE, seq // ATT_COARSE, LANES), F32)] * 3
                       + [pltpu.VMEM((ATT_CLASSES * seq // ATT_COARSE, LANES), BF16)] * 3
                       + [pltpu.VMEM((ATT_CLASSES * seq // ATT_COARSE, LANES), F32)] * 4
                       + [pltpu.VMEM((n_pat, seq, LANES), F32)] * 2,
        compiler_params=_cparams(("arbitrary", "arbitrary"), 56),
        name="attention",
    )(proj3, proj3, proj3, bias_tab)


def _mix_body(ret_ref, att_ref, gr_ref, ga_ref, x_ref, mod_ref, gain_ref,
              wr_ref, wa_ref, wo_ref, wrt_ref, x1_ref, hslab_ref, hrow_ref, logit_ref, slab_scr):
    tm = x_ref.shape[0]
    shift_f, scale_f, gate_m = mod_ref[0, 3:4, :], mod_ref[0, 4:5, :], mod_ref[0, 2:3, :]
    for r in range(tm // MIX_ROWS):
        rows = pl.ds(r * MIX_ROWS, MIX_ROWS)
        y_ret = jnp.dot(ret_ref[rows, :], wr_ref[...], preferred_element_type=F32)
        y_att = jnp.dot(att_ref[rows, :], wa_ref[...], preferred_element_type=F32)
        merged = (_sigmoid(gr_ref[rows, :].astype(F32)) * y_ret
                  + _sigmoid(ga_ref[rows, :].astype(F32)) * y_att)
        mixed = jnp.dot(merged.astype(BF16), wo_ref[...], preferred_element_type=F32)
        x1 = x_ref[rows, :] + gate_m * mixed
        x1_ref[rows, :] = x1
        h = _rms(x1, gain_ref[...]) * (1.0 + scale_f) + shift_f
        hrow_ref[rows, :] = h.astype(BF16)
        _rows_to_tiles(h, slab_scr, hslab_ref, first=r * MIX_ROWS)
        h_hi = h.astype(BF16)
        h_lo = (h - h_hi.astype(F32)).astype(BF16)
        logit_ref[rows, :] = (jnp.dot(h_hi, wrt_ref[0], preferred_element_type=F32)
                              + (jnp.dot(h_lo, wrt_ref[0], preferred_element_type=F32)
                                 + jnp.dot(h_hi, wrt_ref[1], preferred_element_type=F32)))


def _mix(ret2, att2, proj2, x2, mod3, gain, wr, wa, wo, w_router, seq, tm=512):
    t, d = x2.shape
    gate_base = (4 * RET_WIDTH + 3 * ATT_WIDTH) // d
    row = lambda i: (i, 0)
    const = lambda i: (0, 0)
    return pl.pallas_call(
        _mix_body,
        grid=(t // tm,),
        in_specs=[pl.BlockSpec((tm, d), row), pl.BlockSpec((tm, d), row),
                  pl.BlockSpec((tm, d), lambda i: (i, gate_base)),
                  pl.BlockSpec((tm, d), lambda i: (i, gate_base + 1)),
                  pl.BlockSpec((tm, d), row),
                  pl.BlockSpec((1, 6, d), lambda i: ((i * tm) // seq, 0, 0)),
                  pl.BlockSpec((1, d), const),
                  pl.BlockSpec((d, d), const), pl.BlockSpec((d, d), const),
                  pl.BlockSpec((d, d), const),
                  pl.BlockSpec((2, d, N_EXPERTS), lambda i: (0, 0, 0))],
        out_specs=[pl.BlockSpec((tm, d), row),
                   pl.BlockSpec((tm, SLABS, LANES), lambda i: (i, 0, 0)),
                   pl.BlockSpec((tm, d), row),
                   pl.BlockSpec((tm, N_EXPERTS), row)],
        out_shape=[jax.ShapeDtypeStruct((t, d), F32),
                   jax.ShapeDtypeStruct((t, SLABS, LANES), BF16),
                   jax.ShapeDtypeStruct((t, d), BF16),
                   jax.ShapeDtypeStruct((t, N_EXPERTS), F32)],
        scratch_shapes=[pltpu.VMEM((tm * SLABS, LANES), F32)],
        compiler_params=_cparams(("arbitrary",), 56),
        name="mix",
    )(ret2, att2, proj2, proj2, x2, mod3, gain, wr, wa, wo, w_router)


def _route_body(logit_ref, bias_ref, eidx_ref, gate_ref, rank_ref, cnt_ref, carry_scr, tri_scr):
    tn = logit_ref.shape[0]
    per_group = N_EXPERTS // N_GROUPS

    @pl.when(pl.program_id(0) == 0)
    def _():
        carry_scr[...] = jnp.zeros_like(carry_scr)
        r = lax.broadcasted_iota(I32, (tn, tn), 0)
        c = lax.broadcasted_iota(I32, (tn, tn), 1)
        tri_scr[...] = jnp.where(r < c, 1.0, 0.0).astype(BF16)

    scores = _sigmoid(logit_ref[...]).T
    choice = scores + jnp.concatenate([bias_ref[...]] * (tn // LANES), axis=1)
    neg_inf = -jnp.inf

    sub = lax.broadcasted_iota(I32, (per_group, tn), 0).astype(F32)
    group_score = []
    for g in range(N_GROUPS):
        cg = choice[g * per_group:(g + 1) * per_group, :]
        m1 = jnp.max(cg, axis=0, keepdims=True)
        first = jnp.min(jnp.where(cg == m1, sub, float(per_group)), axis=0, keepdims=True)
        m2 = jnp.max(jnp.where(sub == first, neg_inf, cg), axis=0, keepdims=True)
        group_score.append(m1 + m2)

    masked = []
    for a in range(N_GROUPS):
        beaten = jnp.zeros((1, tn), F32)
        for b in range(N_GROUPS):
            if b == a:
                continue
            wins = (group_score[b] >= group_score[a]) if b < a else (group_score[b] > group_score[a])
            beaten = beaten + jnp.where(wins, 1.0, 0.0)
        keep = beaten < float(TOPK_GROUPS)
        masked.append(jnp.where(keep, choice[a * per_group:(a + 1) * per_group, :], neg_inf))
    work = jnp.concatenate(masked, axis=0)

    eid = lax.broadcasted_iota(I32, (N_EXPERTS, tn), 0).astype(F32)
    picked = jnp.zeros((N_EXPERTS, tn), F32)
    idx_rows, gate_rows = [], []
    for _ in range(TOP_K):
        m = jnp.max(work, axis=0, keepdims=True)
        idx = jnp.min(jnp.where(work == m, eid, float(N_EXPERTS)), axis=0, keepdims=True)
        sel = eid == idx
        gate_rows.append(jnp.sum(jnp.where(sel, scores, 0.0), axis=0, keepdims=True))
        picked = picked + jnp.where(sel, 1.0, 0.0)
        work = jnp.where(sel, neg_inf, work)
        idx_rows.append(idx)

    before = (jnp.dot(picked.astype(BF16), tri_scr[...], preferred_element_type=F32)
              + carry_scr[:, 0:1])
    rank_rows = [jnp.sum(jnp.where(eid == idx, before, 0.0), axis=0, keepdims=True)
                 for idx in idx_rows]
    carry = carry_scr[...] + jnp.sum(picked, axis=1, keepdims=True)
    carry_scr[...] = carry
    cnt_ref[...] = carry

    gates = jnp.concatenate(gate_rows, axis=0)
    gates = gates / jnp.sum(gates, axis=0, keepdims=True) * ROUTED_SCALE
    eidx_ref[...] = jnp.concatenate(idx_rows, axis=0).astype(I32)
    gate_ref[...] = gates
    rank_ref[...] = jnp.concatenate(rank_rows, axis=0).astype(I32)


def _route(logits, bias_b, tn=512):
    t = logits.shape[0]
    tok = lambda i: (0, i)
    return pl.pallas_call(
        _route_body,
        grid=(t // tn,),
        in_specs=[pl.BlockSpec((tn, N_EXPERTS), lambda i: (i, 0)),
                  pl.BlockSpec((N_EXPERTS, LANES), lambda i: (0, 0))],
        out_specs=[pl.BlockSpec((TOP_K, tn), tok), pl.BlockSpec((TOP_K, tn), tok),
                   pl.BlockSpec((TOP_K, tn), tok),
                   pl.BlockSpec((N_EXPERTS, LANES), lambda i: (0, 0))],
        out_shape=[jax.ShapeDtypeStruct((TOP_K, t), I32), jax.ShapeDtypeStruct((TOP_K, t), F32),
                   jax.ShapeDtypeStruct((TOP_K, t), I32),
                   jax.ShapeDtypeStruct((N_EXPERTS, LANES), F32)],
        scratch_shapes=[pltpu.VMEM((N_EXPERTS, LANES), F32), pltpu.VMEM((tn, tn), BF16)],
        compiler_params=_cparams(("arbitrary",), 48),
        name="route",
    )(logits, bias_b)


def _dest_body(start_ref, eidx_ref, rank_ref, dest_ref):
    e = eidx_ref[...]

    def body(j, acc):
        return jnp.where(e == j, start_ref[j], acc)

    dest_ref[...] = rank_ref[...] + lax.fori_loop(0, N_EXPERTS, body, jnp.zeros_like(e))


def _dest(pad_start, eidx, rank, tn=2048):
    t = eidx.shape[1]
    tok = lambda i, s: (0, i)
    return pl.pallas_call(
        _dest_body,
        grid_spec=pltpu.PrefetchScalarGridSpec(
            num_scalar_prefetch=1,
            grid=(t // tn,),
            in_specs=[pl.BlockSpec((TOP_K, tn), tok), pl.BlockSpec((TOP_K, tn), tok)],
            out_specs=pl.BlockSpec((TOP_K, tn), tok)),
        out_shape=jax.ShapeDtypeStruct((TOP_K, t), I32),
        compiler_params=_cparams(("arbitrary",), 32),
        name="dest",
    )(pad_start, eidx, rank)


def _dispatch_body(dest_hbm, h_ref, xs_hbm, dest_smem, idx_sem, row_sem):
    tt = dest_smem.shape[0] // TOP_K
    idx_copy = pltpu.make_async_copy(dest_hbm.at[pl.program_id(0)], dest_smem, idx_sem)
    idx_copy.start()
    idx_copy.wait()

    def issue(t, carry):
        for k in range(TOP_K):
            pltpu.make_async_copy(h_ref.at[t], xs_hbm.at[dest_smem[t * TOP_K + k]],
                                  row_sem).start(priority=k % 2)
        return carry

    lax.fori_loop(0, tt, issue, 0)

    for k in range(TOP_K):
        pltpu.make_async_copy(h_ref, xs_hbm.at[pl.ds(0, tt)], row_sem).wait()


def _dispatch(dest_tiles, htiles, n_rows):
    n_tiles, width = dest_tiles.shape
    tt = width // TOP_K
    return pl.pallas_call(
        _dispatch_body,
        grid=(n_tiles,),
        in_specs=[pl.BlockSpec(memory_space=pl.ANY),
                  pl.BlockSpec((tt, SLABS, LANES), lambda i: (i, 0, 0))],
        out_specs=pl.BlockSpec(memory_space=pl.ANY),
        out_shape=jax.ShapeDtypeStruct((n_rows, SLABS, LANES), BF16),
        scratch_shapes=[pltpu.SMEM((width,), I32), pltpu.SemaphoreType.DMA, pltpu.SemaphoreType.DMA],
        compiler_params=_cparams(("arbitrary",), 32),
        name="dispatch",
    )(dest_tiles, htiles)


def _experts_body(bexp_ref, bnew_ref, nused_ref, bnext_ref, bslot_ref, xs_hbm, wg_hbm, wu_hbm, wd_hbm,
                  ys_ref, wgu_s, wd_s, stage_scr, xring, wg_buf, wu_buf, wd_buf, ring_sem, w_sem):
    i = pl.program_id(0)
    hid = wd_s.shape[0]
    n_used = nused_ref[0]

    def block_copy(j):
        slot = j % EXPERT_RING
        src = xs_hbm.at[pl.ds(pl.multiple_of(j * MOE_BLOCK, MOE_BLOCK), MOE_BLOCK)]
        return pltpu.make_async_copy(src, xring.at[slot], ring_sem.at[slot])

    def weight_copies(expert, slot):
        return [pltpu.make_async_copy(src.at[expert], dst.at[slot], w_sem.at[slot])
                for src, dst in ((wg_hbm, wg_buf), (wu_hbm, wu_buf), (wd_hbm, wd_buf))]

    @pl.when(i == 0)
    def _():
        for copy in weight_copies(bexp_ref[0], 0):
            copy.start()
        for j in range(EXPERT_RING - 1):
            @pl.when(j < n_used)
            def _(j=j):
                block_copy(j).start()

    @pl.when(i + EXPERT_RING - 1 < n_used)
    def _():
        block_copy(i + EXPERT_RING - 1).start()

    @pl.when(i < n_used)
    def _():
        @pl.when(bnew_ref[i] == 1)
        def _():
            slot = bslot_ref[i]
            for copy in weight_copies(bexp_ref[i], slot):
                copy.wait()
            nxt = bnext_ref[i]

            @pl.when(nxt >= 0)
            def _():
                for copy in weight_copies(nxt, 1 - slot):
                    copy.start()

            wgu_s[:, :hid] = wg_buf[slot].astype(BF16)
            wgu_s[:, hid:] = wu_buf[slot].astype(BF16)
            wd_s[...] = wd_buf[slot].astype(BF16)

        block_copy(i).wait()
        x = _tiles_to_rows(xring[i % EXPERT_RING], stage_scr).astype(BF16)
        gu = jnp.dot(x, wgu_s[...], preferred_element_type=F32)
        hg, hu = gu[:, :hid], gu[:, hid:]
        y = jnp.dot((_silu(hg) * hu).astype(BF16), wd_s[...], preferred_element_type=F32)
        _rows_to_tiles(y, stage_scr, ys_ref)


def _experts(block_expert, block_new, n_used, block_next, block_slot, xs, w_gate, w_up, w_down):
    n_blocks = block_expert.shape[0]
    d, hid = w_gate.shape[1], w_gate.shape[2]
    tile_block = (MOE_BLOCK, SLABS, LANES)
    hbm = pl.BlockSpec(memory_space=pl.ANY)

    def blk(i, be, bn, nu, bx, bs):
        return (jnp.minimum(i, nu[0] - 1), 0, 0)

    return pl.pallas_call(
        _experts_body,
        grid_spec=pltpu.PrefetchScalarGridSpec(
            num_scalar_prefetch=5,
            grid=(n_blocks,),
            in_specs=[hbm, hbm, hbm, hbm],
            out_specs=pl.BlockSpec(tile_block, blk),
            scratch_shapes=[pltpu.VMEM((d, 2 * hid), BF16), pltpu.VMEM((hid, d), BF16),
                            pltpu.VMEM((MOE_BLOCK * SLABS, LANES), F32),
                            pltpu.VMEM((EXPERT_RING,) + tile_block, BF16),
                            pltpu.VMEM((2, d, hid), F32), pltpu.VMEM((2, d, hid), F32),
                            pltpu.VMEM((2, hid, d), F32),
                            pltpu.SemaphoreType.DMA((EXPERT_RING,)),
                            pltpu.SemaphoreType.DMA((2,))]),
        out_shape=jax.ShapeDtypeStruct(xs.shape, BF16),
        compiler_params=_cparams(("arbitrary",), 32),
        name="experts",
    )(block_expert, block_new, n_used, block_next, block_slot, xs, w_gate, w_up, w_down)


def _combine_body(dest_hbm, gate_hbm, ys_hbm, x1_ref, h_ref, wsg_ref, wsu_ref, wsd_ref,
                  mod_ref, gain_ref, o_ref, dest_smem0, dest_smem1, gate_smem, buf, shared_scr,
                  routed_scr, idx_sem, gate_sem, row_sem, *, final_norm):
    tc = x1_ref.shape[0]
    i = pl.program_id(0)
    has_next = i + 1 < pl.num_programs(0)
    cur = i % 2
    dest_smem = (dest_smem0, dest_smem1)

    def index_copy(tile, slot):
        return pltpu.make_async_copy(dest_hbm.at[tile], dest_smem[slot], idx_sem)

    def start_rows(slot):
        def issue(t, carry):
            for k in range(TOP_K):
                pltpu.make_async_copy(ys_hbm.at[dest_smem[slot][t * TOP_K + k]],
                                      buf.at[slot, t * TOP_K + k],
                                      row_sem.at[slot]).start(priority=k % 2)
            return carry

        lax.fori_loop(0, tc, issue, 0)

    gate_copy = pltpu.make_async_copy(gate_hbm.at[i], gate_smem, gate_sem)
    gate_copy.start()

    @pl.when(i == 0)
    def _():
        index_copy(0, 0).start()
        index_copy(0, 0).wait()
        start_rows(0)

    for slot in range(2):
        @pl.when(jnp.logical_and(has_next, cur != slot))
        def _(slot=slot):
            index_copy(i + 1, slot).start()

    h = h_ref[...]
    hid = (_silu(jnp.dot(h, wsg_ref[...], preferred_element_type=F32))
           * jnp.dot(h, wsu_ref[...], preferred_element_type=F32))
    shared_scr[...] = jnp.dot(hid.astype(BF16), wsd_ref[...], preferred_element_type=F32)

    for slot in range(2):
        @pl.when(jnp.logical_and(has_next, cur != slot))
        def _(slot=slot):
            index_copy(i + 1, slot).wait()
            start_rows(slot)

    pltpu.make_async_copy(ys_hbm.at[pl.ds(0, tc * TOP_K)], buf.at[cur], row_sem.at[cur]).wait()
    gate_copy.wait()

    def weigh(g, carry):
        for u in range(COMBINE_UNROLL):
            t = g * COMBINE_UNROLL + u
            acc = gate_smem[t * TOP_K] * buf[cur, t * TOP_K].astype(F32)
            for k in range(1, TOP_K):
                acc = acc + gate_smem[t * TOP_K + k] * buf[cur, t * TOP_K + k].astype(F32)
            routed_scr[pl.ds(pl.multiple_of(t * SLABS, SLABS), SLABS), :] = acc
        return carry

    lax.fori_loop(0, tc // COMBINE_UNROLL, weigh, 0)

    gate_f = mod_ref[0, 5:6, :]
    gain = gain_ref[...]

    def finish(c, carry):
        r0 = pl.multiple_of(c * COMBINE_ROWS, COMBINE_ROWS)
        rows = pl.ds(r0, COMBINE_ROWS)
        routed = jnp.concatenate(
            [routed_scr[pl.ds(r0 * SLABS + s, COMBINE_ROWS, stride=SLABS), :]
             for s in range(SLABS)], axis=1)
        x2 = x1_ref[rows, :] + gate_f * (shared_scr[rows, :] + routed)
        o_ref[rows, :] = _rms(x2, gain) if final_norm else x2
        return carry

    lax.fori_loop(0, tc // COMBINE_ROWS, finish, 0)


def _combine(dest_tiles, gate_tiles, ys, x1, hrow, wsg, wsu, wsd, mod3, gain, seq, tc, final_norm):
    t, d = x1.shape
    hid = wsg.shape[1]
    row = lambda i: (i, 0)
    const = lambda i: (0, 0)
    return pl.pallas_call(
        functools.partial(_combine_body, final_norm=final_norm),
        grid=(t // tc,),
        in_specs=[pl.BlockSpec(memory_space=pl.ANY),
                  pl.BlockSpec(memory_space=pl.ANY),
                  pl.BlockSpec(memory_space=pl.ANY),
                  pl.BlockSpec((tc, d), row), pl.BlockSpec((tc, d), row),
                  pl.BlockSpec((d, hid), const), pl.BlockSpec((d, hid), const),
                  pl.BlockSpec((hid, d), const),
                  pl.BlockSpec((1, 6, d), lambda i: ((i * tc) // seq, 0, 0)),
                  pl.BlockSpec((1, d), const)],
        out_specs=pl.BlockSpec((tc, d), row),
        out_shape=jax.ShapeDtypeStruct((t, d), F32),
        scratch_shapes=[pltpu.SMEM((TOP_K * tc,), I32), pltpu.SMEM((TOP_K * tc,), I32),
                        pltpu.SMEM((TOP_K * tc,), F32),
                        pltpu.VMEM((2, tc * TOP_K, SLABS, LANES), BF16),
                        pltpu.VMEM((tc, d), F32),
                        pltpu.VMEM((tc * SLABS, LANES), F32),
                        pltpu.SemaphoreType.DMA, pltpu.SemaphoreType.DMA,
                        pltpu.SemaphoreType.DMA((2,))],
        compiler_params=_cparams(("arbitrary",), 56),
        name="combine",
    )(dest_tiles, gate_tiles, ys, x1, hrow, wsg, wsu, wsd, mod3, gain)


def _split_bf16(w):
    hi = w.astype(BF16)
    lo = (w.astype(F32) - hi.astype(F32)).astype(BF16)
    return jnp.stack([hi, lo], axis=0)


def _tile_major(a, tile):
    k, t = a.shape
    return a.reshape(k, t // tile, tile).transpose(1, 2, 0).reshape(t // tile, tile * k)


def kernel(x, c, w_ada, b_ada, norm_mix, w_in, ret_decay, t5_bias, w_ret_up, w_att_up, w_o,
           norm_ffn, w_router, router_bias, w_gate, w_up, w_down, ws_gate, ws_up, ws_down, norm_final):
    bsz, seq, d = x.shape
    depth = w_ada.shape[0]
    t = bsz * seq
    assert d == D_MODEL and seq % (ATT_KWIN * DILATION_PATTERNS[-1][1]) == 0

    half = RET_HEAD_DIM // 2
    inv_freq = ROPE_BASE ** (-jnp.arange(half, dtype=F32) / half)
    ang = jnp.arange(seq, dtype=F32)[:, None] * inv_freq[None, :]
    cos, sin = jnp.cos(ang), jnp.sin(ang)
    bias_tab = _attention_bias(t5_bias)

    n_assign = t * TOP_K
    n_blocks = -(-n_assign // MOE_BLOCK) + N_EXPERTS
    n_rows = n_blocks * MOE_BLOCK
    disp_tile = min(1024, t)
    comb_tile = min(256, t)

    x2 = x.reshape(t, d)
    for layer in range(depth):
        mod3 = _adaln(c, w_ada[layer], b_ada[layer]).reshape(bsz, 6, d)
        proj = _inproj(x2, norm_mix[layer].reshape(1, d), mod3, w_in[layer].astype(BF16), seq)
        proj3 = proj.reshape(bsz, seq, PROJ_WIDTH)
        log_gamma = jnp.log1p(-jnp.exp(ret_decay[layer].astype(F32)))
        ret = _retention(proj3, log_gamma, cos, sin)
        att = _attention(proj3, bias_tab)
        x1, hslab, hrow, logits = _mix(
            ret.reshape(t, RET_WIDTH), att.reshape(t, ATT_WIDTH), proj, x2, mod3,
            norm_ffn[layer].reshape(1, d), w_ret_up[layer].astype(BF16),
            w_att_up[layer].astype(BF16), w_o[layer].astype(BF16), _split_bf16(w_router[layer]),
            seq)

        bias_b = jnp.broadcast_to(router_bias[layer].astype(F32)[:, None], (N_EXPERTS, LANES))
        eidx, gate, rank, counts = _route(logits, bias_b)

        counts = counts[:, 0].astype(I32)
        padded = (counts + MOE_BLOCK - 1) // MOE_BLOCK * MOE_BLOCK
        pad_end = jnp.cumsum(padded)
        pad_start = pad_end - padded
        block_row = jnp.arange(n_blocks, dtype=I32) * MOE_BLOCK
        block_expert = jnp.minimum(
            jnp.sum((pad_end[None, :] <= block_row[:, None]).astype(I32), axis=1), N_EXPERTS - 1)
        block_new = jnp.concatenate(
            [jnp.ones((1,), I32), (block_expert[1:] != block_expert[:-1]).astype(I32)])
        n_used = (pad_end[-1:] // MOE_BLOCK).astype(I32)

        dest = _dest(pad_start.astype(I32), eidx, rank)
        xs = _dispatch(_tile_major(dest, disp_tile), hslab, n_rows)
        expert_ids = jnp.arange(N_EXPERTS, dtype=I32)
        later = jnp.logical_and(expert_ids[None, :] > block_expert[:, None], (padded > 0)[None, :])
        block_next = jnp.min(jnp.where(later, expert_ids[None, :], N_EXPERTS), axis=1)
        block_next = jnp.where(block_next == N_EXPERTS, -1, block_next).astype(I32)
        block_slot = ((jnp.cumsum(block_new) - 1) % 2).astype(I32)
        ys = _experts(block_expert, block_new, n_used, block_next, block_slot, xs,
                      w_gate[layer], w_up[layer], w_down[layer])
        x2 = _combine(_tile_major(dest, comb_tile), _tile_major(gate, comb_tile), ys,
                      x1, hrow, ws_gate[layer].astype(BF16), ws_up[layer].astype(BF16),
                      ws_down[layer].astype(BF16), mod3, norm_final.reshape(1, d), seq, comb_tile,
                      final_norm=(layer == depth - 1))
    return x2.reshape(bsz, seq, d)
```

```python
import functools
import math

import jax
import jax.numpy as jnp
import numpy as np
from jax import lax
from jax.experimental import pallas as pl
from jax.experimental.pallas import tpu as pltpu

F32 = jnp.float32
BF16 = jnp.bfloat16
I32 = jnp.int32

D_MODEL = 1024
RET_HEADS = 4
RET_HEAD_DIM = 256
RET_WIDTH = RET_HEADS * RET_HEAD_DIM
RET_CHUNK = 128
ROPE_BASE = 10000.0
ATT_HEADS = 16
ATT_HEAD_DIM = 64
ATT_WIDTH = ATT_HEADS * ATT_HEAD_DIM
DILATION_PATTERNS = ((128, 1), (512, 4), (2048, 16))
REL_BUCKETS = 32
REL_MAX_DISTANCE = 1024
N_EXPERTS = 256
TOP_K = 8
N_GROUPS = 8
TOPK_GROUPS = 4
EXPERT_HIDDEN = 256
ROUTED_SCALE = 2.5
MOE_BLOCK = 256
NORM_EPS = 1e-6
PROJ_WIDTH = 4 * RET_WIDTH + 3 * ATT_WIDTH + 2 * D_MODEL

LANES = 128
SUBLANES = 8
SLABS = D_MODEL // LANES
ATT_RADIUS = 64
ATT_QBLK = 128
ATT_KWIN = 256
MIX_ROWS = 128
RET_UNROLL = 8
EXPERT_RING = 3
COMBINE_ROWS = 32
COMBINE_UNROLL = 8
ATT_COARSE = 4
ATT_CLASSES = 4
ATT_UNROLL = 32
NEG_BIG = -1e30
MIB = 1024 * 1024


def _cparams(sem, vmem_mib):
    return pltpu.CompilerParams(dimension_semantics=sem, vmem_limit_bytes=vmem_mib * MIB)


def _sigmoid(x):
    return 1.0 / (1.0 + jnp.exp(-x))


def _silu(x):
    return x * _sigmoid(x)


def _rms(x, gain):
    return x * lax.rsqrt(jnp.mean(x * x, axis=-1, keepdims=True) + NORM_EPS) * gain


def _rows_to_tiles(rows, stage_scr, tiles_ref, first=0):
    n = rows.shape[0]
    base = first * SLABS
    for s in range(SLABS):
        stage_scr[pl.ds(base + s, n, stride=SLABS), :] = rows[:, s * LANES:(s + 1) * LANES]
    staged = stage_scr[pl.ds(base, n * SLABS), :]
    tiles_ref[pl.ds(first, n)] = staged.reshape(n, SLABS, LANES).astype(BF16)


def _tiles_to_rows(tiles, stage_scr):
    n = tiles.shape[0]
    stage_scr[...] = tiles.astype(F32).reshape(n * SLABS, LANES)
    return jnp.concatenate([stage_scr[pl.ds(s, n, stride=SLABS), :] for s in range(SLABS)], axis=1)


def _adaln_body(c_ref, w_ref, b_ref, o_ref):
    cond = _silu(c_ref[...])
    o_ref[...] = jnp.dot(cond, w_ref[...], preferred_element_type=F32,
                         precision=lax.Precision.HIGHEST) + b_ref[...]


def _adaln(c, w, b):
    bsz, d = c.shape
    n = w.shape[1]
    return pl.pallas_call(
        _adaln_body,
        grid=(n // d,),
        in_specs=[pl.BlockSpec((bsz, d), lambda j: (0, 0)),
                  pl.BlockSpec((d, d), lambda j: (0, j)),
                  pl.BlockSpec((1, d), lambda j: (0, j))],
        out_specs=pl.BlockSpec((bsz, d), lambda j: (0, j)),
        out_shape=jax.ShapeDtypeStruct((bsz, n), F32),
        compiler_params=_cparams(("arbitrary",), 32),
        name="adaln",
    )(c, w, b.reshape(1, n))


def _inproj_body(x_ref, gain_ref, mod_ref, w_ref, o_ref, h_scr):
    @pl.when(pl.program_id(1) == 0)
    def _():
        y = _rms(x_ref[...], gain_ref[...])
        h = y * (1.0 + mod_ref[0, 1:2, :]) + mod_ref[0, 0:1, :]
        h_scr[...] = h.astype(BF16)

    o_ref[...] = jnp.dot(h_scr[...], w_ref[...], preferred_element_type=F32).astype(BF16)


def _inproj(x2, gain, mod3, w_bf, seq, tm=1024, tn=4608):
    t, d = x2.shape
    n = w_bf.shape[1]
    return pl.pallas_call(
        _inproj_body,
        grid=(t // tm, n // tn),
        in_specs=[pl.BlockSpec((tm, d), lambda i, j: (i, 0)),
                  pl.BlockSpec((1, d), lambda i, j: (0, 0)),
                  pl.BlockSpec((1, 6, d), lambda i, j: ((i * tm) // seq, 0, 0)),
                  pl.BlockSpec((d, tn), lambda i, j: (0, j))],
        out_specs=pl.BlockSpec((tm, tn), lambda i, j: (i, j)),
        out_shape=jax.ShapeDtypeStruct((t, n), BF16),
        scratch_shapes=[pltpu.VMEM((tm, d), BF16)],
        compiler_params=_cparams(("arbitrary", "arbitrary"), 56),
        name="inproj",
    )(x2, gain, mod3, w_bf)


def _ret_body(lg_ref, q_ref, k_ref, v_ref, g_ref, cos_ref, sin_ref, o_ref,
              qr_scr, kr_scr, accf_scr, accb_scr, stf_scr, stb_scr):
    head = pl.program_id(1)
    seq = q_ref.shape[0]
    n_chunks = seq // RET_CHUNK
    half = RET_HEAD_DIM // 2
    rot_rows = 256

    def rot_step(c, carry):
        r = pl.ds(pl.multiple_of(c * rot_rows, rot_rows), rot_rows)
        cs = cos_ref[r, :]
        sn = sin_ref[r, :]
        for src, dst, scale in ((q_ref, qr_scr, 1.0), (k_ref, kr_scr, RET_HEAD_DIM ** -0.5)):
            t = src[r, :].astype(F32)
            t1, t2 = t[:, :half], t[:, half:]
            dst[r, :half] = ((t1 * cs - t2 * sn) * scale).astype(BF16)
            dst[r, half:] = ((t1 * sn + t2 * cs) * scale).astype(BF16)
        return carry

    lax.fori_loop(0, seq // rot_rows, rot_step, 0)

    ri = lax.broadcasted_iota(I32, (RET_CHUNK, RET_CHUNK), 0)
    ci = lax.broadcasted_iota(I32, (RET_CHUNK, RET_CHUNK), 1)
    rowpos = lax.broadcasted_iota(I32, (RET_CHUNK, RET_HEAD_DIM), 0).astype(F32)

    def decay_tables(lg, forward):
        if forward:
            diff = (ri - ci).astype(F32)
            allowed = ri >= ci
            q_scale = jnp.exp(lg * (rowpos + 1.0))
            k_scale = jnp.exp(lg * (RET_CHUNK - 1.0 - rowpos))
        else:
            diff = (ci - ri).astype(F32)
            allowed = ci > ri
            q_scale = jnp.exp(lg * (RET_CHUNK - rowpos))
            k_scale = jnp.exp(lg * rowpos)
        intra = jnp.where(allowed, jnp.exp(lg * jnp.where(allowed, diff, 0.0)), 0.0)
        chunk_decay = jnp.exp(lg * jnp.full((1, RET_HEAD_DIM), float(RET_CHUNK), F32))
        return intra, q_scale, k_scale, chunk_decay

    def chunk_update(c, tables, st_scr, acc_scr):
        intra, q_scale, k_scale, chunk_decay = tables
        r = pl.ds(pl.multiple_of(c * RET_CHUNK, RET_CHUNK), RET_CHUNK)
        q = qr_scr[r, :]
        k = kr_scr[r, :]
        v = v_ref[r, :]
        scores = lax.dot_general(q, k, (((1,), (1,)), ((), ())),
                                 preferred_element_type=F32) * intra
        state = st_scr[...]
        acc_scr[r, :] = (jnp.dot(scores.astype(BF16), v, preferred_element_type=F32)
                         + jnp.dot((q.astype(F32) * q_scale).astype(BF16), state.astype(BF16),
                                   preferred_element_type=F32))
        k_t = (k.astype(F32) * k_scale).T.astype(BF16)
        st_scr[...] = state * chunk_decay + jnp.dot(k_t, v, preferred_element_type=F32)

    fwd_tables = decay_tables(lg_ref[0, head], True)
    bwd_tables = decay_tables(lg_ref[1, head], False)
    stf_scr[...] = jnp.zeros_like(stf_scr)
    stb_scr[...] = jnp.zeros_like(stb_scr)

    def step(trip, carry):
        for u in range(RET_UNROLL):
            i = trip * RET_UNROLL + u
            chunk_update(i, fwd_tables, stf_scr, accf_scr)
            chunk_update(n_chunks - 1 - i, bwd_tables, stb_scr, accb_scr)
        return carry

    lax.fori_loop(0, n_chunks // RET_UNROLL, step, 0)

    def finish(c, carry):
        r = pl.ds(pl.multiple_of(c * rot_rows, rot_rows), rot_rows)
        o = accf_scr[r, :] + accb_scr[r, :]
        mu = jnp.mean(o, axis=-1, keepdims=True)
        oc = o - mu
        var = jnp.mean(oc * oc, axis=-1, keepdims=True)
        o = oc * lax.rsqrt(var + NORM_EPS)
        o_ref[r, :] = (o * _silu(g_ref[r, :].astype(F32))).astype(BF16)
        return carry

    lax.fori_loop(0, seq // rot_rows, finish, 0)


def _retention(proj3, log_gamma, cos, sin):
    bsz, seq, _ = proj3.shape
    hd = RET_HEAD_DIM

    def col(section):
        return pl.BlockSpec((None, seq, hd), lambda b, h, lg: (b, 0, section * RET_HEADS + h))

    return pl.pallas_call(
        _ret_body,
        grid_spec=pltpu.PrefetchScalarGridSpec(
            num_scalar_prefetch=1,
            grid=(bsz, RET_HEADS),
            in_specs=[col(0), col(1), col(2), col(3),
                      pl.BlockSpec((seq, hd // 2), lambda b, h, lg: (0, 0)),
                      pl.BlockSpec((seq, hd // 2), lambda b, h, lg: (0, 0))],
            out_specs=pl.BlockSpec((None, seq, hd), lambda b, h, lg: (b, 0, h)),
            scratch_shapes=[pltpu.VMEM((seq, hd), BF16), pltpu.VMEM((seq, hd), BF16),
                            pltpu.VMEM((seq, hd), F32), pltpu.VMEM((seq, hd), F32),
                            pltpu.VMEM((hd, hd), F32), pltpu.VMEM((hd, hd), F32)]),
        out_shape=jax.ShapeDtypeStruct((bsz, seq, RET_WIDTH), BF16),
        compiler_params=_cparams(("arbitrary", "arbitrary"), 56),
        name="retention",
    )(log_gamma, proj3, proj3, proj3, proj3, cos, sin)


def _t5_bucket(rel):
    half = REL_BUCKETS // 2
    max_exact = half // 2
    n = jnp.abs(rel)
    large = max_exact + (jnp.log(jnp.maximum(n, 1).astype(F32) / max_exact)
                         / math.log(REL_MAX_DISTANCE / max_exact) * (half - max_exact)).astype(I32)
    large = jnp.minimum(large, half - 1)
    return jnp.where(rel > 0, half, 0) + jnp.where(n < max_exact, n, large)


def _band_buckets():
    qi = jnp.arange(ATT_QBLK, dtype=I32)[:, None]
    kj = jnp.arange(ATT_KWIN, dtype=I32)[None, :]
    tables = []
    for _, dilation in DILATION_PATTERNS:
        cases = []
        for offset in (0, -ATT_RADIUS, ATT_QBLK - ATT_KWIN):
            rel = kj + offset - qi
            cases.append(jnp.where(jnp.abs(rel) <= ATT_RADIUS, _t5_bucket(rel * dilation), -1))
        tables.append(jnp.stack(cases, axis=0))
    return jnp.stack(tables, axis=0)


def _bias_body(t5_ref, bucket_ref, o_ref):
    bucket = bucket_ref[...]

    def head(h, carry):
        acc = jnp.full(bucket.shape, NEG_BIG, F32)
        for b in range(REL_BUCKETS):
            acc = jnp.where(bucket == b, t5_ref[b, h], acc)
        o_ref[h] = acc
        return carry

    lax.fori_loop(0, ATT_HEADS, head, 0)


def _attention_bias(t5_bias):
    buckets = _band_buckets()
    n_pat, n_case = buckets.shape[:2]
    return pl.pallas_call(
        _bias_body,
        grid_spec=pltpu.PrefetchScalarGridSpec(
            num_scalar_prefetch=1,
            grid=(n_pat, n_case),
            in_specs=[pl.BlockSpec((None, None, ATT_QBLK, ATT_KWIN), lambda p, c, t5: (p, c, 0, 0))],
            out_specs=pl.BlockSpec((None, ATT_HEADS, None, ATT_QBLK, ATT_KWIN),
                                   lambda p, c, t5: (p, 0, c, 0, 0))),
        out_shape=jax.ShapeDtypeStruct((n_pat, ATT_HEADS, n_case, ATT_QBLK, ATT_KWIN), F32),
        compiler_params=_cparams(("arbitrary", "arbitrary"), 32),
        name="attn_bias",
    )(t5_bias.astype(F32), buckets)


def _attn_body(q_ref, k_ref, v_ref, bias_ref, o_ref,
               qf, kf, vf, qb, q4, k4, v4, qd, kd, vd, od, ld, o4, l4, og, lgs):
    seq = q_ref.shape[0]
    rows = 256
    lane = lax.broadcasted_iota(I32, (ATT_QBLK, LANES), 1)
    head0 = lane < ATT_HEAD_DIM

    def to_f32(c, carry):
        r = pl.ds(pl.multiple_of(c * rows, rows), rows)
        q = q_ref[r, :].astype(F32) * (ATT_HEAD_DIM ** -0.5)
        qf[r, :] = q
        qb[r, :] = q.astype(BF16)
        kf[r, :] = k_ref[r, :].astype(F32)
        vf[r, :] = v_ref[r, :].astype(F32)
        return carry

    lax.fori_loop(0, seq // rows, to_f32, 0)

    def band_blocks(g, length, n_seg, q_src, k_src, v_src, o_dst, l_dst):
        n_qb = length // ATT_QBLK
        n_blocks = n_seg * n_qb
        unroll = min(ATT_UNROLL, n_blocks)

        def qgroup(it, carry):
            for u in range(unroll):
                qblock(it * unroll + u)
            return carry

        def qblock(b):
            qi = b % n_qb
            base = (b // n_qb) * length
            qs = pl.multiple_of(base + qi * ATT_QBLK, ATT_QBLK)
            ws = pl.multiple_of(
                base + jnp.clip(qi * ATT_QBLK - ATT_RADIUS, 0, length - ATT_KWIN), ATT_RADIUS)
            case = jnp.where(qi == 0, 0, jnp.where(qi == n_qb - 1, 2, 1))
            q = q_src[pl.ds(qs, ATT_QBLK), :]
            k = k_src[pl.ds(ws, ATT_KWIN), :]
            v = v_src[pl.ds(ws, ATT_KWIN), :]
            v_ones = jnp.concatenate([v, jnp.ones_like(v)], axis=1)
            outs, lses = [], []
            for hh in range(2):
                mask = head0 if hh == 0 else jnp.logical_not(head0)
                qm = jnp.where(mask, q, jnp.zeros_like(q))
                s = lax.dot_general(qm, k, (((1,), (1,)), ((), ())), preferred_element_type=F32)
                s = s + bias_ref[g, hh, case]
                m = jnp.max(s, axis=-1, keepdims=True)
                p = jnp.exp(s - m).astype(BF16)
                ol = jnp.dot(p, v_ones, preferred_element_type=F32)
                l = ol[:, LANES:]
                outs.append(ol[:, :LANES] / l)
                lses.append(m + jnp.log(l))
            o_dst[pl.ds(qs, ATT_QBLK), :] = jnp.where(head0, outs[0], outs[1])
            l_dst[pl.ds(qs, ATT_QBLK), :] = jnp.where(head0, lses[0], lses[1])

        lax.fori_loop(0, n_blocks // unroll, qgroup, 0)

    coarse_len = seq // ATT_COARSE

    def split(c, carry):
        strided = pl.ds(c, coarse_len, stride=ATT_COARSE)
        q4[c] = qf[strided, :]
        k4[c] = kf[strided, :]
        v4[c] = vf[strided, :]
        return carry

    lax.fori_loop(0, ATT_COARSE, split, 0)

    for g, (_, dilation) in enumerate(DILATION_PATTERNS):
        length = seq // dilation
        if dilation == 1:
            band_blocks(g, length, 1, qb, k_ref, v_ref, og.at[g], lgs.at[g])
            continue
        assert dilation % ATT_COARSE == 0
        fine = dilation // ATT_COARSE
        assert fine * length == coarse_len and fine * (length // ATT_QBLK) >= min(ATT_UNROLL, fine)

        def coarse_classes(trip, carry, g=g, length=length, fine=fine):
            classes = [(trip * ATT_CLASSES + u, u * coarse_len) for u in range(ATT_CLASSES)]
            segs = [(j, pl.ds(j, length, stride=fine) if fine > 1 else pl.ds(0, length), j * length)
                    for j in range(fine)]
            for c, base in classes:
                for _, src, off in segs:
                    dense = pl.ds(base + off, length)
                    qd[dense, :] = q4[c, src, :].astype(BF16)
                    kd[dense, :] = k4[c, src, :].astype(BF16)
                    vd[dense, :] = v4[c, src, :].astype(BF16)
            band_blocks(g, length, ATT_CLASSES * fine, qd, kd, vd, od, ld)
            for c, base in classes:
                whole = pl.ds(c, coarse_len, stride=ATT_COARSE)
                staged = pl.ds(base, coarse_len)
                if fine == 1:
                    og[g, whole, :] = od[staged, :]
                    lgs[g, whole, :] = ld[staged, :]
                else:
                    for j, _, off in segs:
                        interleaved = pl.ds(base + j, length, stride=fine)
                        o4[interleaved, :] = od[pl.ds(base + off, length), :]
                        l4[interleaved, :] = ld[pl.ds(base + off, length), :]
                    og[g, whole, :] = o4[staged, :]
                    lgs[g, whole, :] = l4[staged, :]
            return carry

        lax.fori_loop(0, ATT_COARSE // ATT_CLASSES, coarse_classes, 0)

    def merge(c, carry):
        r = pl.ds(pl.multiple_of(c * rows, rows), rows)
        l0, l1, l2 = lgs[0, r, :], lgs[1, r, :], lgs[2, r, :]
        m = jnp.maximum(jnp.maximum(l0, l1), l2)
        w0, w1, w2 = jnp.exp(l0 - m), jnp.exp(l1 - m), jnp.exp(l2 - m)
        num = w0 * og[0, r, :] + w1 * og[1, r, :] + w2 * og[2, r, :]
        o_ref[r, :] = (num / (w0 + w1 + w2)).astype(BF16)
        return carry

    lax.fori_loop(0, seq // rows, merge, 0)


def _attention(proj3, bias_tab):
    bsz, seq, _ = proj3.shape
    n_pat = len(DILATION_PATTERNS)
    pairs = ATT_HEADS // 2
    base = 4 * RET_WIDTH // LANES

    def col(section):
        return pl.BlockSpec((None, seq, LANES),
                            lambda b, hp: (b, 0, base + section * (ATT_WIDTH // LANES) + hp))

    return pl.pallas_call(
        _attn_body,
        grid=(bsz, pairs),
        in_specs=[col(0), col(1), col(2),
                  pl.BlockSpec((n_pat, 2, 3, ATT_QBLK, ATT_KWIN), lambda b, hp: (0, hp, 0, 0, 0))],
        out_specs=pl.BlockSpec((None, seq, LANES), lambda b, hp: (b, 0, hp)),
        out_shape=jax.ShapeDtypeStruct((bsz, seq, ATT_WIDTH), BF16),
        scratch_shapes=[pltpu.VMEM((seq, LANES), F32)] * 3
                       + [pltpu.VMEM((seq, LANES), BF16)]
                       + [pltpu.VMEM((ATT_COARSE, seq // ATT_COARSE, LANES), F32)] * 3
                       + [pltpu.VMEM((ATT_CLASSES * seq // ATT_COARSE, LANES), BF16)] * 3
                       + [pltpu.VMEM((ATT_CLASSES * seq // ATT_COARSE, LANES), F32)] * 4
                       + [pltpu.VMEM((n_pat, seq, LANES), F32)] * 2,
        compiler_params=_cparams(("arbitrary", "arbitrary"), 56),
        name="attention",
    )(proj3, proj3, proj3, bias_tab)


def _mix_body(ret_ref, att_ref, gr_ref, ga_ref, x_ref, mod_ref, gain_ref,
              wr_ref, wa_ref, wo_ref, wrt_ref, x1_ref, hslab_ref, hrow_ref, logit_ref, slab_scr):
    tm = x_ref.shape[0]
    shift_f, scale_f, gate_m = mod_ref[0, 3:4, :], mod_ref[0, 4:5, :], mod_ref[0, 2:3, :]
    for r in range(tm // MIX_ROWS):
        rows = pl.ds(r * MIX_ROWS, MIX_ROWS)
        y_ret = jnp.dot(ret_ref[rows, :], wr_ref[...], preferred_element_type=F32)
        y_att = jnp.dot(att_ref[rows, :], wa_ref[...], preferred_element_type=F32)
        merged = (_sigmoid(gr_ref[rows, :].astype(F32)) * y_ret
                  + _sigmoid(ga_ref[rows, :].astype(F32)) * y_att)
        mixed = jnp.dot(merged.astype(BF16), wo_ref[...], preferred_element_type=F32)
        x1 = x_ref[rows, :] + gate_m * mixed
        x1_ref[rows, :] = x1
        h = _rms(x1, gain_ref[...]) * (1.0 + scale_f) + shift_f
        hrow_ref[rows, :] = h.astype(BF16)
        _rows_to_tiles(h, slab_scr, hslab_ref, first=r * MIX_ROWS)
        h_hi = h.astype(BF16)
        h_lo = (h - h_hi.astype(F32)).astype(BF16)
        logit_ref[rows, :] = (jnp.dot(h_hi, wrt_ref[0], preferred_element_type=F32)
                              + (jnp.dot(h_lo, wrt_ref[0], preferred_element_type=F32)
                                 + jnp.dot(h_hi, wrt_ref[1], preferred_element_type=F32)))


def _mix(ret2, att2, proj2, x2, mod3, gain, wr, wa, wo, w_router, seq, tm=512):
    t, d = x2.shape
    gate_base = (4 * RET_WIDTH + 3 * ATT_WIDTH) // d
    row = lambda i: (i, 0)
    const = lambda i: (0, 0)
    return pl.pallas_call(
        _mix_body,
        grid=(t // tm,),
        in_specs=[pl.BlockSpec((tm, d), row), pl.BlockSpec((tm, d), row),
                  pl.BlockSpec((tm, d), lambda i: (i, gate_base)),
                  pl.BlockSpec((tm, d), lambda i: (i, gate_base + 1)),
                  pl.BlockSpec((tm, d), row),
                  pl.BlockSpec((1, 6, d), lambda i: ((i * tm) // seq, 0, 0)),
                  pl.BlockSpec((1, d), const),
                  pl.BlockSpec((d, d), const), pl.BlockSpec((d, d), const),
                  pl.BlockSpec((d, d), const),
                  pl.BlockSpec((2, d, N_EXPERTS), lambda i: (0, 0, 0))],
        out_specs=[pl.BlockSpec((tm, d), row),
                   pl.BlockSpec((tm, SLABS, LANES), lambda i: (i, 0, 0)),
                   pl.BlockSpec((tm, d), row),
                   pl.BlockSpec((tm, N_EXPERTS), row)],
        out_shape=[jax.ShapeDtypeStruct((t, d), F32),
                   jax.ShapeDtypeStruct((t, SLABS, LANES), BF16),
                   jax.ShapeDtypeStruct((t, d), BF16),
                   jax.ShapeDtypeStruct((t, N_EXPERTS), F32)],
        scratch_shapes=[pltpu.VMEM((tm * SLABS, LANES), F32)],
        compiler_params=_cparams(("arbitrary",), 56),
        name="mix",
    )(ret2, att2, proj2, proj2, x2, mod3, gain, wr, wa, wo, w_router)


def _route_body(logit_ref, bias_ref, eidx_ref, gate_ref, rank_ref, cnt_ref, carry_scr, tri_scr):
    tn = logit_ref.shape[0]
    per_group = N_EXPERTS // N_GROUPS

    @pl.when(pl.program_id(0) == 0)
    def _():
        carry_scr[...] = jnp.zeros_like(carry_scr)
        r = lax.broadcasted_iota(I32, (tn, tn), 0)
        c = lax.broadcasted_iota(I32, (tn, tn), 1)
        tri_scr[...] = jnp.where(r < c, 1.0, 0.0).astype(BF16)

    scores = _sigmoid(logit_ref[...]).T
    choice = scores + jnp.concatenate([bias_ref[...]] * (tn // LANES), axis=1)
    neg_inf = -jnp.inf

    sub = lax.broadcasted_iota(I32, (per_group, tn), 0).astype(F32)
    group_score = []
    for g in range(N_GROUPS):
        cg = choice[g * per_group:(g + 1) * per_group, :]
        m1 = jnp.max(cg, axis=0, keepdims=True)
        first = jnp.min(jnp.where(cg == m1, sub, float(per_group)), axis=0, keepdims=True)
        m2 = jnp.max(jnp.where(sub == first, neg_inf, cg), axis=0, keepdims=True)
        group_score.append(m1 + m2)

    masked = []
    for a in range(N_GROUPS):
        beaten = jnp.zeros((1, tn), F32)
        for b in range(N_GROUPS):
            if b == a:
                continue
            wins = (group_score[b] >= group_score[a]) if b < a else (group_score[b] > group_score[a])
            beaten = beaten + jnp.where(wins, 1.0, 0.0)
        keep = beaten < float(TOPK_GROUPS)
        masked.append(jnp.where(keep, choice[a * per_group:(a + 1) * per_group, :], neg_inf))
    work = jnp.concatenate(masked, axis=0)

    eid = lax.broadcasted_iota(I32, (N_EXPERTS, tn), 0).astype(F32)
    picked = jnp.zeros((N_EXPERTS, tn), F32)
    idx_rows, gate_rows = [], []
    for _ in range(TOP_K):
        m = jnp.max(work, axis=0, keepdims=True)
        idx = jnp.min(jnp.where(work == m, eid, float(N_EXPERTS)), axis=0, keepdims=True)
        sel = eid == idx
        gate_rows.append(jnp.sum(jnp.where(sel, scores, 0.0), axis=0, keepdims=True))
        picked = picked + jnp.where(sel, 1.0, 0.0)
        work = jnp.where(sel, neg_inf, work)
        idx_rows.append(idx)

    before = (jnp.dot(picked.astype(BF16), tri_scr[...], preferred_element_type=F32)
              + carry_scr[:, 0:1])
    rank_rows = [jnp.sum(jnp.where(eid == idx, before, 0.0), axis=0, keepdims=True)
                 for idx in idx_rows]
    carry = carry_scr[...] + jnp.sum(picked, axis=1, keepdims=True)
    carry_scr[...] = carry
    cnt_ref[...] = carry

    gates = jnp.concatenate(gate_rows, axis=0)
    gates = gates / jnp.sum(gates, axis=0, keepdims=True) * ROUTED_SCALE
    eidx_ref[...] = jnp.concatenate(idx_rows, axis=0).astype(I32)
    gate_ref[...] = gates
    rank_ref[...] = jnp.concatenate(rank_rows, axis=0).astype(I32)


def _route(logits, bias_b, tn=512):
    t = logits.shape[0]
    tok = lambda i: (0, i)
    return pl.pallas_call(
        _route_body,
        grid=(t // tn,),
        in_specs=[pl.BlockSpec((tn, N_EXPERTS), lambda i: (i, 0)),
                  pl.BlockSpec((N_EXPERTS, LANES), lambda i: (0, 0))],
        out_specs=[pl.BlockSpec((TOP_K, tn), tok), pl.BlockSpec((TOP_K, tn), tok),
                   pl.BlockSpec((TOP_K, tn), tok),
                   pl.BlockSpec((N_EXPERTS, LANES), lambda i: (0, 0))],
        out_shape=[jax.ShapeDtypeStruct((TOP_K, t), I32), jax.ShapeDtypeStruct((TOP_K, t), F32),
                   jax.ShapeDtypeStruct((TOP_K, t), I32),
                   jax.ShapeDtypeStruct((N_EXPERTS, LANES), F32)],
        scratch_shapes=[pltpu.VMEM((N_EXPERTS, LANES), F32), pltpu.VMEM((tn, tn), BF16)],
        compiler_params=_cparams(("arbitrary",), 48),
        name="route",
    )(logits, bias_b)


def _dest_body(start_ref, eidx_ref, rank_ref, dest_ref):
    e = eidx_ref[...]

    def body(j, acc):
        return jnp.where(e == j, start_ref[j], acc)

    dest_ref[...] = rank_ref[...] + lax.fori_loop(0, N_EXPERTS, body, jnp.zeros_like(e))


def _dest(pad_start, eidx, rank, tn=2048):
    t = eidx.shape[1]
    tok = lambda i, s: (0, i)
    return pl.pallas_call(
        _dest_body,
        grid_spec=pltpu.PrefetchScalarGridSpec(
            num_scalar_prefetch=1,
            grid=(t // tn,),
            in_specs=[pl.BlockSpec((TOP_K, tn), tok), pl.BlockSpec((TOP_K, tn), tok)],
            out_specs=pl.BlockSpec((TOP_K, tn), tok)),
        out_shape=jax.ShapeDtypeStruct((TOP_K, t), I32),
        compiler_params=_cparams(("arbitrary",), 32),
        name="dest",
    )(pad_start, eidx, rank)


def _dispatch_body(dest_hbm, h_ref, xs_hbm, dest_smem, idx_sem, row_sem):
    tt = dest_smem.shape[0] // TOP_K
    idx_copy = pltpu.make_async_copy(dest_hbm.at[pl.program_id(0)], dest_smem, idx_sem)
    idx_copy.start()
    idx_copy.wait()

    def issue(t, carry):
        for k in range(TOP_K):
            pltpu.make_async_copy(h_ref.at[t], xs_hbm.at[dest_smem[t * TOP_K + k]],
                                  row_sem).start(priority=k % 2)
        return carry

    lax.fori_loop(0, tt, issue, 0)

    for k in range(TOP_K):
        pltpu.make_async_copy(h_ref, xs_hbm.at[pl.ds(0, tt)], row_sem).wait()


def _dispatch(dest_tiles, htiles, n_rows):
    n_tiles, width = dest_tiles.shape
    tt = width // TOP_K
    return pl.pallas_call(
        _dispatch_body,
        grid=(n_tiles,),
        in_specs=[pl.BlockSpec(memory_space=pl.ANY),
                  pl.BlockSpec((tt, SLABS, LANES), lambda i: (i, 0, 0))],
        out_specs=pl.BlockSpec(memory_space=pl.ANY),
        out_shape=jax.ShapeDtypeStruct((n_rows, SLABS, LANES), BF16),
        scratch_shapes=[pltpu.SMEM((width,), I32), pltpu.SemaphoreType.DMA, pltpu.SemaphoreType.DMA],
        compiler_params=_cparams(("arbitrary",), 32),
        name="dispatch",
    )(dest_tiles, htiles)


def _experts_body(bexp_ref, bnew_ref, nused_ref, bnext_ref, bslot_ref, xs_hbm, wg_hbm, wu_hbm, wd_hbm,
                  ys_ref, wgu_s, wd_s, stage_scr, xring, wg_buf, wu_buf, wd_buf, ring_sem, w_sem):
    i = pl.program_id(0)
    hid = wd_s.shape[0]
    n_used = nused_ref[0]

    def block_copy(j):
        slot = j % EXPERT_RING
        src = xs_hbm.at[pl.ds(pl.multiple_of(j * MOE_BLOCK, MOE_BLOCK), MOE_BLOCK)]
        return pltpu.make_async_copy(src, xring.at[slot], ring_sem.at[slot])

    def weight_copies(expert, slot):
        return [pltpu.make_async_copy(src.at[expert], dst.at[slot], w_sem.at[slot])
                for src, dst in ((wg_hbm, wg_buf), (wu_hbm, wu_buf), (wd_hbm, wd_buf))]

    @pl.when(i == 0)
    def _():
        for copy in weight_copies(bexp_ref[0], 0):
            copy.start()
        for j in range(EXPERT_RING - 1):
            @pl.when(j < n_used)
            def _(j=j):
                block_copy(j).start()

    @pl.when(i + EXPERT_RING - 1 < n_used)
    def _():
        block_copy(i + EXPERT_RING - 1).start()

    @pl.when(i < n_used)
    def _():
        @pl.when(bnew_ref[i] == 1)
        def _():
            slot = bslot_ref[i]
            for copy in weight_copies(bexp_ref[i], slot):
                copy.wait()
            nxt = bnext_ref[i]

            @pl.when(nxt >= 0)
            def _():
                for copy in weight_copies(nxt, 1 - slot):
                    copy.start()

            wgu_s[:, :hid] = wg_buf[slot].astype(BF16)
            wgu_s[:, hid:] = wu_buf[slot].astype(BF16)
            wd_s[...] = wd_buf[slot].astype(BF16)

        block_copy(i).wait()
        x = _tiles_to_rows(xring[i % EXPERT_RING], stage_scr).astype(BF16)
        gu = jnp.dot(x, wgu_s[...], preferred_element_type=F32)
        hg, hu = gu[:, :hid], gu[:, hid:]
        y = jnp.dot((_silu(hg) * hu).astype(BF16), wd_s[...], preferred_element_type=F32)
        _rows_to_tiles(y, stage_scr, ys_ref)


def _experts(block_expert, block_new, n_used, block_next, block_slot, xs, w_gate, w_up, w_down):
    n_blocks = block_expert.shape[0]
    d, hid = w_gate.shape[1], w_gate.shape[2]
    tile_block = (MOE_BLOCK, SLABS, LANES)
    hbm = pl.BlockSpec(memory_space=pl.ANY)

    def blk(i, be, bn, nu, bx, bs):
        return (jnp.minimum(i, nu[0] - 1), 0, 0)

    return pl.pallas_call(
        _experts_body,
        grid_spec=pltpu.PrefetchScalarGridSpec(
            num_scalar_prefetch=5,
            grid=(n_blocks,),
            in_specs=[hbm, hbm, hbm, hbm],
            out_specs=pl.BlockSpec(tile_block, blk),
            scratch_shapes=[pltpu.VMEM((d, 2 * hid), BF16), pltpu.VMEM((hid, d), BF16),
                            pltpu.VMEM((MOE_BLOCK * SLABS, LANES), F32),
                            pltpu.VMEM((EXPERT_RING,) + tile_block, BF16),
                            pltpu.VMEM((2, d, hid), F32), pltpu.VMEM((2, d, hid), F32),
                            pltpu.VMEM((2, hid, d), F32),
                            pltpu.SemaphoreType.DMA((EXPERT_RING,)),
                            pltpu.SemaphoreType.DMA((2,))]),
        out_shape=jax.ShapeDtypeStruct(xs.shape, BF16),
        compiler_params=_cparams(("arbitrary",), 32),
        name="experts",
    )(block_expert, block_new, n_used, block_next, block_slot, xs, w_gate, w_up, w_down)


def _combine_body(dest_hbm, gate_hbm, ys_hbm, x1_ref, h_ref, wsg_ref, wsu_ref, wsd_ref,
                  mod_ref, gain_ref, o_ref, dest_smem0, dest_smem1, gate_smem, buf, shared_scr,
                  routed_scr, idx_sem, gate_sem, row_sem, *, final_norm):
    tc = x1_ref.shape[0]
    i = pl.program_id(0)
    has_next = i + 1 < pl.num_programs(0)
    cur = i % 2
    dest_smem = (dest_smem0, dest_smem1)

    def index_copy(tile, slot):
        return pltpu.make_async_copy(dest_hbm.at[tile], dest_smem[slot], idx_sem)

    def start_rows(slot):
        def issue(t, carry):
            for k in range(TOP_K):
                pltpu.make_async_copy(ys_hbm.at[dest_smem[slot][t * TOP_K + k]],
                                      buf.at[slot, t * TOP_K + k],
                                      row_sem.at[slot]).start(priority=k % 2)
            return carry

        lax.fori_loop(0, tc, issue, 0)

    gate_copy = pltpu.make_async_copy(gate_hbm.at[i], gate_smem, gate_sem)
    gate_copy.start()

    @pl.when(i == 0)
    def _():
        index_copy(0, 0).start()
        index_copy(0, 0).wait()
        start_rows(0)

    for slot in range(2):
        @pl.when(jnp.logical_and(has_next, cur != slot))
        def _(slot=slot):
            index_copy(i + 1, slot).start()

    h = h_ref[...]
    hid = (_silu(jnp.dot(h, wsg_ref[...], preferred_element_type=F32))
           * jnp.dot(h, wsu_ref[...], preferred_element_type=F32))
    shared_scr[...] = jnp.dot(hid.astype(BF16), wsd_ref[...], preferred_element_type=F32)

    for slot in range(2):
        @pl.when(jnp.logical_and(has_next, cur != slot))
        def _(slot=slot):
            index_copy(i + 1, slot).wait()
            start_rows(slot)

    pltpu.make_async_copy(ys_hbm.at[pl.ds(0, tc * TOP_K)], buf.at[cur], row_sem.at[cur]).wait()
    gate_copy.wait()

    def weigh(g, carry):
        for u in range(COMBINE_UNROLL):
            t = g * COMBINE_UNROLL + u
            acc = gate_smem[t * TOP_K] * buf[cur, t * TOP_K].astype(F32)
            for k in range(1, TOP_K):
                acc = acc + gate_smem[t * TOP_K + k] * buf[cur, t * TOP_K + k].astype(F32)
            routed_scr[pl.ds(pl.multiple_of(t * SLABS, SLABS), SLABS), :] = acc
        return carry

    lax.fori_loop(0, tc // COMBINE_UNROLL, weigh, 0)

    gate_f = mod_ref[0, 5:6, :]
    gain = gain_ref[...]

    def finish(c, carry):
        r0 = pl.multiple_of(c * COMBINE_ROWS, COMBINE_ROWS)
        rows = pl.ds(r0, COMBINE_ROWS)
        routed = jnp.concatenate(
            [routed_scr[pl.ds(r0 * SLABS + s, COMBINE_ROWS, stride=SLABS), :]
             for s in range(SLABS)], axis=1)
        x2 = x1_ref[rows, :] + gate_f * (shared_scr[rows, :] + routed)
        o_ref[rows, :] = _rms(x2, gain) if final_norm else x2
        return carry

    lax.fori_loop(0, tc // COMBINE_ROWS, finish, 0)


def _combine(dest_tiles, gate_tiles, ys, x1, hrow, wsg, wsu, wsd, mod3, gain, seq, tc, final_norm):
    t, d = x1.shape
    hid = wsg.shape[1]
    row = lambda i: (i, 0)
    const = lambda i: (0, 0)
    return pl.pallas_call(
        functools.partial(_combine_body, final_norm=final_norm),
        grid=(t // tc,),
        in_specs=[pl.BlockSpec(memory_space=pl.ANY),
                  pl.BlockSpec(memory_space=pl.ANY),
                  pl.BlockSpec(memory_space=pl.ANY),
                  pl.BlockSpec((tc, d), row), pl.BlockSpec((tc, d), row),
                  pl.BlockSpec((d, hid), const), pl.BlockSpec((d, hid), const),
                  pl.BlockSpec((hid, d), const),
                  pl.BlockSpec((1, 6, d), lambda i: ((i * tc) // seq, 0, 0)),
                  pl.BlockSpec((1, d), const)],
        out_specs=pl.BlockSpec((tc, d), row),
        out_shape=jax.ShapeDtypeStruct((t, d), F32),
        scratch_shapes=[pltpu.SMEM((TOP_K * tc,), I32), pltpu.SMEM((TOP_K * tc,), I32),
                        pltpu.SMEM((TOP_K * tc,), F32),
                        pltpu.VMEM((2, tc * TOP_K, SLABS, LANES), BF16),
                        pltpu.VMEM((tc, d), F32),
                        pltpu.VMEM((tc * SLABS, LANES), F32),
                        pltpu.SemaphoreType.DMA, pltpu.SemaphoreType.DMA,
                        pltpu.SemaphoreType.DMA((2,))],
        compiler_params=_cparams(("arbitrary",), 56),
        name="combine",
    )(dest_tiles, gate_tiles, ys, x1, hrow, wsg, wsu, wsd, mod3, gain)


def _split_bf16(w):
    hi = w.astype(BF16)
    lo = (w.astype(F32) - hi.astype(F32)).astype(BF16)
    return jnp.stack([hi, lo], axis=0)


def _tile_major(a, tile):
    k, t = a.shape
    return a.reshape(k, t // tile, tile).transpose(1, 2, 0).reshape(t // tile, tile * k)


def kernel(x, c, w_ada, b_ada, norm_mix, w_in, ret_decay, t5_bias, w_ret_up, w_att_up, w_o,
           norm_ffn, w_router, router_bias, w_gate, w_up, w_down, ws_gate, ws_up, ws_down, norm_final):
    bsz, seq, d = x.shape
    depth = w_ada.shape[0]
    t = bsz * seq
    assert d == D_MODEL and seq % (ATT_KWIN * DILATION_PATTERNS[-1][1]) == 0

    half = RET_HEAD_DIM // 2
    inv_freq = ROPE_BASE ** (-jnp.arange(half, dtype=F32) / half)
    ang = jnp.arange(seq, dtype=F32)[:, None] * inv_freq[None, :]
    cos, sin = jnp.cos(ang), jnp.sin(ang)
    bias_tab = _attention_bias(t5_bias)

    n_assign = t * TOP_K
    n_blocks = -(-n_assign // MOE_BLOCK) + N_EXPERTS
    n_rows = n_blocks * MOE_BLOCK
    disp_tile = min(1024, t)
    comb_tile = min(256, t)

    x2 = x.reshape(t, d)
    for layer in range(depth):
        mod3 = _adaln(c, w_ada[layer], b_ada[layer]).reshape(bsz, 6, d)
        proj = _inproj(x2, norm_mix[layer].reshape(1, d), mod3, w_in[layer].astype(BF16), seq)
        proj3 = proj.reshape(bsz, seq, PROJ_WIDTH)
        log_gamma = jnp.log1p(-jnp.exp(ret_decay[layer].astype(F32)))
        ret = _retention(proj3, log_gamma, cos, sin)
        att = _attention(proj3, bias_tab)
        x1, hslab, hrow, logits = _mix(
            ret.reshape(t, RET_WIDTH), att.reshape(t, ATT_WIDTH), proj, x2, mod3,
            norm_ffn[layer].reshape(1, d), w_ret_up[layer].astype(BF16),
            w_att_up[layer].astype(BF16), w_o[layer].astype(BF16), _split_bf16(w_router[layer]),
            seq)

        bias_b = jnp.broadcast_to(router_bias[layer].astype(F32)[:, None], (N_EXPERTS, LANES))
        eidx, gate, rank, counts = _route(logits, bias_b)

        counts = counts[:, 0].astype(I32)
        padded = (counts + MOE_BLOCK - 1) // MOE_BLOCK * MOE_BLOCK
        pad_end = jnp.cumsum(padded)
        pad_start = pad_end - padded
        block_row = jnp.arange(n_blocks, dtype=I32) * MOE_BLOCK
        block_expert = jnp.minimum(
            jnp.sum((pad_end[None, :] <= block_row[:, None]).astype(I32), axis=1), N_EXPERTS - 1)
        block_new = jnp.concatenate(
            [jnp.ones((1,), I32), (block_expert[1:] != block_expert[:-1]).astype(I32)])
        n_used = (pad_end[-1:] // MOE_BLOCK).astype(I32)

        dest = _dest(pad_start.astype(I32), eidx, rank)
        xs = _dispatch(_tile_major(dest, disp_tile), hslab, n_rows)
        expert_ids = jnp.arange(N_EXPERTS, dtype=I32)
        later = jnp.logical_and(expert_ids[None, :] > block_expert[:, None], (padded > 0)[None, :])
        block_next = jnp.min(jnp.where(later, expert_ids[None, :], N_EXPERTS), axis=1)
        block_next = jnp.where(block_next == N_EXPERTS, -1, block_next).astype(I32)
        block_slot = ((jnp.cumsum(block_new) - 1) % 2).astype(I32)
        ys = _experts(block_expert, block_new, n_used, block_next, block_slot, xs,
                      w_gate[layer], w_up[layer], w_down[layer])
        x2 = _combine(_tile_major(dest, comb_tile), _tile_major(gate, comb_tile), ys,
                      x1, hrow, ws_gate[layer].astype(BF16), ws_up[layer].astype(BF16),
                      ws_down[layer].astype(BF16), mod3, norm_final.reshape(1, d), seq, comb_tile,
                      final_norm=(layer == depth - 1))
    return x2.reshape(bsz, seq, d)
```

```python
import functools
import math

import jax
import jax.numpy as jnp
import numpy as np
from jax import lax
from jax.experimental import pallas as pl
from jax.experimental.pallas import tpu as pltpu

F32 = jnp.float32
BF16 = jnp.bfloat16
I32 = jnp.int32

D_MODEL = 1024
RET_HEADS = 4
RET_HEAD_DIM = 256
RET_WIDTH = RET_HEADS * RET_HEAD_DIM
RET_CHUNK = 128
ROPE_BASE = 10000.0
ATT_HEADS = 16
ATT_HEAD_DIM = 64
ATT_WIDTH = ATT_HEADS * ATT_HEAD_DIM
DILATION_PATTERNS = ((128, 1), (512, 4), (2048, 16))
REL_BUCKETS = 32
REL_MAX_DISTANCE = 1024
N_EXPERTS = 256
TOP_K = 8
N_GROUPS = 8
TOPK_GROUPS = 4
EXPERT_HIDDEN = 256
ROUTED_SCALE = 2.5
MOE_BLOCK = 256
NORM_EPS = 1e-6
PROJ_WIDTH = 4 * RET_WIDTH + 3 * ATT_WIDTH + 2 * D_MODEL

LANES = 128
SUBLANES = 8
SLABS = D_MODEL // LANES
ATT_RADIUS = 64
ATT_QBLK = 128
ATT_KWIN = 256
MIX_ROWS = 256
RET_UNROLL = 16
EXPERT_RING = 3
COMBINE_ROWS = 32
COMBINE_UNROLL = 8
ATT_COARSE = 4
ATT_CLASSES = 4
ATT_UNROLL = 32
NEG_BIG = -1e30
MIB = 1024 * 1024


def _cparams(sem, vmem_mib):
    return pltpu.CompilerParams(dimension_semantics=sem, vmem_limit_bytes=vmem_mib * MIB)


def _sigmoid(x):
    return 1.0 / (1.0 + jnp.exp(-x))


def _silu(x):
    return x * _sigmoid(x)


def _rms(x, gain):
    return x * lax.rsqrt(jnp.mean(x * x, axis=-1, keepdims=True) + NORM_EPS) * gain


def _rows_to_tiles(rows, stage_scr, tiles_ref, first=0):
    n = rows.shape[0]
    base = first * SLABS
    for s in range(SLABS):
        stage_scr[pl.ds(base + s, n, stride=SLABS), :] = rows[:, s * LANES:(s + 1) * LANES]
    staged = stage_scr[pl.ds(base, n * SLABS), :]
    tiles_ref[pl.ds(first, n)] = staged.reshape(n, SLABS, LANES).astype(BF16)


def _tiles_to_rows(tiles, stage_scr):
    n = tiles.shape[0]
    stage_scr[...] = tiles.astype(F32).reshape(n * SLABS, LANES)
    return jnp.concatenate([stage_scr[pl.ds(s, n, stride=SLABS), :] for s in range(SLABS)], axis=1)


def _adaln_body(c_ref, w_ref, b_ref, o_ref):
    cond = _silu(c_ref[...])
    o_ref[...] = jnp.dot(cond, w_ref[...], preferred_element_type=F32,
                         precision=lax.Precision.HIGHEST) + b_ref[...]


def _adaln(c, w, b):
    bsz, d = c.shape
    n = w.shape[1]
    return pl.pallas_call(
        _adaln_body,
        grid=(n // d,),
        in_specs=[pl.BlockSpec((bsz, d), lambda j: (0, 0)),
                  pl.BlockSpec((d, d), lambda j: (0, j)),
                  pl.BlockSpec((1, d), lambda j: (0, j))],
        out_specs=pl.BlockSpec((bsz, d), lambda j: (0, j)),
        out_shape=jax.ShapeDtypeStruct((bsz, n), F32),
        compiler_params=_cparams(("arbitrary",), 32),
        name="adaln",
    )(c, w, b.reshape(1, n))


def _inproj_body(x_ref, gain_ref, mod_ref, w_ref, o_ref, h_scr):
    @pl.when(pl.program_id(1) == 0)
    def _():
        y = _rms(x_ref[...], gain_ref[...])
        h = y * (1.0 + mod_ref[0, 1:2, :]) + mod_ref[0, 0:1, :]
        h_scr[...] = h.astype(BF16)

    o_ref[...] = jnp.dot(h_scr[...], w_ref[...], preferred_element_type=F32).astype(BF16)


def _inproj(x2, gain, mod3, w_bf, seq, tm=1024, tn=4608):
    t, d = x2.shape
    n = w_bf.shape[1]
    return pl.pallas_call(
        _inproj_body,
        grid=(t // tm, n // tn),
        in_specs=[pl.BlockSpec((tm, d), lambda i, j: (i, 0)),
                  pl.BlockSpec((1, d), lambda i, j: (0, 0)),
                  pl.BlockSpec((1, 6, d), lambda i, j: ((i * tm) // seq, 0, 0)),
                  pl.BlockSpec((d, tn), lambda i, j: (0, j))],
        out_specs=pl.BlockSpec((tm, tn), lambda i, j: (i, j)),
        out_shape=jax.ShapeDtypeStruct((t, n), BF16),
        scratch_shapes=[pltpu.VMEM((tm, d), BF16)],
        compiler_params=_cparams(("arbitrary", "arbitrary"), 56),
        name="inproj",
    )(x2, gain, mod3, w_bf)


def _ret_body(lg_ref, q_ref, k_ref, v_ref, g_ref, cos_ref, sin_ref, o_ref,
              qr_scr, kr_scr, accf_scr, accb_scr, stf_scr, stb_scr):
    head = pl.program_id(1)
    seq = q_ref.shape[0]
    n_chunks = seq // RET_CHUNK
    half = RET_HEAD_DIM // 2
    rot_rows = 256

    def rot_step(c, carry):
        r = pl.ds(pl.multiple_of(c * rot_rows, rot_rows), rot_rows)
        cs = cos_ref[r, :]
        sn = sin_ref[r, :]
        for src, dst, scale in ((q_ref, qr_scr, 1.0), (k_ref, kr_scr, RET_HEAD_DIM ** -0.5)):
            t = src[r, :].astype(F32)
            t1, t2 = t[:, :half], t[:, half:]
            dst[r, :half] = ((t1 * cs - t2 * sn) * scale).astype(BF16)
            dst[r, half:] = ((t1 * sn + t2 * cs) * scale).astype(BF16)
        return carry

    lax.fori_loop(0, seq // rot_rows, rot_step, 0)

    ri = lax.broadcasted_iota(I32, (RET_CHUNK, RET_CHUNK), 0)
    ci = lax.broadcasted_iota(I32, (RET_CHUNK, RET_CHUNK), 1)
    rowpos = lax.broadcasted_iota(I32, (RET_CHUNK, RET_HEAD_DIM), 0).astype(F32)

    def decay_tables(lg, forward):
        if forward:
            diff = (ri - ci).astype(F32)
            allowed = ri >= ci
            q_scale = jnp.exp(lg * (rowpos + 1.0))
            k_scale = jnp.exp(lg * (RET_CHUNK - 1.0 - rowpos))
        else:
            diff = (ci - ri).astype(F32)
            allowed = ci > ri
            q_scale = jnp.exp(lg * (RET_CHUNK - rowpos))
            k_scale = jnp.exp(lg * rowpos)
        intra = jnp.where(allowed, jnp.exp(lg * jnp.where(allowed, diff, 0.0)), 0.0)
        chunk_decay = jnp.exp(lg * jnp.full((1, RET_HEAD_DIM), float(RET_CHUNK), F32))
        return intra, q_scale, k_scale, chunk_decay

    def chunk_update(c, tables, st_scr, acc_scr):
        intra, q_scale, k_scale, chunk_decay = tables
        r = pl.ds(pl.multiple_of(c * RET_CHUNK, RET_CHUNK), RET_CHUNK)
        q = qr_scr[r, :]
        k = kr_scr[r, :]
        v = v_ref[r, :]
        scores = lax.dot_general(q, k, (((1,), (1,)), ((), ())),
                                 preferred_element_type=F32) * intra
        state = st_scr[...]
        acc_scr[r, :] = (jnp.dot(scores.astype(BF16), v, preferred_element_type=F32)
                         + jnp.dot((q.astype(F32) * q_scale).astype(BF16), state.astype(BF16),
                                   preferred_element_type=F32))
        k_t = (k.astype(F32) * k_scale).T.astype(BF16)
        st_scr[...] = state * chunk_decay + jnp.dot(k_t, v, preferred_element_type=F32)

    fwd_tables = decay_tables(lg_ref[0, head], True)
    bwd_tables = decay_tables(lg_ref[1, head], False)
    stf_scr[...] = jnp.zeros_like(stf_scr)
    stb_scr[...] = jnp.zeros_like(stb_scr)

    def step(trip, carry):
        for u in range(RET_UNROLL):
            i = trip * RET_UNROLL + u
            chunk_update(i, fwd_tables, stf_scr, accf_scr)
            chunk_update(n_chunks - 1 - i, bwd_tables, stb_scr, accb_scr)
        return carry

    lax.fori_loop(0, n_chunks // RET_UNROLL, step, 0)

    def finish(c, carry):
        r = pl.ds(pl.multiple_of(c * rot_rows, rot_rows), rot_rows)
        o = accf_scr[r, :] + accb_scr[r, :]
        mu = jnp.mean(o, axis=-1, keepdims=True)
        oc = o - mu
        var = jnp.mean(oc * oc, axis=-1, keepdims=True)
        o = oc * lax.rsqrt(var + NORM_EPS)
        o_ref[r, :] = (o * _silu(g_ref[r, :].astype(F32))).astype(BF16)
        return carry

    lax.fori_loop(0, seq // rot_rows, finish, 0)


def _retention(proj3, log_gamma, cos, sin):
    bsz, seq, _ = proj3.shape
    hd = RET_HEAD_DIM

    def col(section):
        return pl.BlockSpec((None, seq, hd), lambda b, h, lg: (b, 0, section * RET_HEADS + h))

    return pl.pallas_call(
        _ret_body,
        grid_spec=pltpu.PrefetchScalarGridSpec(
            num_scalar_prefetch=1,
            grid=(bsz, RET_HEADS),
            in_specs=[col(0), col(1), col(2), col(3),
                      pl.BlockSpec((seq, hd // 2), lambda b, h, lg: (0, 0)),
                      pl.BlockSpec((seq, hd // 2), lambda b, h, lg: (0, 0))],
            out_specs=pl.BlockSpec((None, seq, hd), lambda b, h, lg: (b, 0, h)),
            scratch_shapes=[pltpu.VMEM((seq, hd), BF16), pltpu.VMEM((seq, hd), BF16),
                            pltpu.VMEM((seq, hd), F32), pltpu.VMEM((seq, hd), F32),
                            pltpu.VMEM((hd, hd), F32), pltpu.VMEM((hd, hd), F32)]),
        out_shape=jax.ShapeDtypeStruct((bsz, seq, RET_WIDTH), BF16),
        compiler_params=_cparams(("arbitrary", "arbitrary"), 56),
        name="retention",
    )(log_gamma, proj3, proj3, proj3, proj3, cos, sin)


def _t5_bucket(rel):
    half = REL_BUCKETS // 2
    max_exact = half // 2
    n = jnp.abs(rel)
    large = max_exact + (jnp.log(jnp.maximum(n, 1).astype(F32) / max_exact)
                         / math.log(REL_MAX_DISTANCE / max_exact) * (half - max_exact)).astype(I32)
    large = jnp.minimum(large, half - 1)
    return jnp.where(rel > 0, half, 0) + jnp.where(n < max_exact, n, large)


def _band_buckets():
    qi = jnp.arange(ATT_QBLK, dtype=I32)[:, None]
    kj = jnp.arange(ATT_KWIN, dtype=I32)[None, :]
    tables = []
    for _, dilation in DILATION_PATTERNS:
        cases = []
        for offset in (0, -ATT_RADIUS, ATT_QBLK - ATT_KWIN):
            rel = kj + offset - qi
            cases.append(jnp.where(jnp.abs(rel) <= ATT_RADIUS, _t5_bucket(rel * dilation), -1))
        tables.append(jnp.stack(cases, axis=0))
    return jnp.stack(tables, axis=0)


def _bias_body(t5_ref, bucket_ref, o_ref):
    bucket = bucket_ref[...]

    def head(h, carry):
        acc = jnp.full(bucket.shape, NEG_BIG, F32)
        for b in range(REL_BUCKETS):
            acc = jnp.where(bucket == b, t5_ref[b, h], acc)
        o_ref[h] = acc
        return carry

    lax.fori_loop(0, ATT_HEADS, head, 0)


def _attention_bias(t5_bias):
    buckets = _band_buckets()
    n_pat, n_case = buckets.shape[:2]
    return pl.pallas_call(
        _bias_body,
        grid_spec=pltpu.PrefetchScalarGridSpec(
            num_scalar_prefetch=1,
            grid=(n_pat, n_case),
            in_specs=[pl.BlockSpec((None, None, ATT_QBLK, ATT_KWIN), lambda p, c, t5: (p, c, 0, 0))],
            out_specs=pl.BlockSpec((None, ATT_HEADS, None, ATT_QBLK, ATT_KWIN),
                                   lambda p, c, t5: (p, 0, c, 0, 0))),
        out_shape=jax.ShapeDtypeStruct((n_pat, ATT_HEADS, n_case, ATT_QBLK, ATT_KWIN), F32),
        compiler_params=_cparams(("arbitrary", "arbitrary"), 32),
        name="attn_bias",
    )(t5_bias.astype(F32), buckets)


def _attn_body(q_ref, k_ref, v_ref, bias_ref, o_ref,
               qf, kf, vf, qb, q4, k4, v4, qd, kd, vd, od, ld, o4, l4, og, lgs):
    seq = q_ref.shape[0]
    rows = 256
    lane = lax.broadcasted_iota(I32, (ATT_QBLK, LANES), 1)
    head0 = lane < ATT_HEAD_DIM

    def to_f32(c, carry):
        r = pl.ds(pl.multiple_of(c * rows, rows), rows)
        q = q_ref[r, :].astype(F32) * (ATT_HEAD_DIM ** -0.5)
        qf[r, :] = q
        qb[r, :] = q.astype(BF16)
        kf[r, :] = k_ref[r, :].astype(F32)
        vf[r, :] = v_ref[r, :].astype(F32)
        return carry

    lax.fori_loop(0, seq // rows, to_f32, 0)

    def band_blocks(g, length, n_seg, q_src, k_src, v_src, o_dst, l_dst):
        n_qb = length // ATT_QBLK
        n_blocks = n_seg * n_qb
        unroll = min(ATT_UNROLL, n_blocks)

        def qgroup(it, carry):
            for u in range(unroll):
                qblock(it * unroll + u)
            return carry

        def qblock(b):
            qi = b % n_qb
            base = (b // n_qb) * length
            qs = pl.multiple_of(base + qi * ATT_QBLK, ATT_QBLK)
            ws = pl.multiple_of(
                base + jnp.clip(qi * ATT_QBLK - ATT_RADIUS, 0, length - ATT_KWIN), ATT_RADIUS)
            case = jnp.where(qi == 0, 0, jnp.where(qi == n_qb - 1, 2, 1))
            q = q_src[pl.ds(qs, ATT_QBLK), :]
            k = k_src[pl.ds(ws, ATT_KWIN), :]
            v = v_src[pl.ds(ws, ATT_KWIN), :]
            v_ones = jnp.concatenate([v, jnp.ones_like(v)], axis=1)
            outs, lses = [], []
            for hh in range(2):
                mask = head0 if hh == 0 else jnp.logical_not(head0)
                qm = jnp.where(mask, q, jnp.zeros_like(q))
                s = lax.dot_general(qm, k, (((1,), (1,)), ((), ())), preferred_element_type=F32)
                s = s + bias_ref[g, hh, case]
                m = jnp.max(s, axis=-1, keepdims=True)
                p = jnp.exp(s - m).astype(BF16)
                ol = jnp.dot(p, v_ones, preferred_element_type=F32)
                l = ol[:, LANES:]
                outs.append(ol[:, :LANES] / l)
                lses.append(m + jnp.log(l))
            o_dst[pl.ds(qs, ATT_QBLK), :] = jnp.where(head0, outs[0], outs[1])
            l_dst[pl.ds(qs, ATT_QBLK), :] = jnp.where(head0, lses[0], lses[1])

        lax.fori_loop(0, n_blocks // unroll, qgroup, 0)

    coarse_len = seq // ATT_COARSE

    def split(c, carry):
        strided = pl.ds(c, coarse_len, stride=ATT_COARSE)
        q4[c] = qf[strided, :]
        k4[c] = kf[strided, :]
        v4[c] = vf[strided, :]
        return carry

    lax.fori_loop(0, ATT_COARSE, split, 0)

    for g, (_, dilation) in enumerate(DILATION_PATTERNS):
        length = seq // dilation
        if dilation == 1:
            band_blocks(g, length, 1, qb, k_ref, v_ref, og.at[g], lgs.at[g])
            continue
        assert dilation % ATT_COARSE == 0
        fine = dilation // ATT_COARSE
        assert fine * length == coarse_len and fine * (length // ATT_QBLK) >= min(ATT_UNROLL, fine)

        def coarse_classes(trip, carry, g=g, length=length, fine=fine):
            classes = [(trip * ATT_CLASSES + u, u * coarse_len) for u in range(ATT_CLASSES)]
            segs = [(j, pl.ds(j, length, stride=fine) if fine > 1 else pl.ds(0, length), j * length)
                    for j in range(fine)]
            for c, base in classes:
                for _, src, off in segs:
                    dense = pl.ds(base + off, length)
                    qd[dense, :] = q4[c, src, :].astype(BF16)
                    kd[dense, :] = k4[c, src, :].astype(BF16)
                    vd[dense, :] = v4[c, src, :].astype(BF16)
            band_blocks(g, length, ATT_CLASSES * fine, qd, kd, vd, od, ld)
            for c, base in classes:
                whole = pl.ds(c, coarse_len, stride=ATT_COARSE)
                staged = pl.ds(base, coarse_len)
                if fine == 1:
                    og[g, whole, :] = od[staged, :]
                    lgs[g, whole, :] = ld[staged, :]
                else:
                    for j, _, off in segs:
                        interleaved = pl.ds(base + j, length, stride=fine)
                        o4[interleaved, :] = od[pl.ds(base + off, length), :]
                        l4[interleaved, :] = ld[pl.ds(base + off, length), :]
                    og[g, whole, :] = o4[staged, :]
                    lgs[g, whole, :] = l4[staged, :]
            return carry

        lax.fori_loop(0, ATT_COARSE // ATT_CLASSES, coarse_classes, 0)

    def merge(c, carry):
        r = pl.ds(pl.multiple_of(c * rows, rows), rows)
        l0, l1, l2 = lgs[0, r, :], lgs[1, r, :], lgs[2, r, :]
        m = jnp.maximum(jnp.maximum(l0, l1), l2)
        w0, w1, w2 = jnp.exp(l0 - m), jnp.exp(l1 - m), jnp.exp(l2 - m)
        num = w0 * og[0, r, :] + w1 * og[1, r, :] + w2 * og[2, r, :]
        o_ref[r, :] = (num / (w0 + w1 + w2)).astype(BF16)
        return carry

    lax.fori_loop(0, seq // rows, merge, 0)


def _attention(proj3, bias_tab):
    bsz, seq, _ = proj3.shape
    n_pat = len(DILATION_PATTERNS)
    pairs = ATT_HEADS // 2
    base = 4 * RET_WIDTH // LANES

    def col(section):
        return pl.BlockSpec((None, seq, LANES),
                            lambda b, hp: (b, 0, base + section * (ATT_WIDTH // LANES) + hp))

    return pl.pallas_call(
        _attn_body,
        grid=(bsz, pairs),
        in_specs=[col(0), col(1), col(2),
                  pl.BlockSpec((n_pat, 2, 3, ATT_QBLK, ATT_KWIN), lambda b, hp: (0, hp, 0, 0, 0))],
        out_specs=pl.BlockSpec((None, seq, LANES), lambda b, hp: (b, 0, hp)),
        out_shape=jax.ShapeDtypeStruct((bsz, seq, ATT_WIDTH), BF16),
        scratch_shapes=[pltpu.VMEM((seq, LANES), F32)] * 3
                       + [pltpu.VMEM((seq, LANES), BF16)]
                       + [pltpu.VMEM((ATT_COARSE, seq // ATT_COARSE, LANES), F32)] * 3
                       + [pltpu.VMEM((ATT_CLASSES * seq // ATT_COARSE, LANES), BF16)] * 3
                       + [pltpu.VMEM((ATT_CLASSES * seq // ATT_COARSE, LANES), F32)] * 4
                       + [pltpu.VMEM((n_pat, seq, LANES), F32)] * 2,
        compiler_params=_cparams(("arbitrary", "arbitrary"), 56),
        name="attention",
    )(proj3, proj3, proj3, bias_tab)


def _mix_body(ret_ref, att_ref, gr_ref, ga_ref, x_ref, mod_ref, gain_ref,
              wr_ref, wa_ref, wo_ref, wrt_ref, x1_ref, hslab_ref, hrow_ref, logit_ref, slab_scr):
    tm = x_ref.shape[0]
    shift_f, scale_f, gate_m = mod_ref[0, 3:4, :], mod_ref[0, 4:5, :], mod_ref[0, 2:3, :]
    for r in range(tm // MIX_ROWS):
        rows = pl.ds(r * MIX_ROWS, MIX_ROWS)
        y_ret = jnp.dot(ret_ref[rows, :], wr_ref[...], preferred_element_type=F32)
        y_att = jnp.dot(att_ref[rows, :], wa_ref[...], preferred_element_type=F32)
        merged = (_sigmoid(gr_ref[rows, :].astype(F32)) * y_ret
                  + _sigmoid(ga_ref[rows, :].astype(F32)) * y_att)
        mixed = jnp.dot(merged.astype(BF16), wo_ref[...], preferred_element_type=F32)
        x1 = x_ref[rows, :] + gate_m * mixed
        x1_ref[rows, :] = x1
        h = _rms(x1, gain_ref[...]) * (1.0 + scale_f) + shift_f
        hrow_ref[rows, :] = h.astype(BF16)
        _rows_to_tiles(h, slab_scr, hslab_ref, first=r * MIX_ROWS)
        h_hi = h.astype(BF16)
        h_lo = (h - h_hi.astype(F32)).astype(BF16)
        logit_ref[rows, :] = (jnp.dot(h_hi, wrt_ref[0], preferred_element_type=F32)
                              + (jnp.dot(h_lo, wrt_ref[0], preferred_element_type=F32)
                                 + jnp.dot(h_hi, wrt_ref[1], preferred_element_type=F32)))


def _mix(ret2, att2, proj2, x2, mod3, gain, wr, wa, wo, w_router, seq, tm=512):
    t, d = x2.shape
    gate_base = (4 * RET_WIDTH + 3 * ATT_WIDTH) // d
    row = lambda i: (i, 0)
    const = lambda i: (0, 0)
    return pl.pallas_call(
        _mix_body,
        grid=(t // tm,),
        in_specs=[pl.BlockSpec((tm, d), row), pl.BlockSpec((tm, d), row),
                  pl.BlockSpec((tm, d), lambda i: (i, gate_base)),
                  pl.BlockSpec((tm, d), lambda i: (i, gate_base + 1)),
                  pl.BlockSpec((tm, d), row),
                  pl.BlockSpec((1, 6, d), lambda i: ((i * tm) // seq, 0, 0)),
                  pl.BlockSpec((1, d), const),
                  pl.BlockSpec((d, d), const), pl.BlockSpec((d, d), const),
                  pl.BlockSpec((d, d), const),
                  pl.BlockSpec((2, d, N_EXPERTS), lambda i: (0, 0, 0))],
        out_specs=[pl.BlockSpec((tm, d), row),
                   pl.BlockSpec((tm, SLABS, LANES), lambda i: (i, 0, 0)),
                   pl.BlockSpec((tm, d), row),
                   pl.BlockSpec((tm, N_EXPERTS), row)],
        out_shape=[jax.ShapeDtypeStruct((t, d), F32),
                   jax.ShapeDtypeStruct((t, SLABS, LANES), BF16),
                   jax.ShapeDtypeStruct((t, d), BF16),
                   jax.ShapeDtypeStruct((t, N_EXPERTS), F32)],
        scratch_shapes=[pltpu.VMEM((tm * SLABS, LANES), F32)],
        compiler_params=_cparams(("arbitrary",), 56),
        name="mix",
    )(ret2, att2, proj2, proj2, x2, mod3, gain, wr, wa, wo, w_router)


def _route_body(logit_ref, bias_ref, eidx_ref, gate_ref, rank_ref, cnt_ref, carry_scr, tri_scr):
    tn = logit_ref.shape[0]
    per_group = N_EXPERTS // N_GROUPS

    @pl.when(pl.program_id(0) == 0)
    def _():
        carry_scr[...] = jnp.zeros_like(carry_scr)
        r = lax.broadcasted_iota(I32, (tn, tn), 0)
        c = lax.broadcasted_iota(I32, (tn, tn), 1)
        tri_scr[...] = jnp.where(r < c, 1.0, 0.0).astype(BF16)

    scores = _sigmoid(logit_ref[...]).T
    choice = scores + jnp.concatenate([bias_ref[...]] * (tn // LANES), axis=1)
    neg_inf = -jnp.inf

    sub = lax.broadcasted_iota(I32, (per_group, tn), 0).astype(F32)
    group_score = []
    for g in range(N_GROUPS):
        cg = choice[g * per_group:(g + 1) * per_group, :]
        m1 = jnp.max(cg, axis=0, keepdims=True)
        first = jnp.min(jnp.where(cg == m1, sub, float(per_group)), axis=0, keepdims=True)
        m2 = jnp.max(jnp.where(sub == first, neg_inf, cg), axis=0, keepdims=True)
        group_score.append(m1 + m2)

    masked = []
    for a in range(N_GROUPS):
        beaten = jnp.zeros((1, tn), F32)
        for b in range(N_GROUPS):
            if b == a:
                continue
            wins = (group_score[b] >= group_score[a]) if b < a else (group_score[b] > group_score[a])
            beaten = beaten + jnp.where(wins, 1.0, 0.0)
        keep = beaten < float(TOPK_GROUPS)
        masked.append(jnp.where(keep, choice[a * per_group:(a + 1) * per_group, :], neg_inf))
    work = jnp.concatenate(masked, axis=0)

    eid = lax.broadcasted_iota(I32, (N_EXPERTS, tn), 0).astype(F32)
    picked = jnp.zeros((N_EXPERTS, tn), F32)
    idx_rows, gate_rows = [], []
    for _ in range(TOP_K):
        m = jnp.max(work, axis=0, keepdims=True)
        idx = jnp.min(jnp.where(work == m, eid, float(N_EXPERTS)), axis=0, keepdims=True)
        sel = eid == idx
        gate_rows.append(jnp.sum(jnp.where(sel, scores, 0.0), axis=0, keepdims=True))
        picked = picked + jnp.where(sel, 1.0, 0.0)
        work = jnp.where(sel, neg_inf, work)
        idx_rows.append(idx)

    before = (jnp.dot(picked.astype(BF16), tri_scr[...], preferred_element_type=F32)
              + carry_scr[:, 0:1])
    rank_rows = [jnp.sum(jnp.where(eid == idx, before, 0.0), axis=0, keepdims=True)
                 for idx in idx_rows]
    carry = carry_scr[...] + jnp.sum(picked, axis=1, keepdims=True)
    carry_scr[...] = carry
    cnt_ref[...] = carry

    gates = jnp.concatenate(gate_rows, axis=0)
    gates = gates / jnp.sum(gates, axis=0, keepdims=True) * ROUTED_SCALE
    eidx_ref[...] = jnp.concatenate(idx_rows, axis=0).astype(I32)
    gate_ref[...] = gates
    rank_ref[...] = jnp.concatenate(rank_rows, axis=0).astype(I32)


def _route(logits, bias_b, tn=512):
    t = logits.shape[0]
    tok = lambda i: (0, i)
    return pl.pallas_call(
        _route_body,
        grid=(t // tn,),
        in_specs=[pl.BlockSpec((tn, N_EXPERTS), lambda i: (i, 0)),
                  pl.BlockSpec((N_EXPERTS, LANES), lambda i: (0, 0))],
        out_specs=[pl.BlockSpec((TOP_K, tn), tok), pl.BlockSpec((TOP_K, tn), tok),
                   pl.BlockSpec((TOP_K, tn), tok),
                   pl.BlockSpec((N_EXPERTS, LANES), lambda i: (0, 0))],
        out_shape=[jax.ShapeDtypeStruct((TOP_K, t), I32), jax.ShapeDtypeStruct((TOP_K, t), F32),
                   jax.ShapeDtypeStruct((TOP_K, t), I32),
                   jax.ShapeDtypeStruct((N_EXPERTS, LANES), F32)],
        scratch_shapes=[pltpu.VMEM((N_EXPERTS, LANES), F32), pltpu.VMEM((tn, tn), BF16)],
        compiler_params=_cparams(("arbitrary",), 48),
        name="route",
    )(logits, bias_b)


def _dest_body(start_ref, eidx_ref, rank_ref, dest_ref):
    e = eidx_ref[...]

    def body(j, acc):
        return jnp.where(e == j, start_ref[j], acc)

    dest_ref[...] = rank_ref[...] + lax.fori_loop(0, N_EXPERTS, body, jnp.zeros_like(e))


def _dest(pad_start, eidx, rank, tn=2048):
    t = eidx.shape[1]
    tok = lambda i, s: (0, i)
    return pl.pallas_call(
        _dest_body,
        grid_spec=pltpu.PrefetchScalarGridSpec(
            num_scalar_prefetch=1,
            grid=(t // tn,),
            in_specs=[pl.BlockSpec((TOP_K, tn), tok), pl.BlockSpec((TOP_K, tn), tok)],
            out_specs=pl.BlockSpec((TOP_K, tn), tok)),
        out_shape=jax.ShapeDtypeStruct((TOP_K, t), I32),
        compiler_params=_cparams(("arbitrary",), 32),
        name="dest",
    )(pad_start, eidx, rank)


def _dispatch_body(dest_hbm, h_ref, xs_hbm, dest_smem, idx_sem, row_sem):
    tt = dest_smem.shape[0] // TOP_K
    idx_copy = pltpu.make_async_copy(dest_hbm.at[pl.program_id(0)], dest_smem, idx_sem)
    idx_copy.start()
    idx_copy.wait()

    def issue(t, carry):
        for k in range(TOP_K):
            pltpu.make_async_copy(h_ref.at[t], xs_hbm.at[dest_smem[t * TOP_K + k]],
                                  row_sem).start(priority=k % 2)
        return carry

    lax.fori_loop(0, tt, issue, 0)

    for k in range(TOP_K):
        pltpu.make_async_copy(h_ref, xs_hbm.at[pl.ds(0, tt)], row_sem).wait()


def _dispatch(dest_tiles, htiles, n_rows):
    n_tiles, width = dest_tiles.shape
    tt = width // TOP_K
    return pl.pallas_call(
        _dispatch_body,
        grid=(n_tiles,),
        in_specs=[pl.BlockSpec(memory_space=pl.ANY),
                  pl.BlockSpec((tt, SLABS, LANES), lambda i: (i, 0, 0))],
        out_specs=pl.BlockSpec(memory_space=pl.ANY),
        out_shape=jax.ShapeDtypeStruct((n_rows, SLABS, LANES), BF16),
        scratch_shapes=[pltpu.SMEM((width,), I32), pltpu.SemaphoreType.DMA, pltpu.SemaphoreType.DMA],
        compiler_params=_cparams(("arbitrary",), 32),
        name="dispatch",
    )(dest_tiles, htiles)


def _experts_body(bexp_ref, bnew_ref, nused_ref, bnext_ref, bslot_ref, xs_hbm, wg_hbm, wu_hbm, wd_hbm,
                  ys_ref, wgu_s, wd_s, stage_scr, xring, wg_buf, wu_buf, wd_buf, ring_sem, w_sem):
    i = pl.program_id(0)
    hid = wd_s.shape[0]
    n_used = nused_ref[0]

    def block_copy(j):
        slot = j % EXPERT_RING
        src = xs_hbm.at[pl.ds(pl.multiple_of(j * MOE_BLOCK, MOE_BLOCK), MOE_BLOCK)]
        return pltpu.make_async_copy(src, xring.at[slot], ring_sem.at[slot])

    def weight_copies(expert, slot):
        return [pltpu.make_async_copy(src.at[expert], dst.at[slot], w_sem.at[slot])
                for src, dst in ((wg_hbm, wg_buf), (wu_hbm, wu_buf), (wd_hbm, wd_buf))]

    @pl.when(i == 0)
    def _():
        for copy in weight_copies(bexp_ref[0], 0):
            copy.start()
        for j in range(EXPERT_RING - 1):
            @pl.when(j < n_used)
            def _(j=j):
                block_copy(j).start()

    @pl.when(i + EXPERT_RING - 1 < n_used)
    def _():
        block_copy(i + EXPERT_RING - 1).start()

    @pl.when(i < n_used)
    def _():
        @pl.when(bnew_ref[i] == 1)
        def _():
            slot = bslot_ref[i]
            for copy in weight_copies(bexp_ref[i], slot):
                copy.wait()
            nxt = bnext_ref[i]

            @pl.when(nxt >= 0)
            def _():
                for copy in weight_copies(nxt, 1 - slot):
                    copy.start()

            wgu_s[:, :hid] = wg_buf[slot].astype(BF16)
            wgu_s[:, hid:] = wu_buf[slot].astype(BF16)
            wd_s[...] = wd_buf[slot].astype(BF16)

        block_copy(i).wait()
        x = _tiles_to_rows(xring[i % EXPERT_RING], stage_scr).astype(BF16)
        gu = jnp.dot(x, wgu_s[...], preferred_element_type=F32)
        hg, hu = gu[:, :hid], gu[:, hid:]
        y = jnp.dot((_silu(hg) * hu).astype(BF16), wd_s[...], preferred_element_type=F32)
        _rows_to_tiles(y, stage_scr, ys_ref)


def _experts(block_expert, block_new, n_used, block_next, block_slot, xs, w_gate, w_up, w_down):
    n_blocks = block_expert.shape[0]
    d, hid = w_gate.shape[1], w_gate.shape[2]
    tile_block = (MOE_BLOCK, SLABS, LANES)
    hbm = pl.BlockSpec(memory_space=pl.ANY)

    def blk(i, be, bn, nu, bx, bs):
        return (jnp.minimum(i, nu[0] - 1), 0, 0)

    return pl.pallas_call(
        _experts_body,
        grid_spec=pltpu.PrefetchScalarGridSpec(
            num_scalar_prefetch=5,
            grid=(n_blocks,),
            in_specs=[hbm, hbm, hbm, hbm],
            out_specs=pl.BlockSpec(tile_block, blk),
            scratch_shapes=[pltpu.VMEM((d, 2 * hid), BF16), pltpu.VMEM((hid, d), BF16),
                            pltpu.VMEM((MOE_BLOCK * SLABS, LANES), F32),
                            pltpu.VMEM((EXPERT_RING,) + tile_block, BF16),
                            pltpu.VMEM((2, d, hid), F32), pltpu.VMEM((2, d, hid), F32),
                            pltpu.VMEM((2, hid, d), F32),
                            pltpu.SemaphoreType.DMA((EXPERT_RING,)),
                            pltpu.SemaphoreType.DMA((2,))]),
        out_shape=jax.ShapeDtypeStruct(xs.shape, BF16),
        compiler_params=_cparams(("arbitrary",), 32),
        name="experts",
    )(block_expert, block_new, n_used, block_next, block_slot, xs, w_gate, w_up, w_down)


def _combine_body(dest_hbm, gate_hbm, ys_hbm, x1_ref, h_ref, wsg_ref, wsu_ref, wsd_ref,
                  mod_ref, gain_ref, o_ref, dest_smem0, dest_smem1, gate_smem, buf, shared_scr,
                  routed_scr, idx_sem, gate_sem, row_sem, *, final_norm):
    tc = x1_ref.shape[0]
    i = pl.program_id(0)
    has_next = i + 1 < pl.num_programs(0)
    cur = i % 2
    dest_smem = (dest_smem0, dest_smem1)

    def index_copy(tile, slot):
        return pltpu.make_async_copy(dest_hbm.at[tile], dest_smem[slot], idx_sem)

    def start_rows(slot):
        def issue(t, carry):
            for k in range(TOP_K):
                pltpu.make_async_copy(ys_hbm.at[dest_smem[slot][t * TOP_K + k]],
                                      buf.at[slot, t * TOP_K + k],
                                      row_sem.at[slot]).start(priority=k % 2)
            return carry

        lax.fori_loop(0, tc, issue, 0)

    gate_copy = pltpu.make_async_copy(gate_hbm.at[i], gate_smem, gate_sem)
    gate_copy.start()

    @pl.when(i == 0)
    def _():
        index_copy(0, 0).start()
        index_copy(0, 0).wait()
        start_rows(0)

    for slot in range(2):
        @pl.when(jnp.logical_and(has_next, cur != slot))
        def _(slot=slot):
            index_copy(i + 1, slot).start()

    h = h_ref[...]
    hid = (_silu(jnp.dot(h, wsg_ref[...], preferred_element_type=F32))
           * jnp.dot(h, wsu_ref[...], preferred_element_type=F32))
    shared_scr[...] = jnp.dot(hid.astype(BF16), wsd_ref[...], preferred_element_type=F32)

    for slot in range(2):
        @pl.when(jnp.logical_and(has_next, cur != slot))
        def _(slot=slot):
            index_copy(i + 1, slot).wait()
            start_rows(slot)

    pltpu.make_async_copy(ys_hbm.at[pl.ds(0, tc * TOP_K)], buf.at[cur], row_sem.at[cur]).wait()
    gate_copy.wait()

    def weigh(g, carry):
        for u in range(COMBINE_UNROLL):
            t = g * COMBINE_UNROLL + u
            acc = gate_smem[t * TOP_K] * buf[cur, t * TOP_K].astype(F32)
            for k in range(1, TOP_K):
                acc = acc + gate_smem[t * TOP_K + k] * buf[cur, t * TOP_K + k].astype(F32)
            routed_scr[pl.ds(pl.multiple_of(t * SLABS, SLABS), SLABS), :] = acc
        return carry

    lax.fori_loop(0, tc // COMBINE_UNROLL, weigh, 0)

    gate_f = mod_ref[0, 5:6, :]
    gain = gain_ref[...]

    def finish(c, carry):
        r0 = pl.multiple_of(c * COMBINE_ROWS, COMBINE_ROWS)
        rows = pl.ds(r0, COMBINE_ROWS)
        routed = jnp.concatenate(
            [routed_scr[pl.ds(r0 * SLABS + s, COMBINE_ROWS, stride=SLABS), :]
             for s in range(SLABS)], axis=1)
        x2 = x1_ref[rows, :] + gate_f * (shared_scr[rows, :] + routed)
        o_ref[rows, :] = _rms(x2, gain) if final_norm else x2
        return carry

    lax.fori_loop(0, tc // COMBINE_ROWS, finish, 0)


def _combine(dest_tiles, gate_tiles, ys, x1, hrow, wsg, wsu, wsd, mod3, gain, seq, tc, final_norm):
    t, d = x1.shape
    hid = wsg.shape[1]
    row = lambda i: (i, 0)
    const = lambda i: (0, 0)
    return pl.pallas_call(
        functools.partial(_combine_body, final_norm=final_norm),
        grid=(t // tc,),
        in_specs=[pl.BlockSpec(memory_space=pl.ANY),
                  pl.BlockSpec(memory_space=pl.ANY),
                  pl.BlockSpec(memory_space=pl.ANY),
                  pl.BlockSpec((tc, d), row), pl.BlockSpec((tc, d), row),
                  pl.BlockSpec((d, hid), const), pl.BlockSpec((d, hid), const),
                  pl.BlockSpec((hid, d), const),
                  pl.BlockSpec((1, 6, d), lambda i: ((i * tc) // seq, 0, 0)),
                  pl.BlockSpec((1, d), const)],
        out_specs=pl.BlockSpec((tc, d), row),
        out_shape=jax.ShapeDtypeStruct((t, d), F32),
        scratch_shapes=[pltpu.SMEM((TOP_K * tc,), I32), pltpu.SMEM((TOP_K * tc,), I32),
                        pltpu.SMEM((TOP_K * tc,), F32),
                        pltpu.VMEM((2, tc * TOP_K, SLABS, LANES), BF16),
                        pltpu.VMEM((tc, d), F32),
                        pltpu.VMEM((tc * SLABS, LANES), F32),
                        pltpu.SemaphoreType.DMA, pltpu.SemaphoreType.DMA,
                        pltpu.SemaphoreType.DMA((2,))],
        compiler_params=_cparams(("arbitrary",), 56),
        name="combine",
    )(dest_tiles, gate_tiles, ys, x1, hrow, wsg, wsu, wsd, mod3, gain)


def _split_bf16(w):
    hi = w.astype(BF16)
    lo = (w.astype(F32) - hi.astype(F32)).astype(BF16)
    return jnp.stack([hi, lo], axis=0)


def _tile_major(a, tile):
    k, t = a.shape
    return a.reshape(k, t // tile, tile).transpose(1, 2, 0).reshape(t // tile, tile * k)


def kernel(x, c, w_ada, b_ada, norm_mix, w_in, ret_decay, t5_bias, w_ret_up, w_att_up, w_o,
           norm_ffn, w_router, router_bias, w_gate, w_up, w_down, ws_gate, ws_up, ws_down, norm_final):
    bsz, seq, d = x.shape
    depth = w_ada.shape[0]
    t = bsz * seq
    assert d == D_MODEL and seq % (ATT_KWIN * DILATION_PATTERNS[-1][1]) == 0

    half = RET_HEAD_DIM // 2
    inv_freq = ROPE_BASE ** (-jnp.arange(half, dtype=F32) / half)
    ang = jnp.arange(seq, dtype=F32)[:, None] * inv_freq[None, :]
    cos, sin = jnp.cos(ang), jnp.sin(ang)
    bias_tab = _attention_bias(t5_bias)

    n_assign = t * TOP_K
    n_blocks = -(-n_assign // MOE_BLOCK) + N_EXPERTS
    n_rows = n_blocks * MOE_BLOCK
    disp_tile = min(2048, t)
    comb_tile = min(256, t)

    x2 = x.reshape(t, d)
    for layer in range(depth):
        mod3 = _adaln(c, w_ada[layer], b_ada[layer]).reshape(bsz, 6, d)
        proj = _inproj(x2, norm_mix[layer].reshape(1, d), mod3, w_in[layer].astype(BF16), seq)
        proj3 = proj.reshape(bsz, seq, PROJ_WIDTH)
        log_gamma = jnp.log1p(-jnp.exp(ret_decay[layer].astype(F32)))
        ret = _retention(proj3, log_gamma, cos, sin)
        att = _attention(proj3, bias_tab)
        x1, hslab, hrow, logits = _mix(
            ret.reshape(t, RET_WIDTH), att.reshape(t, ATT_WIDTH), proj, x2, mod3,
            norm_ffn[layer].reshape(1, d), w_ret_up[layer].astype(BF16),
            w_att_up[layer].astype(BF16), w_o[layer].astype(BF16), _split_bf16(w_router[layer]),
            seq)

        bias_b = jnp.broadcast_to(router_bias[layer].astype(F32)[:, None], (N_EXPERTS, LANES))
        eidx, gate, rank, counts = _route(logits, bias_b)

        counts = counts[:, 0].astype(I32)
        padded = (counts + MOE_BLOCK - 1) // MOE_BLOCK * MOE_BLOCK
        pad_end = jnp.cumsum(padded)
        pad_start = pad_end - padded
        block_row = jnp.arange(n_blocks, dtype=I32) * MOE_BLOCK
        block_expert = jnp.minimum(
            jnp.sum((pad_end[None, :] <= block_row[:, None]).astype(I32), axis=1), N_EXPERTS - 1)
        block_new = jnp.concatenate(
            [jnp.ones((1,), I32), (block_expert[1:] != block_expert[:-1]).astype(I32)])
        n_used = (pad_end[-1:] // MOE_BLOCK).astype(I32)

        dest = _dest(pad_start.astype(I32), eidx, rank)
        xs = _dispatch(_tile_major(dest, disp_tile), hslab, n_rows)
        expert_ids = jnp.arange(N_EXPERTS, dtype=I32)
        later = jnp.logical_and(expert_ids[None, :] > block_expert[:, None], (padded > 0)[None, :])
        block_next = jnp.min(jnp.where(later, expert_ids[None, :], N_EXPERTS), axis=1)
        block_next = jnp.where(block_next == N_EXPERTS, -1, block_next).astype(I32)
        block_slot = ((jnp.cumsum(block_new) - 1) % 2).astype(I32)
        ys = _experts(block_expert, block_new, n_used, block_next, block_slot, xs,
                      w_gate[layer], w_up[layer], w_down[layer])
        x2 = _combine(_tile_major(dest, comb_tile), _tile_major(gate, comb_tile), ys,
                      x1, hrow, ws_gate[layer].astype(BF16), ws_up[layer].astype(BF16),
                      ws_down[layer].astype(BF16), mod3, norm_final.reshape(1, d), seq, comb_tile,
                      final_norm=(layer == depth - 1))
    return x2.reshape(bsz, seq, d)
```

```python
import functools
import math

import jax
import jax.numpy as jnp
import numpy as np
from jax import lax
from jax.experimental import pallas as pl
from jax.experimental.pallas import tpu as pltpu

F32 = jnp.float32
BF16 = jnp.bfloat16
I32 = jnp.int32

D_MODEL = 1024
RET_HEADS = 4
RET_HEAD_DIM = 256
RET_WIDTH = RET_HEADS * RET_HEAD_DIM
RET_CHUNK = 256
ROPE_BASE = 10000.0
ATT_HEADS = 16
ATT_HEAD_DIM = 64
ATT_WIDTH = ATT_HEADS * ATT_HEAD_DIM
DILATION_PATTERNS = ((128, 1), (512, 4), (2048, 16))
REL_BUCKETS = 32
REL_MAX_DISTANCE = 1024
N_EXPERTS = 256
TOP_K = 8
N_GROUPS = 8
TOPK_GROUPS = 4
EXPERT_HIDDEN = 256
ROUTED_SCALE = 2.5
MOE_BLOCK = 256
NORM_EPS = 1e-6
PROJ_WIDTH = 4 * RET_WIDTH + 3 * ATT_WIDTH + 2 * D_MODEL

LANES = 128
SUBLANES = 8
SLABS = D_MODEL // LANES
ATT_RADIUS = 64
ATT_QBLK = 128
ATT_KWIN = 256
MIX_ROWS = 256
RET_UNROLL = 8
EXPERT_RING = 3
COMBINE_ROWS = 32
COMBINE_UNROLL = 8
ATT_COARSE = 4
ATT_CLASSES = 4
ATT_UNROLL = 32
NEG_BIG = -1e30
MIB = 1024 * 1024


def _cparams(sem, vmem_mib):
    return pltpu.CompilerParams(dimension_semantics=sem, vmem_limit_bytes=vmem_mib * MIB)


def _sigmoid(x):
    return 1.0 / (1.0 + jnp.exp(-x))


def _silu(x):
    return x * _sigmoid(x)


def _rms(x, gain):
    return x * lax.rsqrt(jnp.mean(x * x, axis=-1, keepdims=True) + NORM_EPS) * gain


def _rows_to_tiles(rows, stage_scr, tiles_ref, first=0):
    n = rows.shape[0]
    base = first * SLABS
    for s in range(SLABS):
        stage_scr[pl.ds(base + s, n, stride=SLABS), :] = rows[:, s * LANES:(s + 1) * LANES]
    staged = stage_scr[pl.ds(base, n * SLABS), :]
    tiles_ref[pl.ds(first, n)] = staged.reshape(n, SLABS, LANES).astype(BF16)


def _tiles_to_rows(tiles, stage_scr):
    n = tiles.shape[0]
    stage_scr[...] = tiles.astype(F32).reshape(n * SLABS, LANES)
    return jnp.concatenate([stage_scr[pl.ds(s, n, stride=SLABS), :] for s in range(SLABS)], axis=1)


def _adaln_body(c_ref, w_ref, b_ref, o_ref):
    cond = _silu(c_ref[...])
    o_ref[...] = jnp.dot(cond, w_ref[...], preferred_element_type=F32,
                         precision=lax.Precision.HIGHEST) + b_ref[...]


def _adaln(c, w, b):
    bsz, d = c.shape
    n = w.shape[1]
    return pl.pallas_call(
        _adaln_body,
        grid=(n // d,),
        in_specs=[pl.BlockSpec((bsz, d), lambda j: (0, 0)),
                  pl.BlockSpec((d, d), lambda j: (0, j)),
                  pl.BlockSpec((1, d), lambda j: (0, j))],
        out_specs=pl.BlockSpec((bsz, d), lambda j: (0, j)),
        out_shape=jax.ShapeDtypeStruct((bsz, n), F32),
        compiler_params=_cparams(("arbitrary",), 32),
        name="adaln",
    )(c, w, b.reshape(1, n))


def _inproj_body(x_ref, gain_ref, mod_ref, w_ref, o_ref, h_scr):
    @pl.when(pl.program_id(1) == 0)
    def _():
        y = _rms(x_ref[...], gain_ref[...])
        h = y * (1.0 + mod_ref[0, 1:2, :]) + mod_ref[0, 0:1, :]
        h_scr[...] = h.astype(BF16)

    o_ref[...] = jnp.dot(h_scr[...], w_ref[...], preferred_element_type=F32).astype(BF16)


def _inproj(x2, gain, mod3, w_bf, seq, tm=1024, tn=4608):
    t, d = x2.shape
    n = w_bf.shape[1]
    return pl.pallas_call(
        _inproj_body,
        grid=(t // tm, n // tn),
        in_specs=[pl.BlockSpec((tm, d), lambda i, j: (i, 0)),
                  pl.BlockSpec((1, d), lambda i, j: (0, 0)),
                  pl.BlockSpec((1, 6, d), lambda i, j: ((i * tm) // seq, 0, 0)),
                  pl.BlockSpec((d, tn), lambda i, j: (0, j))],
        out_specs=pl.BlockSpec((tm, tn), lambda i, j: (i, j)),
        out_shape=jax.ShapeDtypeStruct((t, n), BF16),
        scratch_shapes=[pltpu.VMEM((tm, d), BF16)],
        compiler_params=_cparams(("arbitrary", "arbitrary"), 56),
        name="inproj",
    )(x2, gain, mod3, w_bf)


def _ret_body(lg_ref, q_ref, k_ref, v_ref, g_ref, cos_ref, sin_ref, o_ref,
              qr_scr, kr_scr, accf_scr, accb_scr, stf_scr, stb_scr):
    head = pl.program_id(1)
    seq = q_ref.shape[0]
    n_chunks = seq // RET_CHUNK
    half = RET_HEAD_DIM // 2
    rot_rows = 256

    def rot_step(c, carry):
        r = pl.ds(pl.multiple_of(c * rot_rows, rot_rows), rot_rows)
        cs = cos_ref[r, :]
        sn = sin_ref[r, :]
        for src, dst, scale in ((q_ref, qr_scr, 1.0), (k_ref, kr_scr, RET_HEAD_DIM ** -0.5)):
            t = src[r, :].astype(F32)
            t1, t2 = t[:, :half], t[:, half:]
            dst[r, :half] = ((t1 * cs - t2 * sn) * scale).astype(BF16)
            dst[r, half:] = ((t1 * sn + t2 * cs) * scale).astype(BF16)
        return carry

    lax.fori_loop(0, seq // rot_rows, rot_step, 0)

    ri = lax.broadcasted_iota(I32, (RET_CHUNK, RET_CHUNK), 0)
    ci = lax.broadcasted_iota(I32, (RET_CHUNK, RET_CHUNK), 1)
    rowpos = lax.broadcasted_iota(I32, (RET_CHUNK, RET_HEAD_DIM), 0).astype(F32)

    def decay_tables(lg, forward):
        if forward:
            diff = (ri - ci).astype(F32)
            allowed = ri >= ci
            q_scale = jnp.exp(lg * (rowpos + 1.0))
            k_scale = jnp.exp(lg * (RET_CHUNK - 1.0 - rowpos))
        else:
            diff = (ci - ri).astype(F32)
            allowed = ci > ri
            q_scale = jnp.exp(lg * (RET_CHUNK - rowpos))
            k_scale = jnp.exp(lg * rowpos)
        intra = jnp.where(allowed, jnp.exp(lg * jnp.where(allowed, diff, 0.0)), 0.0)
        chunk_decay = jnp.exp(lg * jnp.full((1, RET_HEAD_DIM), float(RET_CHUNK), F32))
        return intra, q_scale, k_scale, chunk_decay

    def chunk_update(c, tables, st_scr, acc_scr):
        intra, q_scale, k_scale, chunk_decay = tables
        r = pl.ds(pl.multiple_of(c * RET_CHUNK, RET_CHUNK), RET_CHUNK)
        q = qr_scr[r, :]
        k = kr_scr[r, :]
        v = v_ref[r, :]
        scores = lax.dot_general(q, k, (((1,), (1,)), ((), ())),
                                 preferred_element_type=F32) * intra
        state = st_scr[...]
        acc_scr[r, :] = (jnp.dot(scores.astype(BF16), v, preferred_element_type=F32)
                         + jnp.dot((q.astype(F32) * q_scale).astype(BF16), state.astype(BF16),
                                   preferred_element_type=F32))
        k_t = (k.astype(F32) * k_scale).T.astype(BF16)
        st_scr[...] = state * chunk_decay + jnp.dot(k_t, v, preferred_element_type=F32)

    fwd_tables = decay_tables(lg_ref[0, head], True)
    bwd_tables = decay_tables(lg_ref[1, head], False)
    stf_scr[...] = jnp.zeros_like(stf_scr)
    stb_scr[...] = jnp.zeros_like(stb_scr)

    def step(trip, carry):
        for u in range(RET_UNROLL):
            i = trip * RET_UNROLL + u
            chunk_update(i, fwd_tables, stf_scr, accf_scr)
            chunk_update(n_chunks - 1 - i, bwd_tables, stb_scr, accb_scr)
        return carry

    lax.fori_loop(0, n_chunks // RET_UNROLL, step, 0)

    def finish(c, carry):
        r = pl.ds(pl.multiple_of(c * rot_rows, rot_rows), rot_rows)
        o = accf_scr[r, :] + accb_scr[r, :]
        mu = jnp.mean(o, axis=-1, keepdims=True)
        oc = o - mu
        var = jnp.mean(oc * oc, axis=-1, keepdims=True)
        o = oc * lax.rsqrt(var + NORM_EPS)
        o_ref[r, :] = (o * _silu(g_ref[r, :].astype(F32))).astype(BF16)
        return carry

    lax.fori_loop(0, seq // rot_rows, finish, 0)


def _retention(proj3, log_gamma, cos, sin):
    bsz, seq, _ = proj3.shape
    hd = RET_HEAD_DIM

    def col(section):
        return pl.BlockSpec((None, seq, hd), lambda b, h, lg: (b, 0, section * RET_HEADS + h))

    return pl.pallas_call(
        _ret_body,
        grid_spec=pltpu.PrefetchScalarGridSpec(
            num_scalar_prefetch=1,
            grid=(bsz, RET_HEADS),
            in_specs=[col(0), col(1), col(2), col(3),
                      pl.BlockSpec((seq, hd // 2), lambda b, h, lg: (0, 0)),
                      pl.BlockSpec((seq, hd // 2), lambda b, h, lg: (0, 0))],
            out_specs=pl.BlockSpec((None, seq, hd), lambda b, h, lg: (b, 0, h)),
            scratch_shapes=[pltpu.VMEM((seq, hd), BF16), pltpu.VMEM((seq, hd), BF16),
                            pltpu.VMEM((seq, hd), F32), pltpu.VMEM((seq, hd), F32),
                            pltpu.VMEM((hd, hd), F32), pltpu.VMEM((hd, hd), F32)]),
        out_shape=jax.ShapeDtypeStruct((bsz, seq, RET_WIDTH), BF16),
        compiler_params=_cparams(("arbitrary", "arbitrary"), 56),
        name="retention",
    )(log_gamma, proj3, proj3, proj3, proj3, cos, sin)


def _t5_bucket(rel):
    half = REL_BUCKETS // 2
    max_exact = half // 2
    n = jnp.abs(rel)
    large = max_exact + (jnp.log(jnp.maximum(n, 1).astype(F32) / max_exact)
                         / math.log(REL_MAX_DISTANCE / max_exact) * (half - max_exact)).astype(I32)
    large = jnp.minimum(large, half - 1)
    return jnp.where(rel > 0, half, 0) + jnp.where(n < max_exact, n, large)


def _band_buckets():
    qi = jnp.arange(ATT_QBLK, dtype=I32)[:, None]
    kj = jnp.arange(ATT_KWIN, dtype=I32)[None, :]
    tables = []
    for _, dilation in DILATION_PATTERNS:
        cases = []
        for offset in (0, -ATT_RADIUS, ATT_QBLK - ATT_KWIN):
            rel = kj + offset - qi
            cases.append(jnp.where(jnp.abs(rel) <= ATT_RADIUS, _t5_bucket(rel * dilation), -1))
        tables.append(jnp.stack(cases, axis=0))
    return jnp.stack(tables, axis=0)


def _bias_body(t5_ref, bucket_ref, o_ref):
    bucket = bucket_ref[...]

    def head(h, carry):
        acc = jnp.full(bucket.shape, NEG_BIG, F32)
        for b in range(REL_BUCKETS):
            acc = jnp.where(bucket == b, t5_ref[b, h], acc)
        o_ref[h] = acc
        return carry

    lax.fori_loop(0, ATT_HEADS, head, 0)


def _attention_bias(t5_bias):
    buckets = _band_buckets()
    n_pat, n_case = buckets.shape[:2]
    return pl.pallas_call(
        _bias_body,
        grid_spec=pltpu.PrefetchScalarGridSpec(
            num_scalar_prefetch=1,
            grid=(n_pat, n_case),
            in_specs=[pl.BlockSpec((None, None, ATT_QBLK, ATT_KWIN), lambda p, c, t5: (p, c, 0, 0))],
            out_specs=pl.BlockSpec((None, ATT_HEADS, None, ATT_QBLK, ATT_KWIN),
                                   lambda p, c, t5: (p, 0, c, 0, 0))),
        out_shape=jax.ShapeDtypeStruct((n_pat, ATT_HEADS, n_case, ATT_QBLK, ATT_KWIN), F32),
        compiler_params=_cparams(("arbitrary", "arbitrary"), 32),
        name="attn_bias",
    )(t5_bias.astype(F32), buckets)


def _attn_body(q_ref, k_ref, v_ref, bias_ref, o_ref,
               qf, kf, vf, qb, q4, k4, v4, qd, kd, vd, od, ld, o4, l4, og, lgs):
    seq = q_ref.shape[0]
    rows = 256
    lane = lax.broadcasted_iota(I32, (ATT_QBLK, LANES), 1)
    head0 = lane < ATT_HEAD_DIM

    def to_f32(c, carry):
        r = pl.ds(pl.multiple_of(c * rows, rows), rows)
        q = q_ref[r, :].astype(F32) * (ATT_HEAD_DIM ** -0.5)
        qf[r, :] = q
        qb[r, :] = q.astype(BF16)
        kf[r, :] = k_ref[r, :].astype(F32)
        vf[r, :] = v_ref[r, :].astype(F32)
        return carry

    lax.fori_loop(0, seq // rows, to_f32, 0)

    def band_blocks(g, length, n_seg, q_src, k_src, v_src, o_dst, l_dst):
        n_qb = length // ATT_QBLK
        n_blocks = n_seg * n_qb
        unroll = min(ATT_UNROLL, n_blocks)

        def qgroup(it, carry):
            for u in range(unroll):
                qblock(it * unroll + u)
            return carry

        def qblock(b):
            qi = b % n_qb
            base = (b // n_qb) * length
            qs = pl.multiple_of(base + qi * ATT_QBLK, ATT_QBLK)
            ws = pl.multiple_of(
                base + jnp.clip(qi * ATT_QBLK - ATT_RADIUS, 0, length - ATT_KWIN), ATT_RADIUS)
            case = jnp.where(qi == 0, 0, jnp.where(qi == n_qb - 1, 2, 1))
            q = q_src[pl.ds(qs, ATT_QBLK), :]
            k = k_src[pl.ds(ws, ATT_KWIN), :]
            v = v_src[pl.ds(ws, ATT_KWIN), :]
            v_ones = jnp.concatenate([v, jnp.ones_like(v)], axis=1)
            outs, lses = [], []
            for hh in range(2):
                mask = head0 if hh == 0 else jnp.logical_not(head0)
                qm = jnp.where(mask, q, jnp.zeros_like(q))
                s = lax.dot_general(qm, k, (((1,), (1,)), ((), ())), preferred_element_type=F32)
                s = s + bias_ref[g, hh, case]
                m = jnp.max(s, axis=-1, keepdims=True)
                p = jnp.exp(s - m).astype(BF16)
                ol = jnp.dot(p, v_ones, preferred_element_type=F32)
                l = ol[:, LANES:]
                outs.append(ol[:, :LANES] / l)
                lses.append(m + jnp.log(l))
            o_dst[pl.ds(qs, ATT_QBLK), :] = jnp.where(head0, outs[0], outs[1])
            l_dst[pl.ds(qs, ATT_QBLK), :] = jnp.where(head0, lses[0], lses[1])

        lax.fori_loop(0, n_blocks // unroll, qgroup, 0)

    coarse_len = seq // ATT_COARSE

    def split(c, carry):
        strided = pl.ds(c, coarse_len, stride=ATT_COARSE)
        q4[c] = qf[strided, :]
        k4[c] = kf[strided, :]
        v4[c] = vf[strided, :]
        return carry

    lax.fori_loop(0, ATT_COARSE, split, 0)

    for g, (_, dilation) in enumerate(DILATION_PATTERNS):
        length = seq // dilation
        if dilation == 1:
            band_blocks(g, length, 1, qb, k_ref, v_ref, og.at[g], lgs.at[g])
            continue
        assert dilation % ATT_COARSE == 0
        fine = dilation // ATT_COARSE
        assert fine * length == coarse_len and fine * (length // ATT_QBLK) >= min(ATT_UNROLL, fine)

        def coarse_classes(trip, carry, g=g, length=length, fine=fine):
            classes = [(trip * ATT_CLASSES + u, u * coarse_len) for u in range(ATT_CLASSES)]
            segs = [(j, pl.ds(j, length, stride=fine) if fine > 1 else pl.ds(0, length), j * length)
                    for j in range(fine)]
            for c, base in classes:
                for _, src, off in segs:
                    dense = pl.ds(base + off, length)
                    qd[dense, :] = q4[c, src, :].astype(BF16)
                    kd[dense, :] = k4[c, src, :].astype(BF16)
                    vd[dense, :] = v4[c, src, :].astype(BF16)
            band_blocks(g, length, ATT_CLASSES * fine, qd, kd, vd, od, ld)
            for c, base in classes:
                whole = pl.ds(c, coarse_len, stride=ATT_COARSE)
                staged = pl.ds(base, coarse_len)
                if fine == 1:
                    og[g, whole, :] = od[staged, :]
                    lgs[g, whole, :] = ld[staged, :]
                else:
                    for j, _, off in segs:
                        interleaved = pl.ds(base + j, length, stride=fine)
                        o4[interleaved, :] = od[pl.ds(base + off, length), :]
                        l4[interleaved, :] = ld[pl.ds(base + off, length), :]
                    og[g, whole, :] = o4[staged, :]
                    lgs[g, whole, :] = l4[staged, :]
            return carry

        lax.fori_loop(0, ATT_COARSE // ATT_CLASSES, coarse_classes, 0)

    def merge(c, carry):
        r = pl.ds(pl.multiple_of(c * rows, rows), rows)
        l0, l1, l2 = lgs[0, r, :], lgs[1, r, :], lgs[2, r, :]
        m = jnp.maximum(jnp.maximum(l0, l1), l2)
        w0, w1, w2 = jnp.exp(l0 - m), jnp.exp(l1 - m), jnp.exp(l2 - m)
        num = w0 * og[0, r, :] + w1 * og[1, r, :] + w2 * og[2, r, :]
        o_ref[r, :] = (num / (w0 + w1 + w2)).astype(BF16)
        return carry

    lax.fori_loop(0, seq // rows, merge, 0)


def _attention(proj3, bias_tab):
    bsz, seq, _ = proj3.shape
    n_pat = len(DILATION_PATTERNS)
    pairs = ATT_HEADS // 2
    base = 4 * RET_WIDTH // LANES

    def col(section):
        return pl.BlockSpec((None, seq, LANES),
                            lambda b, hp: (b, 0, base + section * (ATT_WIDTH // LANES) + hp))

    return pl.pallas_call(
        _attn_body,
        grid=(bsz, pairs),
        in_specs=[col(0), col(1), col(2),
                  pl.BlockSpec((n_pat, 2, 3, ATT_QBLK, ATT_KWIN), lambda b, hp: (0, hp, 0, 0, 0))],
        out_specs=pl.BlockSpec((None, seq, LANES), lambda b, hp: (b, 0, hp)),
        out_shape=jax.ShapeDtypeStruct((bsz, seq, ATT_WIDTH), BF16),
        scratch_shapes=[pltpu.VMEM((seq, LANES), F32)] * 3
                       + [pltpu.VMEM((seq, LANES), BF16)]
                       + [pltpu.VMEM((ATT_COARSE, seq // ATT_COARSE, LANES), F32)] * 3
                       + [pltpu.VMEM((ATT_CLASSES * seq // ATT_COARSE, LANES), BF16)] * 3
                       + [pltpu.VMEM((ATT_CLASSES * seq // ATT_COARSE, LANES), F32)] * 4
                       + [pltpu.VMEM((n_pat, seq, LANES), F32)] * 2,
        compiler_params=_cparams(("arbitrary", "arbitrary"), 56),
        name="attention",
    )(proj3, proj3, proj3, bias_tab)


def _mix_body(ret_ref, att_ref, gr_ref, ga_ref, x_ref, mod_ref, gain_ref,
              wr_ref, wa_ref, wo_ref, wrt_ref, x1_ref, hslab_ref, hrow_ref, logit_ref, slab_scr):
    tm = x_ref.shape[0]
    shift_f, scale_f, gate_m = mod_ref[0, 3:4, :], mod_ref[0, 4:5, :], mod_ref[0, 2:3, :]
    for r in range(tm // MIX_ROWS):
        rows = pl.ds(r * MIX_ROWS, MIX_ROWS)
        y_ret = jnp.dot(ret_ref[rows, :], wr_ref[...], preferred_element_type=F32)
        y_att = jnp.dot(att_ref[rows, :], wa_ref[...], preferred_element_type=F32)
        merged = (_sigmoid(gr_ref[rows, :].astype(F32)) * y_ret
                  + _sigmoid(ga_ref[rows, :].astype(F32)) * y_att)
        mixed = jnp.dot(merged.astype(BF16), wo_ref[...], preferred_element_type=F32)
        x1 = x_ref[rows, :] + gate_m * mixed
        x1_ref[rows, :] = x1
        h = _rms(x1, gain_ref[...]) * (1.0 + scale_f) + shift_f
        hrow_ref[rows, :] = h.astype(BF16)
        _rows_to_tiles(h, slab_scr, hslab_ref, first=r * MIX_ROWS)
        h_hi = h.astype(BF16)
        h_lo = (h - h_hi.astype(F32)).astype(BF16)
        logit_ref[rows, :] = (jnp.dot(h_hi, wrt_ref[0], preferred_element_type=F32)
                              + (jnp.dot(h_lo, wrt_ref[0], preferred_element_type=F32)
                                 + jnp.dot(h_hi, wrt_ref[1], preferred_element_type=F32)))


def _mix(ret2, att2, proj2, x2, mod3, gain, wr, wa, wo, w_router, seq, tm=512):
    t, d = x2.shape
    gate_base = (4 * RET_WIDTH + 3 * ATT_WIDTH) // d
    row = lambda i: (i, 0)
    const = lambda i: (0, 0)
    return pl.pallas_call(
        _mix_body,
        grid=(t // tm,),
        in_specs=[pl.BlockSpec((tm, d), row), pl.BlockSpec((tm, d), row),
                  pl.BlockSpec((tm, d), lambda i: (i, gate_base)),
                  pl.BlockSpec((tm, d), lambda i: (i, gate_base + 1)),
                  pl.BlockSpec((tm, d), row),
                  pl.BlockSpec((1, 6, d), lambda i: ((i * tm) // seq, 0, 0)),
                  pl.BlockSpec((1, d), const),
                  pl.BlockSpec((d, d), const), pl.BlockSpec((d, d), const),
                  pl.BlockSpec((d, d), const),
                  pl.BlockSpec((2, d, N_EXPERTS), lambda i: (0, 0, 0))],
        out_specs=[pl.BlockSpec((tm, d), row),
                   pl.BlockSpec((tm, SLABS, LANES), lambda i: (i, 0, 0)),
                   pl.BlockSpec((tm, d), row),
                   pl.BlockSpec((tm, N_EXPERTS), row)],
        out_shape=[jax.ShapeDtypeStruct((t, d), F32),
                   jax.ShapeDtypeStruct((t, SLABS, LANES), BF16),
                   jax.ShapeDtypeStruct((t, d), BF16),
                   jax.ShapeDtypeStruct((t, N_EXPERTS), F32)],
        scratch_shapes=[pltpu.VMEM((tm * SLABS, LANES), F32)],
        compiler_params=_cparams(("arbitrary",), 56),
        name="mix",
    )(ret2, att2, proj2, proj2, x2, mod3, gain, wr, wa, wo, w_router)


def _route_body(logit_ref, bias_ref, eidx_ref, gate_ref, rank_ref, cnt_ref, carry_scr, tri_scr):
    tn = logit_ref.shape[0]
    per_group = N_EXPERTS // N_GROUPS

    @pl.when(pl.program_id(0) == 0)
    def _():
        carry_scr[...] = jnp.zeros_like(carry_scr)
        r = lax.broadcasted_iota(I32, (tn, tn), 0)
        c = lax.broadcasted_iota(I32, (tn, tn), 1)
        tri_scr[...] = jnp.where(r < c, 1.0, 0.0).astype(BF16)

    scores = _sigmoid(logit_ref[...]).T
    choice = scores + jnp.concatenate([bias_ref[...]] * (tn // LANES), axis=1)
    neg_inf = -jnp.inf

    sub = lax.broadcasted_iota(I32, (per_group, tn), 0).astype(F32)
    group_score = []
    for g in range(N_GROUPS):
        cg = choice[g * per_group:(g + 1) * per_group, :]
        m1 = jnp.max(cg, axis=0, keepdims=True)
        first = jnp.min(jnp.where(cg == m1, sub, float(per_group)), axis=0, keepdims=True)
        m2 = jnp.max(jnp.where(sub == first, neg_inf, cg), axis=0, keepdims=True)
        group_score.append(m1 + m2)

    masked = []
    for a in range(N_GROUPS):
        beaten = jnp.zeros((1, tn), F32)
        for b in range(N_GROUPS):
            if b == a:
                continue
            wins = (group_score[b] >= group_score[a]) if b < a else (group_score[b] > group_score[a])
            beaten = beaten + jnp.where(wins, 1.0, 0.0)
        keep = beaten < float(TOPK_GROUPS)
        masked.append(jnp.where(keep, choice[a * per_group:(a + 1) * per_group, :], neg_inf))
    work = jnp.concatenate(masked, axis=0)

    eid = lax.broadcasted_iota(I32, (N_EXPERTS, tn), 0).astype(F32)
    picked = jnp.zeros((N_EXPERTS, tn), F32)
    idx_rows, gate_rows = [], []
    for _ in range(TOP_K):
        m = jnp.max(work, axis=0, keepdims=True)
        idx = jnp.min(jnp.where(work == m, eid, float(N_EXPERTS)), axis=0, keepdims=True)
        sel = eid == idx
        gate_rows.append(jnp.sum(jnp.where(sel, scores, 0.0), axis=0, keepdims=True))
        picked = picked + jnp.where(sel, 1.0, 0.0)
        work = jnp.where(sel, neg_inf, work)
        idx_rows.append(idx)

    before = (jnp.dot(picked.astype(BF16), tri_scr[...], preferred_element_type=F32)
              + carry_scr[:, 0:1])
    rank_rows = [jnp.sum(jnp.where(eid == idx, before, 0.0), axis=0, keepdims=True)
                 for idx in idx_rows]
    carry = carry_scr[...] + jnp.sum(picked, axis=1, keepdims=True)
    carry_scr[...] = carry
    cnt_ref[...] = carry

    gates = jnp.concatenate(gate_rows, axis=0)
    gates = gates / jnp.sum(gates, axis=0, keepdims=True) * ROUTED_SCALE
    eidx_ref[...] = jnp.concatenate(idx_rows, axis=0).astype(I32)
    gate_ref[...] = gates
    rank_ref[...] = jnp.concatenate(rank_rows, axis=0).astype(I32)


def _route(logits, bias_b, tn=512):
    t = logits.shape[0]
    tok = lambda i: (0, i)
    return pl.pallas_call(
        _route_body,
        grid=(t // tn,),
        in_specs=[pl.BlockSpec((tn, N_EXPERTS), lambda i: (i, 0)),
                  pl.BlockSpec((N_EXPERTS, LANES), lambda i: (0, 0))],
        out_specs=[pl.BlockSpec((TOP_K, tn), tok), pl.BlockSpec((TOP_K, tn), tok),
                   pl.BlockSpec((TOP_K, tn), tok),
                   pl.BlockSpec((N_EXPERTS, LANES), lambda i: (0, 0))],
        out_shape=[jax.ShapeDtypeStruct((TOP_K, t), I32), jax.ShapeDtypeStruct((TOP_K, t), F32),
                   jax.ShapeDtypeStruct((TOP_K, t), I32),
                   jax.ShapeDtypeStruct((N_EXPERTS, LANES), F32)],
        scratch_shapes=[pltpu.VMEM((N_EXPERTS, LANES), F32), pltpu.VMEM((tn, tn), BF16)],
        compiler_params=_cparams(("arbitrary",), 48),
        name="route",
    )(logits, bias_b)


def _dest_body(start_ref, eidx_ref, rank_ref, dest_ref):
    e = eidx_ref[...]

    def body(j, acc):
        return jnp.where(e == j, start_ref[j], acc)

    dest_ref[...] = rank_ref[...] + lax.fori_loop(0, N_EXPERTS, body, jnp.zeros_like(e))


def _dest(pad_start, eidx, rank, tn=2048):
    t = eidx.shape[1]
    tok = lambda i, s: (0, i)
    return pl.pallas_call(
        _dest_body,
        grid_spec=pltpu.PrefetchScalarGridSpec(
            num_scalar_prefetch=1,
            grid=(t // tn,),
            in_specs=[pl.BlockSpec((TOP_K, tn), tok), pl.BlockSpec((TOP_K, tn), tok)],
            out_specs=pl.BlockSpec((TOP_K, tn), tok)),
        out_shape=jax.ShapeDtypeStruct((TOP_K, t), I32),
        compiler_params=_cparams(("arbitrary",), 32),
        name="dest",
    )(pad_start, eidx, rank)


def _dispatch_body(dest_hbm, h_ref, xs_hbm, dest_smem, idx_sem, row_sem):
    tt = dest_smem.shape[0] // TOP_K
    idx_copy = pltpu.make_async_copy(dest_hbm.at[pl.program_id(0)], dest_smem, idx_sem)
    idx_copy.start()
    idx_copy.wait()

    def issue(t, carry):
        for k in range(TOP_K):
            pltpu.make_async_copy(h_ref.at[t], xs_hbm.at[dest_smem[t * TOP_K + k]],
                                  row_sem).start(priority=k % 2)
        return carry

    lax.fori_loop(0, tt, issue, 0)

    for k in range(TOP_K):
        pltpu.make_async_copy(h_ref, xs_hbm.at[pl.ds(0, tt)], row_sem).wait()


def _dispatch(dest_tiles, htiles, n_rows):
    n_tiles, width = dest_tiles.shape
    tt = width // TOP_K
    return pl.pallas_call(
        _dispatch_body,
        grid=(n_tiles,),
        in_specs=[pl.BlockSpec(memory_space=pl.ANY),
                  pl.BlockSpec((tt, SLABS, LANES), lambda i: (i, 0, 0))],
        out_specs=pl.BlockSpec(memory_space=pl.ANY),
        out_shape=jax.ShapeDtypeStruct((n_rows, SLABS, LANES), BF16),
        scratch_shapes=[pltpu.SMEM((width,), I32), pltpu.SemaphoreType.DMA, pltpu.SemaphoreType.DMA],
        compiler_params=_cparams(("arbitrary",), 32),
        name="dispatch",
    )(dest_tiles, htiles)


def _experts_body(bexp_ref, bnew_ref, nused_ref, bnext_ref, bslot_ref, xs_hbm, wg_hbm, wu_hbm, wd_hbm,
                  ys_ref, wgu_s, wd_s, stage_scr, xring, wg_buf, wu_buf, wd_buf, ring_sem, w_sem):
    i = pl.program_id(0)
    hid = wd_s.shape[0]
    n_used = nused_ref[0]

    def block_copy(j):
        slot = j % EXPERT_RING
        src = xs_hbm.at[pl.ds(pl.multiple_of(j * MOE_BLOCK, MOE_BLOCK), MOE_BLOCK)]
        return pltpu.make_async_copy(src, xring.at[slot], ring_sem.at[slot])

    def weight_copies(expert, slot):
        return [pltpu.make_async_copy(src.at[expert], dst.at[slot], w_sem.at[slot])
                for src, dst in ((wg_hbm, wg_buf), (wu_hbm, wu_buf), (wd_hbm, wd_buf))]

    @pl.when(i == 0)
    def _():
        for copy in weight_copies(bexp_ref[0], 0):
            copy.start()
        for j in range(EXPERT_RING - 1):
            @pl.when(j < n_used)
            def _(j=j):
                block_copy(j).start()

    @pl.when(i + EXPERT_RING - 1 < n_used)
    def _():
        block_copy(i + EXPERT_RING - 1).start()

    @pl.when(i < n_used)
    def _():
        @pl.when(bnew_ref[i] == 1)
        def _():
            slot = bslot_ref[i]
            for copy in weight_copies(bexp_ref[i], slot):
                copy.wait()
            nxt = bnext_ref[i]

            @pl.when(nxt >= 0)
            def _():
                for copy in weight_copies(nxt, 1 - slot):
                    copy.start()

            wgu_s[:, :hid] = wg_buf[slot].astype(BF16)
            wgu_s[:, hid:] = wu_buf[slot].astype(BF16)
            wd_s[...] = wd_buf[slot].astype(BF16)

        block_copy(i).wait()
        x = _tiles_to_rows(xring[i % EXPERT_RING], stage_scr).astype(BF16)
        gu = jnp.dot(x, wgu_s[...], preferred_element_type=F32)
        hg, hu = gu[:, :hid], gu[:, hid:]
        y = jnp.dot((_silu(hg) * hu).astype(BF16), wd_s[...], preferred_element_type=F32)
        _rows_to_tiles(y, stage_scr, ys_ref)


def _experts(block_expert, block_new, n_used, block_next, block_slot, xs, w_gate, w_up, w_down):
    n_blocks = block_expert.shape[0]
    d, hid = w_gate.shape[1], w_gate.shape[2]
    tile_block = (MOE_BLOCK, SLABS, LANES)
    hbm = pl.BlockSpec(memory_space=pl.ANY)

    def blk(i, be, bn, nu, bx, bs):
        return (jnp.minimum(i, nu[0] - 1), 0, 0)

    return pl.pallas_call(
        _experts_body,
        grid_spec=pltpu.PrefetchScalarGridSpec(
            num_scalar_prefetch=5,
            grid=(n_blocks,),
            in_specs=[hbm, hbm, hbm, hbm],
            out_specs=pl.BlockSpec(tile_block, blk),
            scratch_shapes=[pltpu.VMEM((d, 2 * hid), BF16), pltpu.VMEM((hid, d), BF16),
                            pltpu.VMEM((MOE_BLOCK * SLABS, LANES), F32),
                            pltpu.VMEM((EXPERT_RING,) + tile_block, BF16),
                            pltpu.VMEM((2, d, hid), F32), pltpu.VMEM((2, d, hid), F32),
                            pltpu.VMEM((2, hid, d), F32),
                            pltpu.SemaphoreType.DMA((EXPERT_RING,)),
                            pltpu.SemaphoreType.DMA((2,))]),
        out_shape=jax.ShapeDtypeStruct(xs.shape, BF16),
        compiler_params=_cparams(("arbitrary",), 32),
        name="experts",
    )(block_expert, block_new, n_used, block_next, block_slot, xs, w_gate, w_up, w_down)


def _combine_body(dest_hbm, gate_hbm, ys_hbm, x1_ref, h_ref, wsg_ref, wsu_ref, wsd_ref,
                  mod_ref, gain_ref, o_ref, dest_smem0, dest_smem1, gate_smem, buf, shared_scr,
                  routed_scr, idx_sem, gate_sem, row_sem, *, final_norm):
    tc = x1_ref.shape[0]
    i = pl.program_id(0)
    has_next = i + 1 < pl.num_programs(0)
    cur = i % 2
    dest_smem = (dest_smem0, dest_smem1)

    def index_copy(tile, slot):
        return pltpu.make_async_copy(dest_hbm.at[tile], dest_smem[slot], idx_sem)

    def start_rows(slot):
        def issue(t, carry):
            for k in range(TOP_K):
                pltpu.make_async_copy(ys_hbm.at[dest_smem[slot][t * TOP_K + k]],
                                      buf.at[slot, t * TOP_K + k],
                                      row_sem.at[slot]).start(priority=k % 2)
            return carry

        lax.fori_loop(0, tc, issue, 0)

    gate_copy = pltpu.make_async_copy(gate_hbm.at[i], gate_smem, gate_sem)
    gate_copy.start()

    @pl.when(i == 0)
    def _():
        index_copy(0, 0).start()
        index_copy(0, 0).wait()
        start_rows(0)

    for slot in range(2):
        @pl.when(jnp.logical_and(has_next, cur != slot))
        def _(slot=slot):
            index_copy(i + 1, slot).start()

    h = h_ref[...]
    hid = (_silu(jnp.dot(h, wsg_ref[...], preferred_element_type=F32))
           * jnp.dot(h, wsu_ref[...], preferred_element_type=F32))
    shared_scr[...] = jnp.dot(hid.astype(BF16), wsd_ref[...], preferred_element_type=F32)

    for slot in range(2):
        @pl.when(jnp.logical_and(has_next, cur != slot))
        def _(slot=slot):
            index_copy(i + 1, slot).wait()
            start_rows(slot)

    pltpu.make_async_copy(ys_hbm.at[pl.ds(0, tc * TOP_K)], buf.at[cur], row_sem.at[cur]).wait()
    gate_copy.wait()

    def weigh(g, carry):
        for u in range(COMBINE_UNROLL):
            t = g * COMBINE_UNROLL + u
            acc = gate_smem[t * TOP_K] * buf[cur, t * TOP_K].astype(F32)
            for k in range(1, TOP_K):
                acc = acc + gate_smem[t * TOP_K + k] * buf[cur, t * TOP_K + k].astype(F32)
            routed_scr[pl.ds(pl.multiple_of(t * SLABS, SLABS), SLABS), :] = acc
        return carry

    lax.fori_loop(0, tc // COMBINE_UNROLL, weigh, 0)

    gate_f = mod_ref[0, 5:6, :]
    gain = gain_ref[...]

    def finish(c, carry):
        r0 = pl.multiple_of(c * COMBINE_ROWS, COMBINE_ROWS)
        rows = pl.ds(r0, COMBINE_ROWS)
        routed = jnp.concatenate(
            [routed_scr[pl.ds(r0 * SLABS + s, COMBINE_ROWS, stride=SLABS), :]
             for s in range(SLABS)], axis=1)
        x2 = x1_ref[rows, :] + gate_f * (shared_scr[rows, :] + routed)
        o_ref[rows, :] = _rms(x2, gain) if final_norm else x2
        return carry

    lax.fori_loop(0, tc // COMBINE_ROWS, finish, 0)


def _combine(dest_tiles, gate_tiles, ys, x1, hrow, wsg, wsu, wsd, mod3, gain, seq, tc, final_norm):
    t, d = x1.shape
    hid = wsg.shape[1]
    row = lambda i: (i, 0)
    const = lambda i: (0, 0)
    return pl.pallas_call(
        functools.partial(_combine_body, final_norm=final_norm),
        grid=(t // tc,),
        in_specs=[pl.BlockSpec(memory_space=pl.ANY),
                  pl.BlockSpec(memory_space=pl.ANY),
                  pl.BlockSpec(memory_space=pl.ANY),
                  pl.BlockSpec((tc, d), row), pl.BlockSpec((tc, d), row),
                  pl.BlockSpec((d, hid), const), pl.BlockSpec((d, hid), const),
                  pl.BlockSpec((hid, d), const),
                  pl.BlockSpec((1, 6, d), lambda i: ((i * tc) // seq, 0, 0)),
                  pl.BlockSpec((1, d), const)],
        out_specs=pl.BlockSpec((tc, d), row),
        out_shape=jax.ShapeDtypeStruct((t, d), F32),
        scratch_shapes=[pltpu.SMEM((TOP_K * tc,), I32), pltpu.SMEM((TOP_K * tc,), I32),
                        pltpu.SMEM((TOP_K * tc,), F32),
                        pltpu.VMEM((2, tc * TOP_K, SLABS, LANES), BF16),
                        pltpu.VMEM((tc, d), F32),
                        pltpu.VMEM((tc * SLABS, LANES), F32),
                        pltpu.SemaphoreType.DMA, pltpu.SemaphoreType.DMA,
                        pltpu.SemaphoreType.DMA((2,))],
        compiler_params=_cparams(("arbitrary",), 56),
        name="combine",
    )(dest_tiles, gate_tiles, ys, x1, hrow, wsg, wsu, wsd, mod3, gain)


def _split_bf16(w):
    hi = w.astype(BF16)
    lo = (w.astype(F32) - hi.astype(F32)).astype(BF16)
    return jnp.stack([hi, lo], axis=0)


def _tile_major(a, tile):
    k, t = a.shape
    return a.reshape(k, t // tile, tile).transpose(1, 2, 0).reshape(t // tile, tile * k)


def kernel(x, c, w_ada, b_ada, norm_mix, w_in, ret_decay, t5_bias, w_ret_up, w_att_up, w_o,
           norm_ffn, w_router, router_bias, w_gate, w_up, w_down, ws_gate, ws_up, ws_down, norm_final):
    bsz, seq, d = x.shape
    depth = w_ada.shape[0]
    t = bsz * seq
    assert d == D_MODEL and seq % (ATT_KWIN * DILATION_PATTERNS[-1][1]) == 0

    half = RET_HEAD_DIM // 2
    inv_freq = ROPE_BASE ** (-jnp.arange(half, dtype=F32) / half)
    ang = jnp.arange(seq, dtype=F32)[:, None] * inv_freq[None, :]
    cos, sin = jnp.cos(ang), jnp.sin(ang)
    bias_tab = _attention_bias(t5_bias)

    n_assign = t * TOP_K
    n_blocks = -(-n_assign // MOE_BLOCK) + N_EXPERTS
    n_rows = n_blocks * MOE_BLOCK
    disp_tile = min(2048, t)
    comb_tile = min(256, t)

    x2 = x.reshape(t, d)
    for layer in range(depth):
        mod3 = _adaln(c, w_ada[layer], b_ada[layer]).reshape(bsz, 6, d)
        proj = _inproj(x2, norm_mix[layer].reshape(1, d), mod3, w_in[layer].astype(BF16), seq)
        proj3 = proj.reshape(bsz, seq, PROJ_WIDTH)
        log_gamma = jnp.log1p(-jnp.exp(ret_decay[layer].astype(F32)))
        ret = _retention(proj3, log_gamma, cos, sin)
        att = _attention(proj3, bias_tab)
        x1, hslab, hrow, logits = _mix(
            ret.reshape(t, RET_WIDTH), att.reshape(t, ATT_WIDTH), proj, x2, mod3,
            norm_ffn[layer].reshape(1, d), w_ret_up[layer].astype(BF16),
            w_att_up[layer].astype(BF16), w_o[layer].astype(BF16), _split_bf16(w_router[layer]),
            seq)

        bias_b = jnp.broadcast_to(router_bias[layer].astype(F32)[:, None], (N_EXPERTS, LANES))
        eidx, gate, rank, counts = _route(logits, bias_b)

        counts = counts[:, 0].astype(I32)
        padded = (counts + MOE_BLOCK - 1) // MOE_BLOCK * MOE_BLOCK
        pad_end = jnp.cumsum(padded)
        pad_start = pad_end - padded
        block_row = jnp.arange(n_blocks, dtype=I32) * MOE_BLOCK
        block_expert = jnp.minimum(
            jnp.sum((pad_end[None, :] <= block_row[:, None]).astype(I32), axis=1), N_EXPERTS - 1)
        block_new = jnp.concatenate(
            [jnp.ones((1,), I32), (block_expert[1:] != block_expert[:-1]).astype(I32)])
        n_used = (pad_end[-1:] // MOE_BLOCK).astype(I32)

        dest = _dest(pad_start.astype(I32), eidx, rank)
        xs = _dispatch(_tile_major(dest, disp_tile), hslab, n_rows)
        expert_ids = jnp.arange(N_EXPERTS, dtype=I32)
        later = jnp.logical_and(expert_ids[None, :] > block_expert[:, None], (padded > 0)[None, :])
        block_next = jnp.min(jnp.where(later, expert_ids[None, :], N_EXPERTS), axis=1)
        block_next = jnp.where(block_next == N_EXPERTS, -1, block_next).astype(I32)
        block_slot = ((jnp.cumsum(block_new) - 1) % 2).astype(I32)
        ys = _experts(block_expert, block_new, n_used, block_next, block_slot, xs,
                      w_gate[layer], w_up[layer], w_down[layer])
        x2 = _combine(_tile_major(dest, comb_tile), _tile_major(gate, comb_tile), ys,
                      x1, hrow, ws_gate[layer].astype(BF16), ws_up[layer].astype(BF16),
                      ws_down[layer].astype(BF16), mod3, norm_final.reshape(1, d), seq, comb_tile,
                      final_norm=(layer == depth - 1))
    return x2.reshape(bsz, seq, d)
```
